```python
import jax
import jax.numpy as jnp
from jax import lax
import numpy as np

D_MODEL = 2048
BATCH = 8
SEQ = 2048
DEPTH = 2

HEAD_DIM = 128
HEADS = 4
GROUP_W = HEADS * HEAD_DIM
D_MIX = 4 * GROUP_W
N_SOFTMAX_HEADS = 2 * HEADS

MLSTM_CHUNK = 64
MLSTM_CONV = 4
MOBA_BLOCK = 256
MOBA_TOPK = 3
MOBA_QC = 16
RET_CHUNK = 128
NSA_CMP_LEN = 32
NSA_CMP_STRIDE = 16
NSA_SEL_BLOCK = 64
NSA_SEL_TOPN = 4
NSA_WINDOW = 512
NSA_QC = 32
WIN_QBLOCK = 128
N_GROUPS = 4
EXPERTS_PER_GROUP = 8
N_EXPERTS = N_GROUPS * EXPERTS_PER_GROUP
TOP_K_IN_GROUP = 2
D_EXPERT = 512
MOE_BLOCK = 128

NORM_EPS = 1e-6
NEG = -1e30
BIG = 1e9

IN_LAYOUT = (
    ('m_q', GROUP_W), ('m_k', GROUP_W), ('m_v', GROUP_W), ('m_o', GROUP_W), ('m_i', HEADS), ('m_f', HEADS),
    ('b_q', GROUP_W), ('b_k', GROUP_W), ('b_v', GROUP_W),
    ('r_q', GROUP_W), ('r_k', GROUP_W), ('r_v', GROUP_W), ('r_g', GROUP_W),
    ('n_q', GROUP_W), ('n_kc', GROUP_W), ('n_vc', GROUP_W), ('n_ks', GROUP_W), ('n_vs', GROUP_W),
    ('n_kw', GROUP_W), ('n_vw', GROUP_W), ('n_g', 3 * HEADS),
)
D_IN = sum(w for _, w in IN_LAYOUT)

kernel_name = 'hybrid_mlstm_moba_retnet_nsa_hmoe'


def _rms(z, g):
    zf = z.astype(jnp.float32)
    zf = zf * lax.rsqrt(jnp.mean(zf * zf, axis=-1, keepdims=True) + NORM_EPS)
    return zf * g.astype(jnp.float32)


def to_heads(z):
    b, t, _ = z.shape
    return z.reshape(b, t, -1, HEAD_DIM).transpose(0, 2, 1, 3)


def from_heads(z):
    b, h, t, d = z.shape
    return z.transpose(0, 2, 1, 3).reshape(b, t, h * d)


def alibi_slopes():
    n = N_SOFTMAX_HEADS
    return jnp.exp2(-8.0 * jnp.arange(1, n + 1, dtype=jnp.float32) / n)


def causal_dwconv(x, w):
    k = w.shape[0]
    return lax.conv_general_dilated(
        x, w.astype(x.dtype)[:, None, :], window_strides=(1,), padding=((k - 1, 0),),
        dimension_numbers=('NWC', 'WIO', 'NWC'), feature_group_count=x.shape[-1])


def mlstm_mixer(q, k, v, o, ig, fg, conv_w, gate_b, out_g):
    f32 = jnp.float32
    qk = jax.nn.silu(causal_dwconv(jnp.concatenate([q, k], axis=-1), conv_w))
    q, k = jnp.split(qk, 2, axis=-1)
    q = to_heads(q).astype(f32) * HEAD_DIM ** -0.5
    k = to_heads(k).astype(f32)
    v = to_heads(v).astype(f32)
    i_pre = jnp.swapaxes((ig + gate_b[0]).astype(f32), 1, 2)
    log_f = jax.nn.log_sigmoid(jnp.swapaxes((fg + gate_b[1]).astype(f32), 1, 2))
    b, h, t, d = q.shape
    nc = t // MLSTM_CHUNK
    causal = jnp.tril(jnp.ones((MLSTM_CHUNK, MLSTM_CHUNK), dtype=bool))

    def chunked(z):
        return jnp.moveaxis(z.reshape(b, h, nc, MLSTM_CHUNK, *z.shape[3:]), 2, 0)

    def step(carry, inp):
        c_st, n_st, m_st = carry
        qc, kc, vc, ic, fc = inp
        a = jnp.cumsum(fc, axis=-1)
        log_d = jnp.where(causal, a[..., :, None] - a[..., None, :] + ic[..., None, :], NEG)
        m_inter = a + m_st[..., None]
        m_row = jnp.maximum(m_inter, jnp.max(log_d, axis=-1))
        s = jnp.einsum('bhjd,bhsd->bhjs', qc, kc) * jnp.exp(log_d - m_row[..., None])
        w_inter = jnp.exp(m_inter - m_row)
        num = (jnp.einsum('bhjs,bhse->bhje', s, vc)
               + w_inter[..., None] * jnp.einsum('bhjd,bhde->bhje', qc, c_st))
        den = jnp.sum(s, axis=-1) + w_inter * jnp.einsum('bhjd,bhd->bhj', qc, n_st)
        h_out = num / jnp.maximum(jnp.abs(den), jnp.exp(-m_row))[..., None]
        a_last = a[..., -1]
        w_log = a_last[..., None] - a + ic
        m_new = jnp.maximum(a_last + m_st, jnp.max(w_log, axis=-1))
        w = jnp.exp(w_log - m_new[..., None])
        decay = jnp.exp(a_last + m_st - m_new)
        c_st = decay[..., None, None] * c_st + jnp.einsum('bhs,bhsd,bhse->bhde', w, kc, vc)
        n_st = decay[..., None] * n_st + jnp.einsum('bhs,bhsd->bhd', w, kc)
        return (c_st, n_st, m_new), h_out

    carry0 = (jnp.zeros((b, h, d, d), f32), jnp.zeros((b, h, d), f32), jnp.zeros((b, h), f32))
    _, hs = lax.scan(step, carry0, (chunked(q), chunked(k), chunked(v), chunked(i_pre), chunked(log_f)))
    hs = jnp.moveaxis(hs, 0, 2).reshape(b, h, t, d)
    return from_heads(_rms(hs, out_g)) * jax.nn.sigmoid(o.astype(f32))


def moba_mixer(q, k, v, qk_g, slopes):
    f32 = jnp.float32
    q = _rms(to_heads(q), qk_g[0])
    k = _rms(to_heads(k), qk_g[1])
    v = to_heads(v).astype(f32)
    b, h, t, d = q.shape
    scale = d ** -0.5
    nb = -(-t // MOBA_BLOCK)
    tp = nb * MOBA_BLOCK
    padw = ((0, 0), (0, 0), (0, tp - t), (0, 0))
    q, k, v = jnp.pad(q, padw), jnp.pad(k, padw), jnp.pad(v, padw)
    kb = k.reshape(b, h, nb, MOBA_BLOCK, d)
    vb = v.reshape(b, h, nb, MOBA_BLOCK, d)
    qblk = jnp.arange(tp, dtype=jnp.int32) // MOBA_BLOCK
    blk = jnp.arange(nb, dtype=jnp.int32)
    gate = jnp.einsum('bhtd,bhnd->bhtn', q, jnp.mean(kb, axis=3))
    gate = jnp.where(blk[None, :] < qblk[:, None], gate, NEG)
    n_sel = min(MOBA_TOPK, nb)
    _, sel = lax.top_k(gate, n_sel)
    nq = tp // MOBA_QC
    q_ch = jnp.moveaxis(q.reshape(b, h, nq, MOBA_QC, d), 2, 0)
    sel_ch = jnp.moveaxis(sel.reshape(b, h, nq, MOBA_QC, n_sel), 2, 0)
    starts = jnp.arange(nq, dtype=jnp.int32) * MOBA_QC
    bi = jnp.arange(b)[:, None, None, None]
    hi = jnp.arange(h)[None, :, None, None]
    slope = slopes[None, :, None, None]
    offs = jnp.arange(MOBA_BLOCK, dtype=jnp.int32)

    def chunk_attn(args):
        qc, sc, start = args
        tq = start + jnp.arange(MOBA_QC, dtype=jnp.int32)
        own = start // MOBA_BLOCK
        k_own = lax.dynamic_index_in_dim(kb, own, axis=2, keepdims=False)
        v_own = lax.dynamic_index_in_dim(vb, own, axis=2, keepdims=False)
        dist_own = (tq[:, None] - (own * MOBA_BLOCK + offs)[None, :]).astype(f32)
        s_own = jnp.einsum('bhqd,bhkd->bhqk', qc, k_own) * scale - slope * dist_own
        s_own = jnp.where(dist_own >= 0, s_own, NEG)
        k_sel = kb[bi, hi, sc]
        v_sel = vb[bi, hi, sc]
        dist_sel = (tq[None, None, :, None, None] - (sc[..., None] * MOBA_BLOCK + offs)).astype(f32)
        s_sel = jnp.einsum('bhqd,bhqnkd->bhqnk', qc, k_sel) * scale - slope[..., None] * dist_sel
        s_sel = jnp.where((sc < own)[..., None], s_sel, NEG)
        s_all = jnp.concatenate([s_own, s_sel.reshape(b, h, MOBA_QC, n_sel * MOBA_BLOCK)], axis=-1)
        p = jax.nn.softmax(s_all, axis=-1)
        p_own = p[..., :MOBA_BLOCK]
        p_sel = p[..., MOBA_BLOCK:].reshape(b, h, MOBA_QC, n_sel, MOBA_BLOCK)
        return (jnp.einsum('bhqk,bhkd->bhqd', p_own, v_own)
                + jnp.einsum('bhqnk,bhqnkd->bhqd', p_sel, v_sel))

    out = lax.map(chunk_attn, (q_ch, sel_ch, starts))
    out = jnp.moveaxis(out, 0, 2).reshape(b, h, tp, d)[:, :, :t]
    return from_heads(out)


def retention_mixer(q, k, v, g, out_g):
    f32 = jnp.float32
    q = to_heads(q).astype(f32)
    k = to_heads(k).astype(f32) * HEAD_DIM ** -0.5
    v = to_heads(v).astype(f32)
    b, h, t, d = q.shape
    lg = jnp.log(1.0 - jnp.exp2(-5.0 - jnp.arange(h, dtype=f32)))
    cl = RET_CHUNK
    nc = t // cl
    qc = q.reshape(b, h, nc, cl, d)
    kc = k.reshape(b, h, nc, cl, d)
    vc = v.reshape(b, h, nc, cl, d)
    j = jnp.arange(cl, dtype=f32)
    diff = j[:, None] - j[None, :]
    decay_in = jnp.where(diff >= 0, jnp.exp(lg[:, None, None] * jnp.maximum(diff, 0.0)), 0.0)
    scores = jnp.einsum('bhcid,bhcjd->bhcij', qc, kc) * decay_in[None, :, None]
    o_intra = jnp.einsum('bhcij,bhcjd->bhcid', scores, vc)
    zeta = jnp.exp(lg[:, None] * (cl - 1 - j)[None, :])
    kv = jnp.einsum('bhcjd,hj,bhcje->bhcde', kc, zeta, vc)
    g_chunk = jnp.exp(lg * cl)[None, :, None, None]

    def step(r, kv_c):
        return g_chunk * r + kv_c, r

    _, r_prev = lax.scan(step, jnp.zeros((b, h, d, d), f32), jnp.moveaxis(kv, 2, 0))
    r_prev = jnp.moveaxis(r_prev, 0, 2)
    xi = jnp.exp(lg[:, None] * (j + 1.0)[None, :])
    o_cross = jnp.einsum('bhcid,bhcde->bhcie', qc, r_prev) * xi[None, :, None, :, None]
    o = (o_intra + o_cross).reshape(b, h, t, d)
    return from_heads(_rms(o, out_g)) * jax.nn.silu(g.astype(f32))


def nsa_mixer(q, kc, vc, ks, vs, kw, vw, gates, q_g, k_g, cmp_pos, cmp_w1, cmp_w2, slopes):
    f32 = jnp.float32
    q = _rms(to_heads(q), q_g)
    b, h, t, d = q.shape
    scale = d ** -0.5
    slope = slopes[None, :, None, None]
    pos = jnp.arange(t, dtype=jnp.int32)

    def compress(z, pe, w1, w2):
        nsub = t // NSA_CMP_STRIDE
        zs = z.reshape(b, h, nsub, NSA_CMP_STRIDE, d)
        blocks = jnp.concatenate([zs[:, :, :-1], zs[:, :, 1:]], axis=3) + pe
        hid = jax.nn.gelu(jnp.einsum('bhnf,fe->bhne', blocks.reshape(b, h, nsub - 1, NSA_CMP_LEN * d), w1))
        return jnp.einsum('bhne,ed->bhnd', hid, w2)

    k_cmp = _rms(compress(to_heads(kc).astype(f32), cmp_pos[0], cmp_w1[0], cmp_w2[0]), k_g[0])
    v_cmp = compress(to_heads(vc).astype(f32), cmp_pos[1], cmp_w1[1], cmp_w2[1])
    n_cmp = k_cmp.shape[2]
    cmp_end = jnp.arange(n_cmp, dtype=jnp.int32) * NSA_CMP_STRIDE + NSA_CMP_LEN - 1
    dist_c = (pos[:, None] - cmp_end[None, :]).astype(f32)
    ok_c = dist_c >= 0
    s_c = jnp.where(ok_c, jnp.einsum('bhtd,bhnd->bhtn', q, k_cmp) * scale - slope * dist_c, NEG)
    m_c = jnp.max(s_c, axis=-1, keepdims=True)
    e_c = jnp.where(ok_c, jnp.exp(s_c - m_c), 0.0)
    p_c = e_c / jnp.maximum(jnp.sum(e_c, axis=-1, keepdims=True), 1e-30)
    o_cmp = jnp.einsum('bhtn,bhnd->bhtd', p_c, v_cmp)

    n_sel = t // NSA_SEL_BLOCK
    ci = jnp.arange(n_cmp, dtype=jnp.int32)[:, None]
    sj = jnp.arange(n_sel, dtype=jnp.int32)[None, :]
    overlap = ((ci * NSA_CMP_STRIDE <= sj * NSA_SEL_BLOCK + NSA_SEL_BLOCK - 1)
               & (ci * NSA_CMP_STRIDE + NSA_CMP_LEN - 1 >= sj * NSA_SEL_BLOCK)).astype(f32)
    imp = jnp.einsum('bhtn,ns->bhts', p_c, overlap)
    cur = (pos // NSA_SEL_BLOCK)[:, None]
    imp = jnp.where(sj == cur, BIG, jnp.where(sj < cur, imp, NEG))
    n_top = min(NSA_SEL_TOPN, n_sel)
    _, sidx = lax.top_k(imp, n_top)
    ks_b = _rms(to_heads(ks), k_g[1]).reshape(b, h, n_sel, NSA_SEL_BLOCK, d)
    vs_b = to_heads(vs).astype(f32).reshape(b, h, n_sel, NSA_SEL_BLOCK, d)
    nq = t // NSA_QC
    q_ch = jnp.moveaxis(q.reshape(b, h, nq, NSA_QC, d), 2, 0)
    s_ch = jnp.moveaxis(sidx.reshape(b, h, nq, NSA_QC, n_top), 2, 0)
    starts = jnp.arange(nq, dtype=jnp.int32) * NSA_QC
    bi = jnp.arange(b)[:, None, None, None]
    hi = jnp.arange(h)[None, :, None, None]
    offs = jnp.arange(NSA_SEL_BLOCK, dtype=jnp.int32)

    def chunk_attn(args):
        qc, sc, start = args
        tq = start + jnp.arange(NSA_QC, dtype=jnp.int32)
        k_gth = ks_b[bi, hi, sc]
        v_gth = vs_b[bi, hi, sc]
        dist = (tq[None, None, :, None, None] - (sc[..., None] * NSA_SEL_BLOCK + offs)).astype(f32)
        s = jnp.einsum('bhqd,bhqnkd->bhqnk', qc, k_gth) * scale - slope[..., None] * dist
        s = jnp.where(dist >= 0, s, NEG).reshape(b, h, NSA_QC, n_top * NSA_SEL_BLOCK)
        p = jax.nn.softmax(s, axis=-1).reshape(b, h, NSA_QC, n_top, NSA_SEL_BLOCK)
        return jnp.einsum('bhqnk,bhqnkd->bhqd', p, v_gth)

    o_slc = lax.map(chunk_attn, (q_ch, s_ch, starts))
    o_slc = jnp.moveaxis(o_slc, 0, 2).reshape(b, h, t, d)

    qb = WIN_QBLOCK
    nqb = t // qb
    nprev = NSA_WINDOW // qb

    def band(z):
        zp = jnp.pad(z, ((0, 0), (0, 0), (nprev * qb, 0), (0, 0))).reshape(b, h, nqb + nprev, qb, d)
        return jnp.concatenate([zp[:, :, i:i + nqb] for i in range(nprev + 1)], axis=3)

    k_band = band(_rms(to_heads(kw), k_g[2]))
    v_band = band(to_heads(vw).astype(f32))
    blk_i = jnp.arange(nqb, dtype=jnp.int32)[:, None]
    qpos = blk_i * qb + jnp.arange(qb, dtype=jnp.int32)[None, :]
    kpos = blk_i * qb - nprev * qb + jnp.arange((nprev + 1) * qb, dtype=jnp.int32)[None, :]
    dist_w = qpos[:, :, None] - kpos[:, None, :]
    ok_w = (dist_w >= 0) & (dist_w < NSA_WINDOW) & (kpos[:, None, :] >= 0)
    s_w = (jnp.einsum('bhnqd,bhnkd->bhnqk', q.reshape(b, h, nqb, qb, d), k_band) * scale
           - slopes[None, :, None, None, None] * dist_w.astype(f32))
    p_w = jax.nn.softmax(jnp.where(ok_w, s_w, NEG), axis=-1)
    o_win = jnp.einsum('bhnqk,bhnkd->bhnqd', p_w, v_band).reshape(b, h, t, d)

    g = jax.nn.sigmoid(gates.astype(f32)).reshape(b, t, 3, h)
    g = jnp.transpose(g, (2, 0, 3, 1))[..., None]
    return from_heads(g[0] * o_cmp + g[1] * o_slc + g[2] * o_win)


def grouped_experts(xf, eid, gate, w1, w3, w2):
    n, dm = xf.shape
    na = eid.shape[0]
    tok = jnp.arange(na, dtype=jnp.int32) // TOP_K_IN_GROUP
    order = jnp.argsort(eid)
    se = eid[order]
    counts = jnp.bincount(eid, length=N_EXPERTS)
    pcounts = (counts + MOE_BLOCK - 1) // MOE_BLOCK * MOE_BLOCK
    starts = jnp.cumsum(counts) - counts
    pends = jnp.cumsum(pcounts)
    pstarts = pends - pcounts
    dest = pstarts[se] + (jnp.arange(na, dtype=jnp.int32) - starts[se])
    cap = (-(-na // MOE_BLOCK) + N_EXPERTS) * MOE_BLOCK
    nblk = cap // MOE_BLOCK
    buf_tok = jnp.full((cap,), n, dtype=jnp.int32).at[dest].set(tok[order])
    buf_gate = jnp.zeros((cap,), jnp.float32).at[dest].set(gate[order])
    blk_e = jnp.minimum(jnp.searchsorted(pends, jnp.arange(nblk, dtype=jnp.int32) * MOE_BLOCK, side='right'),
                        N_EXPERTS - 1)
    xpad = jnp.concatenate([xf, jnp.zeros((1, dm), xf.dtype)], axis=0)
    xb = xpad[buf_tok].reshape(nblk, MOE_BLOCK, dm)

    def expert_block(args):
        xblk, e = args
        return (jax.nn.silu(xblk @ w1[e]) * (xblk @ w3[e])) @ w2[e]

    yb = lax.map(expert_block, (xb, blk_e)).reshape(cap, dm)
    yb = yb * buf_gate[:, None].astype(yb.dtype)
    return jnp.zeros((n + 1, dm), yb.dtype).at[buf_tok].add(yb)[:n]


def hier_moe(h, wg, bg, we, be, w1, w3, w2):
    f32 = jnp.float32
    b, t, dm = h.shape
    n = b * t
    xf = h.reshape(n, dm)
    lg = (xf @ wg + bg).astype(f32)
    pg = jax.nn.softmax(lg, axis=-1)
    _, grp = lax.top_k(lg, 1)
    grp_1h = jax.nn.one_hot(grp[:, 0], N_GROUPS, dtype=f32)
    p_grp = jnp.sum(pg * grp_1h, axis=-1, keepdims=True)
    le = (xf @ we + be).astype(f32).reshape(n, N_GROUPS, EXPERTS_PER_GROUP)
    le_g = jnp.einsum('ng,nge->ne', grp_1h, le)
    top_p, top_i = lax.top_k(jax.nn.softmax(le_g, axis=-1), TOP_K_IN_GROUP)
    top_p = top_p / jnp.sum(top_p, axis=-1, keepdims=True)
    gate = p_grp * top_p
    eid = grp * EXPERTS_PER_GROUP + top_i
    y = grouped_experts(xf, eid.reshape(-1), gate.reshape(-1), w1, w3, w2)
    return y.reshape(b, t, dm)


def hybrid_layer(x, c, norm1_g, norm2_g, ada_w, ada_b, w_in, mlstm_gate_b, mlstm_conv_w, mlstm_out_g,
                 moba_qk_g, ret_out_g, nsa_q_g, nsa_k_g, nsa_cmp_pos, nsa_cmp_w1, nsa_cmp_w2, w_out,
                 router_g_w, router_g_b, router_e_w, router_e_b, exp_w1, exp_w3, exp_w2):
    dtype = x.dtype
    mod = jax.nn.silu(c) @ ada_w + ada_b
    sh1, sc1, g1, sh2, sc2, g2 = jnp.split(mod[:, None, :], 6, axis=-1)
    h = (_rms(x, norm1_g) * (1.0 + sc1) + sh1).astype(dtype)
    z = h @ w_in
    p = {}
    off = 0
    for name, width in IN_LAYOUT:
        p[name] = z[..., off:off + width]
        off += width
    slopes = alibi_slopes()
    y_m = mlstm_mixer(p['m_q'], p['m_k'], p['m_v'], p['m_o'], p['m_i'], p['m_f'],
                      mlstm_conv_w, mlstm_gate_b, mlstm_out_g)
    y_b = moba_mixer(p['b_q'], p['b_k'], p['b_v'], moba_qk_g, slopes[0::2])
    y_r = retention_mixer(p['r_q'], p['r_k'], p['r_v'], p['r_g'], ret_out_g)
    y_n = nsa_mixer(p['n_q'], p['n_kc'], p['n_vc'], p['n_ks'], p['n_vs'], p['n_kw'], p['n_vw'], p['n_g'],
                    nsa_q_g, nsa_k_g, nsa_cmp_pos, nsa_cmp_w1, nsa_cmp_w2, slopes[1::2])
    y = jnp.concatenate([y_m, y_b, y_r, y_n], axis=-1).astype(dtype) @ w_out
    x = x + g1 * y
    h2 = (_rms(x, norm2_g) * (1.0 + sc2) + sh2).astype(dtype)
    return x + g2 * hier_moe(h2, router_g_w, router_g_b, router_e_w, router_e_b, exp_w1, exp_w3, exp_w2)


def setup_inputs(seed: int = 0) -> dict:
    key = jax.random.key(seed)
    keys = iter(jax.random.split(key, 40))

    def nrm(shape, scale):
        return jax.random.normal(next(keys), shape, jnp.float32) * scale

    L, D, H, DH = DEPTH, D_MODEL, HEADS, HEAD_DIM
    f_bias = jnp.linspace(3.0, 6.0, H, dtype=jnp.float32)[None, :] + nrm((L, H), 0.1)
    return {
        'x': nrm((BATCH, SEQ, D), 1.0),
        'c': nrm((BATCH, D), 1.0),
        'norm1_g': 1.0 + nrm((L, D), 0.02),
        'norm2_g': 1.0 + nrm((L, D), 0.02),
        'ada_w': nrm((L, D, 6 * D), 0.5 * D ** -0.5),
        'ada_b': nrm((L, 6 * D), 0.01),
        'w_in': nrm((L, D, D_IN), D ** -0.5),
        'mlstm_gate_b': jnp.stack([nrm((L, H), 0.1), f_bias], axis=1),
        'mlstm_conv_w': nrm((L, MLSTM_CONV, 2 * GROUP_W), MLSTM_CONV ** -0.5),
        'mlstm_out_g': 1.0 + nrm((L, DH), 0.02),
        'moba_qk_g': 1.0 + nrm((L, 2, DH), 0.02),
        'ret_out_g': 1.0 + nrm((L, DH), 0.02),
        'nsa_q_g': 1.0 + nrm((L, DH), 0.02),
        'nsa_k_g': 1.0 + nrm((L, 3, DH), 0.02),
        'nsa_cmp_pos': nrm((L, 2, NSA_CMP_LEN, DH), 0.02),
        'nsa_cmp_w1': nrm((L, 2, NSA_CMP_LEN * DH, DH), (NSA_CMP_LEN * DH) ** -0.5),
        'nsa_cmp_w2': nrm((L, 2, DH, DH), DH ** -0.5),
        'w_out': nrm((L, D_MIX, D), D_MIX ** -0.5),
        'router_g_w': nrm((L, D, N_GROUPS), D ** -0.5),
        'router_g_b': nrm((L, N_GROUPS), 0.01),
        'router_e_w': nrm((L, D, N_EXPERTS), D ** -0.5),
        'router_e_b': nrm((L, N_EXPERTS), 0.01),
        'exp_w1': nrm((L, N_EXPERTS, D, D_EXPERT), D ** -0.5),
        'exp_w3': nrm((L, N_EXPERTS, D, D_EXPERT), D ** -0.5),
        'exp_w2': nrm((L, N_EXPERTS, D_EXPERT, D), D_EXPERT ** -0.5),
    }


def reference(x, c, norm1_g, norm2_g, ada_w, ada_b, w_in, mlstm_gate_b, mlstm_conv_w, mlstm_out_g,
              moba_qk_g, ret_out_g, nsa_q_g, nsa_k_g, nsa_cmp_pos, nsa_cmp_w1, nsa_cmp_w2, w_out,
              router_g_w, router_g_b, router_e_w, router_e_b, exp_w1, exp_w3, exp_w2):
    for l in range(DEPTH):
        x = hybrid_layer(x, c, norm1_g[l], norm2_g[l], ada_w[l], ada_b[l], w_in[l], mlstm_gate_b[l],
                         mlstm_conv_w[l], mlstm_out_g[l], moba_qk_g[l], ret_out_g[l], nsa_q_g[l], nsa_k_g[l],
                         nsa_cmp_pos[l], nsa_cmp_w1[l], nsa_cmp_w2[l], w_out[l], router_g_w[l], router_g_b[l],
                         router_e_w[l], router_e_b[l], exp_w1[l], exp_w3[l], exp_w2[l])
    return x
```

```python
import functools

import jax
import jax.numpy as jnp
from jax import lax
from jax.experimental import pallas as pl
from jax.experimental.pallas import tpu as pltpu

F32 = jnp.float32
BF16 = jnp.bfloat16
MXU_DTYPE = jnp.bfloat16

D_MODEL = 2048
HEAD_DIM = 128
HEADS = 4
GROUP_W = HEADS * HEAD_DIM
N_WIDE_GROUPS = 18
D_WIDE = N_WIDE_GROUPS * GROUP_W
GATE_W = 128

MLSTM_CHUNK = 64
MLSTM_CONV = 4
MOBA_BLOCK = 256
MOBA_TOPK = 3
RET_CHUNK = 128
NSA_CMP_LEN = 32
NSA_CMP_STRIDE = 16
NSA_SEL_BLOCK = 64
NSA_SEL_SHIFT = 6
NSA_SEL_TOPN = 4
NSA_WINDOW = 512
NSA_TQ = 256
N_GROUPS = 4
EXPERTS_PER_GROUP = 8
N_EXPERTS = N_GROUPS * EXPERTS_PER_GROUP
D_EXPERT = 512
EXPERT_ROWS = 256

NORM_EPS = 1e-6
NEG = -1e30
BIG = 1e9

G_MQ, G_MK, G_MV, G_MO = 0, 1, 2, 3
G_BQ, G_BK, G_BV = 4, 5, 6
G_RQ, G_RK, G_RV, G_RG = 7, 8, 9, 10
G_NQ, G_NKC, G_NVC, G_NKS, G_NVS, G_NKW, G_NVW = 11, 12, 13, 14, 15, 16, 17
GL_MI, GL_MF, GL_NG = 0, 4, 8

VMEM_LIMIT = 56 * 1024 * 1024


def _cparams(sem):
    return pltpu.CompilerParams(dimension_semantics=sem, vmem_limit_bytes=VMEM_LIMIT)


def _dot(a, b):
    return jnp.dot(a.astype(MXU_DTYPE), b.astype(MXU_DTYPE), preferred_element_type=F32)


def _dot_nt(a, b):
    return lax.dot_general(a.astype(MXU_DTYPE), b.astype(MXU_DTYPE), (((1,), (1,)), ((), ())),
                           preferred_element_type=F32)


def _dot_tn(a, b):
    return lax.dot_general(a.astype(MXU_DTYPE), b.astype(MXU_DTYPE), (((0,), (0,)), ((), ())),
                           preferred_element_type=F32)


def _dot_split(a, b01):
    a_hi = a.astype(MXU_DTYPE)
    r1 = a - a_hi.astype(F32)
    a_mid = r1.astype(MXU_DTYPE)
    a_lo = (r1 - a_mid.astype(F32)).astype(MXU_DTYPE)
    b = b01.astype(MXU_DTYPE)
    return (jnp.dot(a_hi, b, preferred_element_type=F32) + jnp.dot(a_mid, b, preferred_element_type=F32)
            + jnp.dot(a_lo, b, preferred_element_type=F32))


def _rms(x, g):
    return x * lax.rsqrt(jnp.mean(x * x, axis=-1, keepdims=True) + NORM_EPS) * g


def _sigmoid(x):
    return jax.nn.sigmoid(x)


def _lane_col(x, idx):
    lane = lax.broadcasted_iota(jnp.int32, x.shape, 1)
    return jnp.sum(jnp.where(lane == idx, x, 0.0), axis=1, keepdims=True)


def _ada_kernel(c_ref, w_ref, b_ref, o_ref):
    c = c_ref[...]
    o_ref[0] = _dot(c * _sigmoid(c), w_ref[0]) + b_ref[0]


def _ada_mod(c, ada_w, ada_b):
    depth, d, n6 = ada_w.shape
    b = c.shape[0]
    tn = 1024
    return pl.pallas_call(
        _ada_kernel,
        grid=(depth, n6 // tn),
        in_specs=[pl.BlockSpec((b, d), lambda l, j: (0, 0)),
                  pl.BlockSpec((1, d, tn), lambda l, j: (l, 0, j)),
                  pl.BlockSpec((1, 1, tn), lambda l, j: (l, 0, j))],
        out_specs=pl.BlockSpec((1, b, tn), lambda l, j: (l, 0, j)),
        out_shape=jax.ShapeDtypeStruct((depth, b, n6), F32),
        compiler_params=_cparams(("arbitrary", "arbitrary")),
        name="ada_mod",
    )(c, ada_w, ada_b.reshape(depth, 1, n6))


def _norm_in_kernel(x_ref, g_ref, sc_ref, sh_ref, w_ref, ws_ref, z_ref, zg_ref, h_s):
    @pl.when(pl.program_id(1) == 0)
    def _():
        h = _rms(x_ref[...], g_ref[...]) * (1.0 + sc_ref[0]) + sh_ref[0]
        hb = h.astype(MXU_DTYPE)
        h_s[...] = hb
        zg_ref[...] = jnp.dot(hb, ws_ref[...], preferred_element_type=F32)

    z_ref[...] = jnp.dot(h_s[...], w_ref[...], preferred_element_type=F32)


def _norm_in_proj(x2d, norm_g, mod3, w_wide, w_gate, seq):
    n, d = x2d.shape
    tm, tn = 1024, 1024
    per_b = seq // tm
    return pl.pallas_call(
        _norm_in_kernel,
        grid=(n // tm, D_WIDE // tn),
        in_specs=[pl.BlockSpec((tm, d), lambda i, j: (i, 0)),
                  pl.BlockSpec((1, d), lambda i, j: (0, 0)),
                  pl.BlockSpec((1, 1, d), lambda i, j: (i // per_b, 0, 1)),
                  pl.BlockSpec((1, 1, d), lambda i, j: (i // per_b, 0, 0)),
                  pl.BlockSpec((d, tn), lambda i, j: (0, j)),
                  pl.BlockSpec((d, GATE_W), lambda i, j: (0, 0))],
        out_specs=[pl.BlockSpec((tm, tn), lambda i, j: (i, j)),
                   pl.BlockSpec((tm, GATE_W), lambda i, j: (i, 0))],
        out_shape=[jax.ShapeDtypeStruct((n, D_WIDE), F32),
                   jax.ShapeDtypeStruct((n, GATE_W), F32)],
        scratch_shapes=[pltpu.VMEM((tm, d), MXU_DTYPE)],
        compiler_params=_cparams(("arbitrary", "arbitrary")),
        name="norm_in_proj",
    )(x2d, norm_g.reshape(1, d), mod3, mod3, w_wide, w_gate)


def _mlstm_kernel(gb_ref, q_ref, k_ref, v_ref, o_ref, zg_ref, cwq_ref, cwk_ref, og_ref, y_ref,
                  pad_s, qs_s, ks_s, ic_s, fc_s):
    h = pl.program_id(1)
    t = q_ref.shape[1]
    cl = MLSTM_CHUNK
    nc = t // cl

    def conv_silu(src_ref, cw_ref, dst_s, scale):
        pad_s[0:8, :] = jnp.zeros((8, HEAD_DIM), F32)
        pad_s[8:8 + t, :] = src_ref[0]
        off = 8 - (MLSTM_CONV - 1)
        acc = cw_ref[0:1, :] * pad_s[off:off + t, :]
        for j in range(1, MLSTM_CONV):
            acc = acc + cw_ref[j:j + 1, :] * pad_s[off + j:off + j + t, :]
        dst_s[...] = acc * _sigmoid(acc) * scale

    conv_silu(q_ref, cwq_ref, qs_s, HEAD_DIM ** -0.5)
    conv_silu(k_ref, cwk_ref, ks_s, 1.0)

    zg = zg_ref[0]
    ic_s[...] = _lane_col(zg, GL_MI + h) + gb_ref[0, h]
    f_pre = _lane_col(zg, GL_MF + h) + gb_ref[1, h]
    fc_s[...] = jnp.minimum(f_pre, 0.0) - jnp.log1p(jnp.exp(-jnp.abs(f_pre)))

    rr = lax.broadcasted_iota(jnp.int32, (cl, cl), 0)
    cc = lax.broadcasted_iota(jnp.int32, (cl, cl), 1)
    eye = rr == cc
    causal = cc <= rr
    og = og_ref[...]

    def body(c, carry):
        c_st, n_st, m_st = carry
        sl = pl.ds(pl.multiple_of(c * cl, cl), cl)
        qc = qs_s[sl, :]
        kc = ks_s[sl, :]
        vc = v_ref[0, sl, :]
        i_col = ic_s[sl, :]
        f_col = fc_s[sl, :]
        f_row = jnp.sum(jnp.where(eye, f_col, 0.0), axis=0, keepdims=True)
        i_row = jnp.sum(jnp.where(eye, i_col, 0.0), axis=0, keepdims=True)
        a_col = jnp.sum(jnp.where(causal, f_row, 0.0), axis=1, keepdims=True)
        a_row = jnp.sum(jnp.where(rr <= cc, f_col, 0.0), axis=0, keepdims=True)
        log_d = jnp.where(causal, a_col - a_row + i_row, NEG)
        m_inter = a_col + m_st
        m_row = jnp.maximum(m_inter, jnp.max(log_d, axis=1, keepdims=True))
        s = _dot_nt(qc, kc) * jnp.exp(log_d - m_row)
        w_inter = jnp.exp(m_inter - m_row)
        num = _dot(s, vc) + w_inter * _dot(qc, c_st)
        den = jnp.sum(s, axis=1, keepdims=True) + w_inter * jnp.sum(qc * n_st, axis=1, keepdims=True)
        h_out = num / jnp.maximum(jnp.abs(den), jnp.exp(-m_row))
        a_last = jnp.sum(f_col, axis=0, keepdims=True)
        w_log = a_last - a_col + i_col
        m_new = jnp.maximum(a_last + m_st, jnp.max(w_log, axis=0, keepdims=True))
        w = jnp.exp(w_log - m_new)
        decay = jnp.exp(a_last + m_st - m_new)
        kw = kc * w
        c_new = decay * c_st + _dot_tn(kw, vc)
        n_new = decay * n_st + jnp.sum(kw, axis=0, keepdims=True)
        y = _rms(h_out, og) * _sigmoid(o_ref[0, sl, :])
        y_ref[0, sl, :] = y.astype(y_ref.dtype)
        return c_new, n_new, m_new

    carry0 = (jnp.zeros((HEAD_DIM, HEAD_DIM), F32), jnp.zeros((1, HEAD_DIM), F32), jnp.zeros((1, 1), F32))
    lax.fori_loop(0, nc, body, carry0)


def _col_spec(seq, group):
    return pl.BlockSpec((1, seq, HEAD_DIM), lambda b, h: (b, 0, group * HEADS + h))


def _mlstm(z3, zg3, gate_b, conv_w, out_g):
    bsz, seq, _ = z3.shape
    smem = pl.BlockSpec(memory_space=pltpu.SMEM)
    return pl.pallas_call(
        _mlstm_kernel,
        grid=(bsz, HEADS),
        in_specs=[smem,
                  _col_spec(seq, G_MQ), _col_spec(seq, G_MK), _col_spec(seq, G_MV), _col_spec(seq, G_MO),
                  pl.BlockSpec((1, seq, GATE_W), lambda b, h: (b, 0, 0)),
                  pl.BlockSpec((MLSTM_CONV, HEAD_DIM), lambda b, h: (0, h)),
                  pl.BlockSpec((MLSTM_CONV, HEAD_DIM), lambda b, h: (0, HEADS + h)),
                  pl.BlockSpec((1, HEAD_DIM), lambda b, h: (0, 0))],
        out_specs=pl.BlockSpec((1, seq, HEAD_DIM), lambda b, h: (b, 0, h)),
        out_shape=jax.ShapeDtypeStruct((bsz, seq, GROUP_W), BF16),
        scratch_shapes=[pltpu.VMEM((seq + 8, HEAD_DIM), F32), pltpu.VMEM((seq, HEAD_DIM), F32),
                        pltpu.VMEM((seq, HEAD_DIM), F32), pltpu.VMEM((seq, 1), F32), pltpu.VMEM((seq, 1), F32)],
        compiler_params=_cparams(("arbitrary", "arbitrary")),
        name="mlstm",
    )(gate_b, z3, z3, z3, z3, zg3, conv_w, conv_w, out_g.reshape(1, HEAD_DIM))


def _ret_kernel(lg_ref, q_ref, k_ref, v_ref, g_ref, og_ref, y_ref):
    h = pl.program_id(1)
    t = q_ref.shape[1]
    cl = RET_CHUNK
    lg = lg_ref[h]
    rr = lax.broadcasted_iota(jnp.int32, (cl, cl), 0)
    cc = lax.broadcasted_iota(jnp.int32, (cl, cl), 1)
    diff = (rr - cc).astype(F32)
    decay_in = jnp.where(diff >= 0, jnp.exp(lg * jnp.maximum(diff, 0.0)), 0.0)
    jcol = lax.broadcasted_iota(jnp.int32, (cl, 1), 0).astype(F32)
    zeta = jnp.exp(lg * (cl - 1.0 - jcol))
    xi = jnp.exp(lg * (jcol + 1.0))
    g_chunk = jnp.exp(jnp.full((1, 1), lg * cl, F32))
    og = og_ref[...]

    def body(c, r_st):
        sl = pl.ds(pl.multiple_of(c * cl, cl), cl)
        qc = q_ref[0, sl, :]
        kc = k_ref[0, sl, :] * HEAD_DIM ** -0.5
        vc = v_ref[0, sl, :]
        scores = _dot_nt(qc, kc) * decay_in
        o = _dot(scores, vc) + _dot(qc, r_st) * xi
        r_new = g_chunk * r_st + _dot_tn(kc * zeta, vc)
        gg = g_ref[0, sl, :]
        y = _rms(o, og) * (gg * _sigmoid(gg))
        y_ref[0, sl, :] = y.astype(y_ref.dtype)
        return r_new

    lax.fori_loop(0, t // cl, body, jnp.zeros((HEAD_DIM, HEAD_DIM), F32))


def _retention(z3, log_gamma, out_g):
    bsz, seq, _ = z3.shape
    return pl.pallas_call(
        _ret_kernel,
        grid=(bsz, HEADS),
        in_specs=[pl.BlockSpec(memory_space=pltpu.SMEM),
                  _col_spec(seq, G_RQ), _col_spec(seq, G_RK), _col_spec(seq, G_RV), _col_spec(seq, G_RG),
                  pl.BlockSpec((1, HEAD_DIM), lambda b, h: (0, 0))],
        out_specs=pl.BlockSpec((1, seq, HEAD_DIM), lambda b, h: (b, 0, h)),
        out_shape=jax.ShapeDtypeStruct((bsz, seq, GROUP_W), BF16),
        compiler_params=_cparams(("arbitrary", "arbitrary")),
        name="retention",
    )(log_gamma, z3, z3, z3, z3, out_g.reshape(1, HEAD_DIM))


def _online_step(carry, s, valid, v_tile):
    m, l, acc = carry
    sm = jnp.where(valid, s, NEG)
    m_new = jnp.maximum(m, jnp.max(sm, axis=1, keepdims=True))
    p = jnp.where(valid, jnp.exp(sm - m_new), 0.0)
    alpha = jnp.exp(m - m_new)
    return m_new, alpha * l + jnp.sum(p, axis=1, keepdims=True), alpha * acc + _dot(p, v_tile)


def _online_init(rows):
    return (jnp.full((rows, 1), NEG, F32), jnp.zeros((rows, 1), F32), jnp.zeros((rows, HEAD_DIM), F32))


def _rank_before(vals, n_cols):
    lane = lax.broadcasted_iota(jnp.int32, vals.shape, 1)
    rank = jnp.zeros(vals.shape, jnp.int32)
    for jp in range(n_cols):
        col = vals[:, jp:jp + 1]
        beats = (col > vals) | ((col == vals) & (lane > jp))
        rank = rank + beats.astype(jnp.int32)
    return rank


def _moba_kernel(slope_ref, q_ref, k_ref, v_ref, g_ref, y_ref, kn_s, vb_s, kmean_s):
    h = pl.program_id(1)
    qi = pl.program_id(2)
    t = k_ref.shape[1]
    blk = MOBA_BLOCK
    nb = t // blk
    scale = HEAD_DIM ** -0.5

    @pl.when(qi == 0)
    def _():
        kn = _rms(k_ref[0], g_ref[1:2, :])
        kn_s[...] = kn.astype(kn_s.dtype)
        vb_s[...] = v_ref[0].astype(vb_s.dtype)
        kmean_s[...] = jnp.zeros(kmean_s.shape, F32)
        for j in range(nb):
            kmean_s[j:j + 1, :] = jnp.mean(kn[j * blk:(j + 1) * blk, :], axis=0, keepdims=True)

    qn = _rms(q_ref[0], g_ref[0:1, :])
    lane = lax.broadcasted_iota(jnp.int32, (blk, HEAD_DIM), 1)
    gate = jnp.where(lane < qi, _dot_nt(qn, kmean_s[...]), NEG)
    sel = ((_rank_before(gate, nb) < MOBA_TOPK) & (lane < qi)).astype(F32)

    slope = slope_ref[h]
    rw = lax.broadcasted_iota(jnp.int32, (blk, blk), 0)
    cw = lax.broadcasted_iota(jnp.int32, (blk, blk), 1)
    rel = rw - cw

    def body(j, carry):
        ksl = pl.ds(pl.multiple_of(j * blk, blk), blk)
        dist = rel + (qi - j) * blk
        own = j == qi
        thr = jnp.where(own, 0, -t)
        picked = _lane_col(sel, j) + jnp.where(own, 1.0, 0.0)
        valid = (rel >= thr) & (picked > 0.0)
        s = _dot_nt(qn, kn_s[ksl, :]) * scale - slope * dist.astype(F32)
        return _online_step(carry, s, valid, vb_s[ksl, :])

    _, l, acc = lax.fori_loop(0, qi + 1, body, _online_init(blk))
    y_ref[0] = (acc / l).astype(y_ref.dtype)


def _moba(z3, qk_g, slopes):
    bsz, seq, _ = z3.shape
    blk = MOBA_BLOCK
    kv = lambda group: pl.BlockSpec((1, seq, HEAD_DIM), lambda b, h, i: (b, 0, group * HEADS + h))
    return pl.pallas_call(
        _moba_kernel,
        grid=(bsz, HEADS, seq // blk),
        in_specs=[pl.BlockSpec(memory_space=pltpu.SMEM),
                  pl.BlockSpec((1, blk, HEAD_DIM), lambda b, h, i: (b, i, G_BQ * HEADS + h)),
                  kv(G_BK), kv(G_BV),
                  pl.BlockSpec((2, HEAD_DIM), lambda b, h, i: (0, 0))],
        out_specs=pl.BlockSpec((1, blk, HEAD_DIM), lambda b, h, i: (b, i, h)),
        out_shape=jax.ShapeDtypeStruct((bsz, seq, GROUP_W), BF16),
        scratch_shapes=[pltpu.VMEM((seq, HEAD_DIM), MXU_DTYPE), pltpu.VMEM((seq, HEAD_DIM), MXU_DTYPE),
                        pltpu.VMEM((HEAD_DIM, HEAD_DIM), F32)],
        compiler_params=_cparams(("arbitrary", "arbitrary", "arbitrary")),
        name="moba",
    )(slopes, z3, z3, z3, qk_g)


def _gelu_tanh(x):
    return 0.5 * x * (1.0 + jnp.tanh(0.7978845608028654 * (x + 0.044715 * (x * x * x))))


def _nsa_kernel(slope_ref, q_ref, kc_ref, vc_ref, ks_ref, vs_ref, kw_ref, vw_ref, zg_ref,
                qg_ref, kg_ref, pe_ref, w1_ref, w2_ref, y_ref,
                kcmp_s, vcmp_s, ksn_s, vsb_s, kwn_s, vwb_s):
    h = pl.program_id(1)
    qi = pl.program_id(2)
    t = kc_ref.shape[1]
    tq = NSA_TQ
    nsub = t // NSA_CMP_STRIDE
    n_cmp = nsub - 1
    n_sel = t // NSA_SEL_BLOCK
    scale = HEAD_DIM ** -0.5

    @pl.when(qi == 0)
    def _():
        for cv, (src, dst) in enumerate(((kc_ref, kcmp_s), (vc_ref, vcmp_s))):
            acc_a = jnp.zeros((nsub, HEAD_DIM), F32)
            acc_b = jnp.zeros((nsub, HEAD_DIM), F32)
            for r in range(NSA_CMP_STRIDE):
                zr = src[0, pl.ds(r, nsub, stride=NSA_CMP_STRIDE), :]
                acc_a = acc_a + _dot(zr + pe_ref[cv, r:r + 1, :], w1_ref[cv, r])
                rb = NSA_CMP_STRIDE + r
                acc_b = acc_b + _dot(zr + pe_ref[cv, rb:rb + 1, :], w1_ref[cv, rb])
            hid = _gelu_tanh(acc_a + pltpu.roll(acc_b, nsub - 1, axis=0))
            cmp = _dot(hid, w2_ref[cv])
            if cv == 0:
                cmp = _rms(cmp, kg_ref[0:1, :])
            dst[...] = cmp.astype(dst.dtype)
        ksn_s[...] = _rms(ks_ref[0], kg_ref[1:2, :]).astype(ksn_s.dtype)
        vsb_s[...] = vs_ref[0].astype(vsb_s.dtype)
        kwn_s[...] = _rms(kw_ref[0], kg_ref[2:3, :]).astype(kwn_s.dtype)
        vwb_s[...] = vw_ref[0].astype(vwb_s.dtype)

    t0 = qi * tq
    slope = slope_ref[h]
    qn = _rms(q_ref[0], qg_ref[...])
    rowi = lax.broadcasted_iota(jnp.int32, (tq, HEAD_DIM), 0)
    lane = lax.broadcasted_iota(jnp.int32, (tq, HEAD_DIM), 1)
    tpos = t0 + rowi

    dist_c = tpos - (lane * NSA_CMP_STRIDE + (NSA_CMP_LEN - 1))
    ok_c = (dist_c >= 0) & (lane < n_cmp)
    s_c = jnp.where(ok_c, _dot_nt(qn, kcmp_s[...]) * scale - slope * dist_c.astype(F32), NEG)
    m_c = jnp.max(s_c, axis=1, keepdims=True)
    e_c = jnp.where(ok_c, jnp.exp(s_c - m_c), 0.0)
    p_c = e_c / jnp.maximum(jnp.sum(e_c, axis=1, keepdims=True), 1e-30)
    o_cmp = _dot(p_c, vcmp_s[...])

    oc = lax.broadcasted_iota(jnp.int32, (nsub, HEAD_DIM), 0)
    oj = lax.broadcasted_iota(jnp.int32, (nsub, HEAD_DIM), 1)
    overlap = ((oc * NSA_CMP_STRIDE <= oj * NSA_SEL_BLOCK + (NSA_SEL_BLOCK - 1))
               & (oc * NSA_CMP_STRIDE + (NSA_CMP_LEN - 1) >= oj * NSA_SEL_BLOCK)
               & (oc < n_cmp) & (oj < n_sel)).astype(F32)
    imp = _dot_split(p_c, overlap)
    cur = jnp.right_shift(tpos, NSA_SEL_SHIFT)
    imp = jnp.where(lane == cur, BIG, jnp.where(lane < cur, imp, NEG))
    sel = ((_rank_before(imp, n_sel) < NSA_SEL_TOPN) & (lane <= cur)).astype(BF16)

    rw = lax.broadcasted_iota(jnp.int32, (tq, tq), 0)
    cw = lax.broadcasted_iota(jnp.int32, (tq, tq), 1)
    rel = rw - cw
    en = lax.broadcasted_iota(jnp.int32, (HEAD_DIM, tq), 0)
    ec = jnp.right_shift(lax.broadcasted_iota(jnp.int32, (HEAD_DIM, tq), 1), NSA_SEL_SHIFT)
    per_tile = tq // NSA_SEL_BLOCK

    def sel_body(j, carry):
        ksl = pl.ds(pl.multiple_of(j * tq, tq), tq)
        dist = rel + (qi - j) * tq
        expand = (en == ec + j * per_tile).astype(BF16)
        picked = jnp.dot(sel, expand, preferred_element_type=F32)
        valid = (picked > 0.5) & (dist >= 0)
        s = _dot_nt(qn, ksn_s[ksl, :]) * scale - slope * dist.astype(F32)
        return _online_step(carry, s, valid, vsb_s[ksl, :])

    _, l_s, acc_s = lax.fori_loop(0, qi + 1, sel_body, _online_init(tq))
    o_slc = acc_s / l_s

    def win_body(j, carry):
        ksl = pl.ds(pl.multiple_of(j * tq, tq), tq)
        dist = rel + (qi - j) * tq
        valid = (dist >= 0) & (dist < NSA_WINDOW)
        s = _dot_nt(qn, kwn_s[ksl, :]) * scale - slope * dist.astype(F32)
        return _online_step(carry, s, valid, vwb_s[ksl, :])

    first = jnp.maximum(qi - NSA_WINDOW // tq, 0)
    _, l_w, acc_w = lax.fori_loop(first, qi + 1, win_body, _online_init(tq))
    o_win = acc_w / l_w

    zg = zg_ref[0]
    g_cmp = _sigmoid(_lane_col(zg, GL_NG + h))
    g_slc = _sigmoid(_lane_col(zg, GL_NG + HEADS + h))
    g_win = _sigmoid(_lane_col(zg, GL_NG + 2 * HEADS + h))
    y_ref[0] = (g_cmp * o_cmp + g_slc * o_slc + g_win * o_win).astype(y_ref.dtype)


def _nsa(z3, zg3, q_g, k_g, cmp_pos, cmp_w1, cmp_w2, slopes):
    bsz, seq, _ = z3.shape
    assert seq // NSA_CMP_STRIDE == HEAD_DIM, "compressed blocks are laid out on the 128 lanes"
    tq = NSA_TQ
    kv = lambda group: pl.BlockSpec((1, seq, HEAD_DIM), lambda b, h, i: (b, 0, group * HEADS + h))
    full = lambda shape: pl.BlockSpec(shape, lambda b, h, i: (0,) * len(shape))
    w1 = cmp_w1.reshape(2, NSA_CMP_LEN, HEAD_DIM, HEAD_DIM).astype(MXU_DTYPE)
    w2 = cmp_w2.astype(MXU_DTYPE)
    return pl.pallas_call(
        _nsa_kernel,
        grid=(bsz, HEADS, seq // tq),
        in_specs=[pl.BlockSpec(memory_space=pltpu.SMEM),
                  pl.BlockSpec((1, tq, HEAD_DIM), lambda b, h, i: (b, i, G_NQ * HEADS + h)),
                  kv(G_NKC), kv(G_NVC), kv(G_NKS), kv(G_NVS), kv(G_NKW), kv(G_NVW),
                  pl.BlockSpec((1, tq, GATE_W), lambda b, h, i: (b, i, 0)),
                  full((1, HEAD_DIM)), full((3, HEAD_DIM)), full((2, NSA_CMP_LEN, HEAD_DIM)),
                  full((2, NSA_CMP_LEN, HEAD_DIM, HEAD_DIM)), full((2, HEAD_DIM, HEAD_DIM))],
        out_specs=pl.BlockSpec((1, tq, HEAD_DIM), lambda b, h, i: (b, i, h)),
        out_shape=jax.ShapeDtypeStruct((bsz, seq, GROUP_W), BF16),
        scratch_shapes=[pltpu.VMEM((HEAD_DIM, HEAD_DIM), MXU_DTYPE), pltpu.VMEM((HEAD_DIM, HEAD_DIM), MXU_DTYPE),
                        pltpu.VMEM((seq, HEAD_DIM), MXU_DTYPE), pltpu.VMEM((seq, HEAD_DIM), MXU_DTYPE),
                        pltpu.VMEM((seq, HEAD_DIM), MXU_DTYPE), pltpu.VMEM((seq, HEAD_DIM), MXU_DTYPE)],
        compiler_params=_cparams(("arbitrary", "arbitrary", "arbitrary")),
        name="nsa",
    )(slopes, z3, z3, z3, z3, z3, z3, z3, zg3, q_g.reshape(1, HEAD_DIM), k_g, cmp_pos, w1, w2)


def _out_proj_kernel(ym_ref, yb_ref, yr_ref, yn_ref, w_ref, x_ref, g_ref, o_ref):
    acc = jnp.dot(ym_ref[...], w_ref[0:GROUP_W, :], preferred_element_type=F32)
    acc = acc + jnp.dot(yb_ref[...], w_ref[GROUP_W:2 * GROUP_W, :], preferred_element_type=F32)
    acc = acc + jnp.dot(yr_ref[...], w_ref[2 * GROUP_W:3 * GROUP_W, :], preferred_element_type=F32)
    acc = acc + jnp.dot(yn_ref[...], w_ref[3 * GROUP_W:4 * GROUP_W, :], preferred_element_type=F32)
    o_ref[...] = x_ref[...] + g_ref[0] * acc


def _out_proj(ys, w_out, x2d, mod3, seq):
    n, d = x2d.shape
    tm, tn = 1024, 1024
    per_b = seq // tm
    y_spec = pl.BlockSpec((tm, GROUP_W), lambda i, j: (i, 0))
    return pl.pallas_call(
        _out_proj_kernel,
        grid=(n // tm, d // tn),
        in_specs=[y_spec, y_spec, y_spec, y_spec,
                  pl.BlockSpec((4 * GROUP_W, tn), lambda i, j: (0, j)),
                  pl.BlockSpec((tm, tn), lambda i, j: (i, j)),
                  pl.BlockSpec((1, 1, tn), lambda i, j: (i // per_b, 0, 2 * (d // tn) + j))],
        out_specs=pl.BlockSpec((tm, tn), lambda i, j: (i, j)),
        out_shape=jax.ShapeDtypeStruct((n, d), F32),
        compiler_params=_cparams(("arbitrary", "arbitrary")),
        name="out_proj",
    )(*[y.reshape(n, GROUP_W) for y in ys], w_out, x2d, mod3)


SLAB_E0, SLAB_E1, SLAB_R0, SLAB_R1, SLAB_G0, SLAB_G1 = 0, 1, 2, 3, 4, 5


def _route_kernel(x_ref, g_ref, sc_ref, sh_ref, wr_ref, br_ref, h_ref, slab_ref, cnt_ref, carry_s):
    tm = x_ref.shape[0]

    @pl.when(pl.program_id(0) == 0)
    def _():
        carry_s[...] = jnp.zeros(carry_s.shape, F32)

    hmod = _rms(x_ref[...], g_ref[...]) * (1.0 + sc_ref[0]) + sh_ref[0]
    h_ref[...] = hmod
    logits = _dot(hmod, wr_ref[...]) + br_ref[...]
    lane = lax.broadcasted_iota(jnp.int32, logits.shape, 1).astype(F32)
    far = 4.0 * GATE_W

    in_g = lane < N_GROUPS
    lg = jnp.where(in_g, logits, NEG)
    g_max = jnp.max(lg, axis=1, keepdims=True)
    grp = jnp.min(jnp.where(in_g & (lg == g_max), lane, far), axis=1, keepdims=True)
    p_grp = 1.0 / jnp.sum(jnp.where(in_g, jnp.exp(lg - g_max), 0.0), axis=1, keepdims=True)

    lo = N_GROUPS + grp * EXPERTS_PER_GROUP
    in_e = (lane >= lo) & (lane < lo + EXPERTS_PER_GROUP)
    le = jnp.where(in_e, logits, NEG)
    e_max = jnp.max(le, axis=1, keepdims=True)
    ee = jnp.where(in_e, jnp.exp(le - e_max), 0.0)
    pe = jnp.where(in_e, ee / jnp.sum(ee, axis=1, keepdims=True), -1.0)
    p1 = jnp.max(pe, axis=1, keepdims=True)
    i1 = jnp.min(jnp.where(pe == p1, lane, far), axis=1, keepdims=True)
    pe2 = jnp.where(lane == i1, -1.0, pe)
    p2 = jnp.max(pe2, axis=1, keepdims=True)
    i2 = jnp.min(jnp.where(pe2 == p2, lane, far), axis=1, keepdims=True)
    e0 = i1 - N_GROUPS
    e1 = i2 - N_GROUPS
    g0 = p_grp * (p1 / (p1 + p2))
    g1 = p_grp * (p2 / (p1 + p2))

    onehot = ((lane == e0) | (lane == e1)).astype(BF16)
    rr = lax.broadcasted_iota(jnp.int32, (tm, tm), 0)
    cc = lax.broadcasted_iota(jnp.int32, (tm, tm), 1)
    before = jnp.dot((cc < rr).astype(BF16), onehot, preferred_element_type=F32) + carry_s[...]
    r0 = jnp.sum(jnp.where(lane == e0, before, 0.0), axis=1, keepdims=True)
    r1 = jnp.sum(jnp.where(lane == e1, before, 0.0), axis=1, keepdims=True)
    carry_s[...] = carry_s[...] + jnp.sum(onehot.astype(F32), axis=0, keepdims=True)
    cnt_ref[...] = carry_s[...]

    slab = jnp.where(lane == SLAB_E0, e0.astype(F32), 0.0)
    slab = jnp.where(lane == SLAB_E1, e1.astype(F32), slab)
    slab = jnp.where(lane == SLAB_R0, r0, slab)
    slab = jnp.where(lane == SLAB_R1, r1, slab)
    slab = jnp.where(lane == SLAB_G0, g0, slab)
    slab = jnp.where(lane == SLAB_G1, g1, slab)
    slab_ref[...] = slab


def _route(x2d, norm_g, mod3, w_router, b_router, seq):
    n, d = x2d.shape
    tm = 512
    per_b = seq // tm
    return pl.pallas_call(
        _route_kernel,
        grid=(n // tm,),
        in_specs=[pl.BlockSpec((tm, d), lambda i: (i, 0)),
                  pl.BlockSpec((1, d), lambda i: (0, 0)),
                  pl.BlockSpec((1, 1, d), lambda i: (i // per_b, 0, 4)),
                  pl.BlockSpec((1, 1, d), lambda i: (i // per_b, 0, 3)),
                  pl.BlockSpec((d, GATE_W), lambda i: (0, 0)),
                  pl.BlockSpec((1, GATE_W), lambda i: (0, 0))],
        out_specs=[pl.BlockSpec((tm, d), lambda i: (i, 0)),
                   pl.BlockSpec((tm, GATE_W), lambda i: (i, 0)),
                   pl.BlockSpec((1, GATE_W), lambda i: (0, 0))],
        out_shape=[jax.ShapeDtypeStruct((n, d), F32),
                   jax.ShapeDtypeStruct((n, GATE_W), F32),
                   jax.ShapeDtypeStruct((1, GATE_W), F32)],
        scratch_shapes=[pltpu.VMEM((1, GATE_W), F32)],
        compiler_params=_cparams(("arbitrary",)),
        name="route",
    )(x2d, norm_g.reshape(1, d), mod3, mod3, w_router, b_router)


def _dispatch_kernel(dest_ref, h_hbm, xs_in_hbm, xs_hbm, sem):
    del xs_in_hbm
    n_assign = dest_ref.shape[0]
    t0 = pl.program_id(0) * (n_assign // 2)

    def row_copy(src_row, dst_row):
        return pltpu.make_async_copy(h_hbm.at[pl.ds(src_row, 1)], xs_hbm.at[pl.ds(dst_row, 1)], sem)

    def start(a, _):
        row_copy(t0 + jnp.right_shift(a, 1), dest_ref[a]).start()
        return 0

    def wait(a, _):
        row_copy(0, 0).wait()
        return 0

    lax.fori_loop(0, n_assign, start, 0)
    lax.fori_loop(0, n_assign, wait, 0)


def _dispatch(h2, dest_flat, xs_init):
    n, d = h2.shape
    tm = 256
    return pl.pallas_call(
        _dispatch_kernel,
        grid=(n // tm,),
        in_specs=[pl.BlockSpec((2 * tm,), lambda i: (i,), memory_space=pltpu.SMEM),
                  pl.BlockSpec(memory_space=pl.ANY),
                  pl.BlockSpec(memory_space=pl.ANY)],
        out_specs=pl.BlockSpec(memory_space=pl.ANY),
        out_shape=jax.ShapeDtypeStruct(xs_init.shape, xs_init.dtype),
        scratch_shapes=[pltpu.SemaphoreType.DMA(())],
        input_output_aliases={2: 0},
        compiler_params=_cparams(("arbitrary",)),
        name="dispatch",
    )(dest_flat, h2, xs_init)


def _expert_kernel(blk_e_ref, n_used_ref, xs_ref, w1_ref, w3_ref, w2_ref, y_ref):
    del blk_e_ref
    used = pl.program_id(0) < n_used_ref[0]

    @pl.when(used)
    def _():
        x = xs_ref[...].astype(MXU_DTYPE)
        a = jnp.dot(x, w1_ref[0], preferred_element_type=F32)
        b = jnp.dot(x, w3_ref[0], preferred_element_type=F32)
        y_ref[...] = _dot(a * _sigmoid(a) * b, w2_ref[0])

    @pl.when(jnp.logical_not(used))
    def _():
        y_ref[...] = jnp.zeros(y_ref.shape, y_ref.dtype)


def _experts(xs, blk_e, n_used, w1, w3, w2):
    cap, d = xs.shape
    rows = EXPERT_ROWS
    row_map = lambda i, be, nu: (jnp.minimum(i, nu[0] - 1), 0)
    return pl.pallas_call(
        _expert_kernel,
        grid_spec=pltpu.PrefetchScalarGridSpec(
            num_scalar_prefetch=2,
            grid=(cap // rows,),
            in_specs=[pl.BlockSpec((rows, d), row_map),
                      pl.BlockSpec((1, d, D_EXPERT), lambda i, be, nu: (be[i], 0, 0)),
                      pl.BlockSpec((1, d, D_EXPERT), lambda i, be, nu: (be[i], 0, 0)),
                      pl.BlockSpec((1, D_EXPERT, d), lambda i, be, nu: (be[i], 0, 0))],
            out_specs=pl.BlockSpec((rows, d), lambda i, be, nu: (i, 0))),
        out_shape=jax.ShapeDtypeStruct((cap, d), F32),
        compiler_params=_cparams(("arbitrary",)),
        name="experts",
    )(blk_e, n_used, xs, w1, w3, w2)


def _combine_kernel(dest_ref, x_ref, g_ref, slab_ref, yb_hbm, o_ref, rows_s, sem):
    tm = x_ref.shape[0]

    def row_copy(src_row, k, r):
        return pltpu.make_async_copy(yb_hbm.at[pl.ds(src_row, 1)], rows_s.at[k, pl.ds(r, 1)], sem)

    def start(r, _):
        row_copy(dest_ref[2 * r], 0, r).start()
        row_copy(dest_ref[2 * r + 1], 1, r).start()
        return 0

    def wait(r, _):
        row_copy(0, 0, 0).wait()
        row_copy(0, 1, 0).wait()
        return 0

    lax.fori_loop(0, tm, start, 0)
    lax.fori_loop(0, tm, wait, 0)
    slab = slab_ref[...]
    moe = slab[:, SLAB_G0:SLAB_G0 + 1] * rows_s[0] + slab[:, SLAB_G1:SLAB_G1 + 1] * rows_s[1]
    o_ref[...] = x_ref[...] + g_ref[0] * moe


def _combine(x2d, mod3, slab, dest_flat, yb, seq):
    n, d = x2d.shape
    tm = 256
    per_b = seq // tm
    return pl.pallas_call(
        _combine_kernel,
        grid=(n // tm,),
        in_specs=[pl.BlockSpec((2 * tm,), lambda i: (i,), memory_space=pltpu.SMEM),
                  pl.BlockSpec((tm, d), lambda i: (i, 0)),
                  pl.BlockSpec((1, 1, d), lambda i: (i // per_b, 0, 5)),
                  pl.BlockSpec((tm, GATE_W), lambda i: (i, 0)),
                  pl.BlockSpec(memory_space=pl.ANY)],
        out_specs=pl.BlockSpec((tm, d), lambda i: (i, 0)),
        out_shape=jax.ShapeDtypeStruct((n, d), F32),
        scratch_shapes=[pltpu.VMEM((2, tm, d), F32), pltpu.SemaphoreType.DMA(())],
        compiler_params=_cparams(("arbitrary",)),
        name="combine",
    )(dest_flat, x2d, mod3, slab, yb)


_OFF_MI = 4 * GROUP_W
_OFF_BQ = _OFF_MI + 2 * HEADS
_OFF_NG = _OFF_BQ + 14 * GROUP_W


def _split_w_in(w_in):
    d = w_in.shape[0]
    w_wide = jnp.concatenate([w_in[:, :_OFF_MI], w_in[:, _OFF_BQ:_OFF_NG]], axis=1)
    n_gate = 2 * HEADS + 3 * HEADS
    w_gate = jnp.concatenate([w_in[:, _OFF_MI:_OFF_BQ], w_in[:, _OFF_NG:_OFF_NG + 3 * HEADS],
                              jnp.zeros((d, GATE_W - n_gate), w_in.dtype)], axis=1)
    return w_wide.astype(MXU_DTYPE), w_gate.astype(MXU_DTYPE)


def _moe(x2d, norm_g, mod3, wg, bg, we, be, w1, w3, w2, seq):
    n, d = x2d.shape
    n_route = N_GROUPS + N_EXPERTS
    w_router = jnp.concatenate([wg, we, jnp.zeros((d, GATE_W - n_route), wg.dtype)], axis=1).astype(MXU_DTYPE)
    b_router = jnp.concatenate([bg, be, jnp.zeros((GATE_W - n_route,), bg.dtype)]).reshape(1, GATE_W)
    h2, slab, cnt = _route(x2d, norm_g, mod3, w_router, b_router, seq)

    rows = EXPERT_ROWS
    counts = cnt[0, :N_EXPERTS].astype(jnp.int32)
    pcounts = (counts + rows - 1) // rows * rows
    pends = jnp.cumsum(pcounts)
    pstarts = pends - pcounts
    eid = slab[:, SLAB_E0:SLAB_E1 + 1].astype(jnp.int32)
    rank = slab[:, SLAB_R0:SLAB_R1 + 1].astype(jnp.int32)
    dest_flat = (pstarts[eid] + rank).reshape(-1)
    n_blocks = -(-2 * n // rows) + N_EXPERTS
    blk_row0 = jnp.arange(n_blocks, dtype=jnp.int32) * rows
    blk_e = jnp.minimum(jnp.sum((pends[None, :] <= blk_row0[:, None]).astype(jnp.int32), axis=1), N_EXPERTS - 1)
    n_used = (pends[-1:] // rows).astype(jnp.int32)

    xs = _dispatch(h2, dest_flat, jnp.zeros((n_blocks * rows, d), F32))
    yb = _experts(xs, blk_e, n_used, w1.astype(MXU_DTYPE), w3.astype(MXU_DTYPE), w2.astype(MXU_DTYPE))
    return _combine(x2d, mod3, slab, dest_flat, yb, seq)


def _layer(x2d, mod, bsz, seq, norm1_g, norm2_g, w_in, mlstm_gate_b, mlstm_conv_w, mlstm_out_g, moba_qk_g,
           ret_out_g, nsa_q_g, nsa_k_g, nsa_cmp_pos, nsa_cmp_w1, nsa_cmp_w2, w_out, router_g_w, router_g_b,
           router_e_w, router_e_b, exp_w1, exp_w3, exp_w2, slopes, log_gamma):
    n, d = x2d.shape
    mod3 = mod.reshape(bsz, 1, 6 * d)
    w_wide, w_gate = _split_w_in(w_in)
    z, zg = _norm_in_proj(x2d, norm1_g, mod3, w_wide, w_gate, seq)
    z3 = z.reshape(bsz, seq, D_WIDE)
    zg3 = zg.reshape(bsz, seq, GATE_W)
    y_m = _mlstm(z3, zg3, mlstm_gate_b, mlstm_conv_w, mlstm_out_g)
    y_b = _moba(z3, moba_qk_g, slopes[0::2])
    y_r = _retention(z3, log_gamma, ret_out_g)
    y_n = _nsa(z3, zg3, nsa_q_g, nsa_k_g, nsa_cmp_pos, nsa_cmp_w1, nsa_cmp_w2, slopes[1::2])
    x2d = _out_proj((y_m, y_b, y_r, y_n), w_out.astype(MXU_DTYPE), x2d, mod3, seq)
    return _moe(x2d, norm2_g, mod3, router_g_w, router_g_b, router_e_w, router_e_b, exp_w1, exp_w3, exp_w2, seq)


def kernel(x, c, norm1_g, norm2_g, ada_w, ada_b, w_in, mlstm_gate_b, mlstm_conv_w, mlstm_out_g, moba_qk_g,
           ret_out_g, nsa_q_g, nsa_k_g, nsa_cmp_pos, nsa_cmp_w1, nsa_cmp_w2, w_out, router_g_w, router_g_b,
           router_e_w, router_e_b, exp_w1, exp_w3, exp_w2):
    bsz, seq, d = x.shape
    depth = ada_w.shape[0]
    n_softmax_heads = 2 * HEADS
    slopes = jnp.exp2(-8.0 * jnp.arange(1, n_softmax_heads + 1, dtype=F32) / n_softmax_heads)
    log_gamma = jnp.log(1.0 - jnp.exp2(-5.0 - jnp.arange(HEADS, dtype=F32)))
    mod = _ada_mod(c, ada_w, ada_b)
    x2d = x.reshape(bsz * seq, d)
    for l in range(depth):
        x2d = _layer(x2d, mod[l], bsz, seq, norm1_g[l], norm2_g[l], w_in[l], mlstm_gate_b[l], mlstm_conv_w[l],
                     mlstm_out_g[l], moba_qk_g[l], ret_out_g[l], nsa_q_g[l], nsa_k_g[l], nsa_cmp_pos[l],
                     nsa_cmp_w1[l], nsa_cmp_w2[l], w_out[l], router_g_w[l], router_g_b[l], router_e_w[l],
                     router_e_b[l], exp_w1[l], exp_w3[l], exp_w2[l], slopes, log_gamma)
    return x2d.reshape(bsz, seq, d)
```

```python
import functools

import jax
import jax.numpy as jnp
from jax import lax
from jax.experimental import pallas as pl
from jax.experimental.pallas import tpu as pltpu

F32 = jnp.float32
BF16 = jnp.bfloat16
MXU_DTYPE = jnp.bfloat16

D_MODEL = 2048
HEAD_DIM = 128
HEADS = 4
GROUP_W = HEADS * HEAD_DIM
N_WIDE_GROUPS = 18
D_WIDE = N_WIDE_GROUPS * GROUP_W
GATE_W = 128

MLSTM_CHUNK = 64
MLSTM_CONV = 4
MOBA_BLOCK = 256
MOBA_SHIFT = 8
MOBA_TOPK = 3
RET_CHUNK = 128
NSA_CMP_LEN = 32
NSA_CMP_STRIDE = 16
NSA_SEL_BLOCK = 64
NSA_SEL_SHIFT = 6
NSA_SEL_TOPN = 4
NSA_WINDOW = 512
N_GROUPS = 4
EXPERTS_PER_GROUP = 8
N_EXPERTS = N_GROUPS * EXPERTS_PER_GROUP
D_EXPERT = 512
EXPERT_ROWS = 256

NORM_EPS = 1e-6
NEG = -1e30
BIG = 1e9

G_MQ, G_MK, G_MV, G_MO = 0, 1, 2, 3
G_BQ, G_BK, G_BV = 4, 5, 6
G_RQ, G_RK, G_RV, G_RG = 7, 8, 9, 10
G_NQ, G_NKC, G_NVC, G_NKS, G_NVS, G_NKW, G_NVW = 11, 12, 13, 14, 15, 16, 17
GL_MI, GL_MF, GL_NG = 0, 4, 8

VMEM_LIMIT = 56 * 1024 * 1024


def _cparams(sem):
    return pltpu.CompilerParams(dimension_semantics=sem, vmem_limit_bytes=VMEM_LIMIT)


def _dot(a, b):
    return jnp.dot(a.astype(MXU_DTYPE), b.astype(MXU_DTYPE), preferred_element_type=F32)


def _dot_nt(a, b):
    return lax.dot_general(a.astype(MXU_DTYPE), b.astype(MXU_DTYPE), (((1,), (1,)), ((), ())),
                           preferred_element_type=F32)


def _dot_tn(a, b):
    return lax.dot_general(a.astype(MXU_DTYPE), b.astype(MXU_DTYPE), (((0,), (0,)), ((), ())),
                           preferred_element_type=F32)


def _dot_split_nt(b01, a):
    a_hi = a.astype(MXU_DTYPE)
    r1 = a - a_hi.astype(F32)
    a_mid = r1.astype(MXU_DTYPE)
    a_lo = (r1 - a_mid.astype(F32)).astype(MXU_DTYPE)
    return _dot_nt(b01, a_hi) + _dot_nt(b01, a_mid) + _dot_nt(b01, a_lo)


def _rms(x, g):
    return x * lax.rsqrt(jnp.mean(x * x, axis=-1, keepdims=True) + NORM_EPS) * g


def _sigmoid(x):
    return jax.nn.sigmoid(x)


def _lane_col(x, idx):
    lane = lax.broadcasted_iota(jnp.int32, x.shape, 1)
    return jnp.sum(jnp.where(lane == idx, x, 0.0), axis=1, keepdims=True)


def _ada_kernel(c_ref, w_ref, b_ref, o_ref):
    c = c_ref[...]
    o_ref[0] = _dot(c * _sigmoid(c), w_ref[0]) + b_ref[0]


def _ada_mod(c, ada_w, ada_b):
    depth, d, n6 = ada_w.shape
    b = c.shape[0]
    tn = 1024
    return pl.pallas_call(
        _ada_kernel,
        grid=(depth, n6 // tn),
        in_specs=[pl.BlockSpec((b, d), lambda l, j: (0, 0)),
                  pl.BlockSpec((1, d, tn), lambda l, j: (l, 0, j)),
                  pl.BlockSpec((1, 1, tn), lambda l, j: (l, 0, j))],
        out_specs=pl.BlockSpec((1, b, tn), lambda l, j: (l, 0, j)),
        out_shape=jax.ShapeDtypeStruct((depth, b, n6), F32),
        compiler_params=_cparams(("arbitrary", "arbitrary")),
        name="ada_mod",
    )(c, ada_w, ada_b.reshape(depth, 1, n6))


def _norm_in_kernel(x_ref, g_ref, sc_ref, sh_ref, w_ref, ws_ref, z_ref, zg_ref, h_s):
    @pl.when(pl.program_id(1) == 0)
    def _():
        h = _rms(x_ref[...], g_ref[...]) * (1.0 + sc_ref[0]) + sh_ref[0]
        hb = h.astype(MXU_DTYPE)
        h_s[...] = hb
        zg_ref[...] = jnp.dot(hb, ws_ref[...], preferred_element_type=F32)

    z_ref[...] = jnp.dot(h_s[...], w_ref[0], preferred_element_type=F32)


def _norm_in_proj(x2d, norm_g, mod3, w_wide, layer, w_gate, seq):
    n, d = x2d.shape
    tm, tn = 1024, 1024
    per_b = seq // tm
    return pl.pallas_call(
        _norm_in_kernel,
        grid=(n // tm, D_WIDE // tn),
        in_specs=[pl.BlockSpec((tm, d), lambda i, j: (i, 0)),
                  pl.BlockSpec((1, d), lambda i, j: (0, 0)),
                  pl.BlockSpec((1, 1, d), lambda i, j: (i // per_b, 0, 1)),
                  pl.BlockSpec((1, 1, d), lambda i, j: (i // per_b, 0, 0)),
                  pl.BlockSpec((1, d, tn), lambda i, j: (layer, 0, j)),
                  pl.BlockSpec((d, GATE_W), lambda i, j: (0, 0))],
        out_specs=[pl.BlockSpec((tm, tn), lambda i, j: (i, j)),
                   pl.BlockSpec((tm, GATE_W), lambda i, j: (i, 0))],
        out_shape=[jax.ShapeDtypeStruct((n, D_WIDE), F32),
                   jax.ShapeDtypeStruct((n, GATE_W), F32)],
        scratch_shapes=[pltpu.VMEM((tm, d), MXU_DTYPE)],
        compiler_params=_cparams(("arbitrary", "arbitrary")),
        name="norm_in_proj",
    )(x2d, norm_g.reshape(1, d), mod3, mod3, w_wide, w_gate)


REC_HEADS_PER_STEP = 2


def _mlstm_kernel(gb_ref, q_ref, k_ref, v_ref, o_ref, zg_ref, cwq_ref, cwk_ref, og_ref, y_ref,
                  pad_s, qs_s, ks_s, ic_s, fc_s):
    hp = pl.program_id(1)
    t = q_ref.shape[1]
    cl = MLSTM_CHUNK
    nc = t // cl
    hps = REC_HEADS_PER_STEP

    def conv_silu(src_ref, cw_ref, cols, dst_s, scale):
        pad_s[8:8 + t, :] = src_ref[0, :, cols]
        off = 8 - (MLSTM_CONV - 1)
        acc = cw_ref[0:1, cols] * pad_s[off:off + t, :]
        for j in range(1, MLSTM_CONV):
            acc = acc + cw_ref[j:j + 1, cols] * pad_s[off + j:off + j + t, :]
        dst_s[...] = acc * _sigmoid(acc) * scale

    pad_s[0:8, :] = jnp.zeros((8, HEAD_DIM), F32)
    zg = zg_ref[0]
    for hh in range(hps):
        h = hp * hps + hh
        conv_silu(q_ref, cwq_ref, _head_cols(hh), qs_s.at[hh], HEAD_DIM ** -0.5)
        conv_silu(k_ref, cwk_ref, _head_cols(hh), ks_s.at[hh], 1.0)
        ic_s[hh] = _lane_col(zg, GL_MI + h) + gb_ref[0, h]
        f_pre = _lane_col(zg, GL_MF + h) + gb_ref[1, h]
        fc_s[hh] = jnp.minimum(f_pre, 0.0) - jnp.log1p(jnp.exp(-jnp.abs(f_pre)))

    rr = lax.broadcasted_iota(jnp.int32, (cl, cl), 0)
    cc = lax.broadcasted_iota(jnp.int32, (cl, cl), 1)
    eye = rr == cc
    causal = cc <= rr
    og = og_ref[...]

    def head_step(hh, c, carry):
        c_st, n_st, m_st = carry
        sl = pl.ds(pl.multiple_of(c * cl, cl), cl)
        cols = _head_cols(hh)
        qc = qs_s[hh, sl, :]
        kc = ks_s[hh, sl, :]
        vc = v_ref[0, sl, cols]
        i_col = ic_s[hh, sl, :]
        f_col = fc_s[hh, sl, :]
        f_row = jnp.sum(jnp.where(eye, f_col, 0.0), axis=0, keepdims=True)
        i_row = jnp.sum(jnp.where(eye, i_col, 0.0), axis=0, keepdims=True)
        a_col = jnp.sum(jnp.where(causal, f_row, 0.0), axis=1, keepdims=True)
        a_row = jnp.sum(jnp.where(rr <= cc, f_col, 0.0), axis=0, keepdims=True)
        log_d = jnp.where(causal, a_col - a_row + i_row, NEG)
        m_inter = a_col + m_st
        m_row = jnp.maximum(m_inter, jnp.max(log_d, axis=1, keepdims=True))
        s = _dot_nt(qc, kc) * jnp.exp(log_d - m_row)
        w_inter = jnp.exp(m_inter - m_row)
        num = _dot(s, vc) + w_inter * _dot(qc, c_st)
        den = jnp.sum(s, axis=1, keepdims=True) + w_inter * jnp.sum(qc * n_st, axis=1, keepdims=True)
        h_out = num / jnp.maximum(jnp.abs(den), jnp.exp(-m_row))
        a_last = jnp.sum(f_col, axis=0, keepdims=True)
        w_log = a_last - a_col + i_col
        m_new = jnp.maximum(a_last + m_st, jnp.max(w_log, axis=0, keepdims=True))
        w = jnp.exp(w_log - m_new)
        decay = jnp.exp(a_last + m_st - m_new)
        kw = kc * w
        c_new = decay * c_st + _dot_tn(kw, vc)
        n_new = decay * n_st + jnp.sum(kw, axis=0, keepdims=True)
        y = _rms(h_out, og) * _sigmoid(o_ref[0, sl, cols])
        y_ref[0, sl, cols] = y.astype(y_ref.dtype)
        return c_new, n_new, m_new

    def body(c, carry):
        return tuple(head_step(hh, c, carry[hh]) for hh in range(hps))

    carry0 = (jnp.zeros((HEAD_DIM, HEAD_DIM), F32), jnp.zeros((1, HEAD_DIM), F32), jnp.zeros((1, 1), F32))
    lax.fori_loop(0, nc, body, (carry0,) * hps)


def _col_spec(seq, group, hps):
    return pl.BlockSpec((1, seq, hps * HEAD_DIM), lambda b, h: (b, 0, group * (HEADS // hps) + h))


def _mlstm(z3, zg3, gate_b, conv_w, out_g):
    bsz, seq, _ = z3.shape
    hps = REC_HEADS_PER_STEP
    wide = hps * HEAD_DIM
    per_group = HEADS // hps
    smem = pl.BlockSpec(memory_space=pltpu.SMEM)
    return pl.pallas_call(
        _mlstm_kernel,
        grid=(bsz, per_group),
        in_specs=[smem,
                  _col_spec(seq, G_MQ, hps), _col_spec(seq, G_MK, hps), _col_spec(seq, G_MV, hps),
                  _col_spec(seq, G_MO, hps),
                  pl.BlockSpec((1, seq, GATE_W), lambda b, h: (b, 0, 0)),
                  pl.BlockSpec((MLSTM_CONV, wide), lambda b, h: (0, h)),
                  pl.BlockSpec((MLSTM_CONV, wide), lambda b, h: (0, per_group + h)),
                  pl.BlockSpec((1, HEAD_DIM), lambda b, h: (0, 0))],
        out_specs=pl.BlockSpec((1, seq, wide), lambda b, h: (b, 0, h)),
        out_shape=jax.ShapeDtypeStruct((bsz, seq, GROUP_W), BF16),
        scratch_shapes=[pltpu.VMEM((seq + 8, HEAD_DIM), F32), pltpu.VMEM((hps, seq, HEAD_DIM), F32),
                        pltpu.VMEM((hps, seq, HEAD_DIM), F32), pltpu.VMEM((hps, seq, 1), F32),
                        pltpu.VMEM((hps, seq, 1), F32)],
        compiler_params=_cparams(("arbitrary", "arbitrary")),
        name="mlstm",
    )(gate_b, z3, z3, z3, z3, zg3, conv_w, conv_w, out_g.reshape(1, HEAD_DIM))


def _ret_kernel(lg_ref, q_ref, k_ref, v_ref, g_ref, og_ref, y_ref):
    t = q_ref.shape[1]
    cl = RET_CHUNK
    rr = lax.broadcasted_iota(jnp.int32, (cl, cl), 0)
    cc = lax.broadcasted_iota(jnp.int32, (cl, cl), 1)
    diff = (rr - cc).astype(F32)
    jcol = lax.broadcasted_iota(jnp.int32, (cl, 1), 0).astype(F32)
    og = og_ref[...]
    consts = []
    for h in range(HEADS):
        lg = lg_ref[h]
        consts.append(dict(decay_in=jnp.where(diff >= 0, jnp.exp(lg * jnp.maximum(diff, 0.0)), 0.0),
                           zeta=jnp.exp(lg * (cl - 1.0 - jcol)), xi=jnp.exp(lg * (jcol + 1.0)),
                           g_chunk=jnp.exp(jnp.full((1, 1), lg * cl, F32))))

    def head_step(h, c, r_st):
        sl = pl.ds(pl.multiple_of(c * cl, cl), cl)
        cols = _head_cols(h)
        qc = q_ref[0, sl, cols]
        kc = k_ref[0, sl, cols] * HEAD_DIM ** -0.5
        vc = v_ref[0, sl, cols]
        scores = _dot_nt(qc, kc) * consts[h]["decay_in"]
        o = _dot(scores, vc) + _dot(qc, r_st) * consts[h]["xi"]
        r_new = consts[h]["g_chunk"] * r_st + _dot_tn(kc * consts[h]["zeta"], vc)
        gg = g_ref[0, sl, cols]
        y = _rms(o, og) * (gg * _sigmoid(gg))
        y_ref[0, sl, cols] = y.astype(y_ref.dtype)
        return r_new

    def body(c, carry):
        return tuple(head_step(h, c, carry[h]) for h in range(HEADS))

    lax.fori_loop(0, t // cl, body, (jnp.zeros((HEAD_DIM, HEAD_DIM), F32),) * HEADS)


def _retention(z3, log_gamma, out_g):
    bsz, seq, _ = z3.shape
    grp = lambda group: pl.BlockSpec((1, seq, GROUP_W), lambda b: (b, 0, group))
    return pl.pallas_call(
        _ret_kernel,
        grid=(bsz,),
        in_specs=[pl.BlockSpec(memory_space=pltpu.SMEM), grp(G_RQ), grp(G_RK), grp(G_RV), grp(G_RG),
                  pl.BlockSpec((1, HEAD_DIM), lambda b: (0, 0))],
        out_specs=pl.BlockSpec((1, seq, GROUP_W), lambda b: (b, 0, 0)),
        out_shape=jax.ShapeDtypeStruct((bsz, seq, GROUP_W), BF16),
        compiler_params=_cparams(("arbitrary",)),
        name="retention",
    )(log_gamma, z3, z3, z3, z3, out_g.reshape(1, HEAD_DIM))


ATT_TQ = 256
STRIP_W = 512
STRIP_SHIFT = 9
HEADS_PER_STEP = 2


def _rank_rows(vals):
    n = vals.shape[0]
    rowb = lax.broadcasted_iota(jnp.int32, vals.shape, 0)
    rank = jnp.zeros(vals.shape, jnp.int32)
    for jp in range(n):
        rv = vals[jp:jp + 1, :]
        beats = (rv > vals) | ((rv == vals) & (rowb > jp))
        rank = rank + beats.astype(jnp.int32)
    return rank


def _pad_rows(x, rows):
    return jnp.concatenate([x, jnp.zeros((rows - x.shape[0], x.shape[1]), x.dtype)], axis=0)


def _fold_lanes(op, acc, x):
    for b in range(x.shape[1] // HEAD_DIM):
        acc = op(acc, x[:, b * HEAD_DIM:(b + 1) * HEAD_DIM])
    return acc


def _block_masked_attention(heads, qi, strip_s):
    tq, w = ATT_TQ, STRIP_W
    scale = HEAD_DIM ** -0.5
    t0 = qi * tq
    last = jnp.right_shift(t0, STRIP_SHIFT)
    rr = lax.broadcasted_iota(jnp.int32, (tq, w), 0)
    cc = lax.broadcasted_iota(jnp.int32, (tq, w), 1)
    rel = rr - cc
    rel_f = rel.astype(F32)
    alibi = [(-hd["slope"]) * rel_f for hd in heads]
    erow = lax.broadcasted_iota(jnp.int32, (HEAD_DIM, w), 0)
    ecol = lax.broadcasted_iota(jnp.int32, (HEAD_DIM, w), 1)

    def scores(hd, c, bias):
        expand = jnp.where(erow == jnp.right_shift(c * w + ecol, hd["blk_shift"]), NEG, 0.0).astype(BF16)
        mask_bias = lax.dot_general(hd["unsel"], expand, (((0,), (0,)), ((), ())), preferred_element_type=F32)
        kc = hd["k_s"][pl.ds(pl.multiple_of(c * w, w), w), :]
        return _dot_nt(hd["qb"], kc) * scale + bias + mask_bias

    def first_pass(c, ms):
        out = []
        for hi, hd in enumerate(heads):
            s = scores(hd, c, alibi[hi] + (-hd["slope"]) * (t0 - c * w).astype(F32))
            strip_s[hi, c] = s
            out.append(_fold_lanes(jnp.maximum, ms[hi], s))
        return tuple(out)

    ms = lax.fori_loop(0, last, first_pass, tuple(jnp.full((tq, HEAD_DIM), NEG, F32) for _ in heads))
    dist = rel + (t0 - last * w)
    row_max = []
    for hi, hd in enumerate(heads):
        bias = jnp.where(dist >= 0, (-hd["slope"]) * dist.astype(F32), NEG)
        s = scores(hd, last, bias)
        strip_s[hi, last] = s
        row_max.append(jnp.max(_fold_lanes(jnp.maximum, ms[hi], s), axis=1, keepdims=True))

    def second_pass(c, carry):
        out = []
        for hi, hd in enumerate(heads):
            l_run, acc = carry[hi]
            p = jnp.exp(strip_s[hi, c] - row_max[hi])
            vc = hd["v_s"][pl.ds(pl.multiple_of(c * w, w), w), :]
            out.append((_fold_lanes(jnp.add, l_run, p), acc + _dot(p, vc)))
        return tuple(out)

    zero = jnp.zeros((tq, HEAD_DIM), F32)
    res = lax.fori_loop(0, last + 1, second_pass, tuple((zero, zero) for _ in heads))
    return [acc / jnp.sum(l_run, axis=1, keepdims=True) for l_run, acc in res]


def _head_cols(hh):
    return slice(hh * HEAD_DIM, (hh + 1) * HEAD_DIM)


def _moba_kernel(slope_ref, q_ref, k_ref, v_ref, g_ref, y_ref, kn_s, vb_s, kmean_s, strip_s):
    hp = pl.program_id(1)
    qi = pl.program_id(2)
    t = k_ref.shape[1]
    blk = MOBA_BLOCK
    nb = t // blk

    @pl.when(qi == 0)
    def _():
        for hh in range(HEADS_PER_STEP):
            kn = _rms(k_ref[0, :, _head_cols(hh)], g_ref[1:2, :])
            kn_s[hh] = kn.astype(kn_s.dtype)
            vb_s[hh] = v_ref[0, :, _head_cols(hh)].astype(vb_s.dtype)
            kmean_s[hh] = jnp.zeros(kmean_s.shape[1:], F32)
            for j in range(nb):
                kmean_s[hh, j:j + 1, :] = jnp.mean(kn[j * blk:(j + 1) * blk, :], axis=0, keepdims=True)

    rowb = lax.broadcasted_iota(jnp.int32, (nb, ATT_TQ), 0)
    heads = []
    for hh in range(HEADS_PER_STEP):
        qn = _rms(q_ref[0, :, _head_cols(hh)], g_ref[0:1, :])
        gate = jnp.where(rowb < qi, _dot_nt(kmean_s[hh], qn)[0:nb, :], NEG)
        sel = (_rank_rows(gate) < MOBA_TOPK) & (rowb < qi)
        unsel = jnp.where(sel | (rowb == qi), 0.0, 1.0)
        heads.append(dict(qb=qn.astype(MXU_DTYPE), k_s=kn_s.at[hh], v_s=vb_s.at[hh],
                          slope=slope_ref[hp * HEADS_PER_STEP + hh], blk_shift=MOBA_SHIFT,
                          unsel=_pad_rows(unsel, HEAD_DIM).astype(BF16)))
    outs = _block_masked_attention(heads, qi, strip_s)
    for hh in range(HEADS_PER_STEP):
        y_ref[0, :, _head_cols(hh)] = outs[hh].astype(y_ref.dtype)


def _moba(z3, qk_g, slopes):
    bsz, seq, _ = z3.shape
    assert MOBA_BLOCK == ATT_TQ and seq % STRIP_W == 0
    hps = HEADS_PER_STEP
    wide = hps * HEAD_DIM
    per_group = HEADS // hps
    kv = lambda group: pl.BlockSpec((1, seq, wide), lambda b, h, i: (b, 0, group * per_group + h))
    return pl.pallas_call(
        _moba_kernel,
        grid=(bsz, per_group, seq // ATT_TQ),
        in_specs=[pl.BlockSpec(memory_space=pltpu.SMEM),
                  pl.BlockSpec((1, ATT_TQ, wide), lambda b, h, i: (b, i, G_BQ * per_group + h)),
                  kv(G_BK), kv(G_BV),
                  pl.BlockSpec((2, HEAD_DIM), lambda b, h, i: (0, 0))],
        out_specs=pl.BlockSpec((1, ATT_TQ, wide), lambda b, h, i: (b, i, h)),
        out_shape=jax.ShapeDtypeStruct((bsz, seq, GROUP_W), BF16),
        scratch_shapes=[pltpu.VMEM((hps, seq, HEAD_DIM), MXU_DTYPE), pltpu.VMEM((hps, seq, HEAD_DIM), MXU_DTYPE),
                        pltpu.VMEM((hps, HEAD_DIM, HEAD_DIM), F32),
                        pltpu.VMEM((hps, seq // STRIP_W, ATT_TQ, STRIP_W), F32)],
        compiler_params=_cparams(("arbitrary", "arbitrary", "arbitrary")),
        name="moba",
    )(slopes, z3, z3, z3, qk_g)


def _gelu_tanh(x):
    return 0.5 * x * (1.0 + jnp.tanh(0.7978845608028654 * (x + 0.044715 * (x * x * x))))


def _nsa_kernel(slope_ref, q_ref, kc0_ref, kc1_ref, vc0_ref, vc1_ref, ks_ref, vs_ref, kw_ref, vw_ref, zg_ref,
                qg_ref, kg_ref, pe_ref, w1_ref, w2_ref, y_ref,
                kcmp_s, vcmp_s, ksn_s, vsb_s, kwn_s, vwb_s, strip_s):
    kc_refs, vc_refs = (kc0_ref, kc1_ref), (vc0_ref, vc1_ref)
    hp = pl.program_id(1)
    qi = pl.program_id(2)
    t = ks_ref.shape[1]
    tq = ATT_TQ
    nsub = t // NSA_CMP_STRIDE
    n_cmp = nsub - 1
    n_sel = t // NSA_SEL_BLOCK
    scale = HEAD_DIM ** -0.5

    @pl.when(qi == 0)
    def _():
        for hh in range(HEADS_PER_STEP):
            cols = _head_cols(hh)
            for cv, (src, dst) in enumerate(((kc_refs[hh], kcmp_s), (vc_refs[hh], vcmp_s))):
                acc_a = jnp.zeros((nsub, HEAD_DIM), F32)
                acc_b = jnp.zeros((nsub, HEAD_DIM), F32)
                for r in range(NSA_CMP_STRIDE):
                    zr = src[0, pl.ds(r, nsub, stride=NSA_CMP_STRIDE), :]
                    acc_a = acc_a + _dot(zr + pe_ref[cv, r:r + 1, :], w1_ref[cv, r])
                    rb = NSA_CMP_STRIDE + r
                    acc_b = acc_b + _dot(zr + pe_ref[cv, rb:rb + 1, :], w1_ref[cv, rb])
                hid = _gelu_tanh(acc_a + pltpu.roll(acc_b, nsub - 1, axis=0))
                cmp = _dot(hid, w2_ref[cv])
                if cv == 0:
                    cmp = _rms(cmp, kg_ref[0:1, :])
                dst[hh] = cmp.astype(dst.dtype)
            ksn_s[hh] = _rms(ks_ref[0, :, cols], kg_ref[1:2, :]).astype(ksn_s.dtype)
            vsb_s[hh] = vs_ref[0, :, cols].astype(vsb_s.dtype)
            kwn_s[hh] = _rms(kw_ref[0, :, cols], kg_ref[2:3, :]).astype(kwn_s.dtype)
            vwb_s[hh] = vw_ref[0, :, cols].astype(vwb_s.dtype)

    t0 = qi * tq
    rowi = lax.broadcasted_iota(jnp.int32, (tq, HEAD_DIM), 0)
    lane = lax.broadcasted_iota(jnp.int32, (tq, HEAD_DIM), 1)
    dist_c = (t0 + rowi) - (lane * NSA_CMP_STRIDE + (NSA_CMP_LEN - 1))
    ok_c = (dist_c >= 0) & (lane < n_cmp)
    dist_cf = dist_c.astype(F32)

    ob = lax.broadcasted_iota(jnp.int32, (HEAD_DIM, nsub), 0)
    oc = lax.broadcasted_iota(jnp.int32, (HEAD_DIM, nsub), 1)
    overlap_t = ((oc * NSA_CMP_STRIDE <= ob * NSA_SEL_BLOCK + (NSA_SEL_BLOCK - 1))
                 & (oc * NSA_CMP_STRIDE + (NSA_CMP_LEN - 1) >= ob * NSA_SEL_BLOCK)
                 & (oc < n_cmp) & (ob < n_sel)).astype(F32)
    rowb = lax.broadcasted_iota(jnp.int32, (n_sel, tq), 0)
    cur = jnp.right_shift(t0 + lax.broadcasted_iota(jnp.int32, (n_sel, tq), 1), NSA_SEL_SHIFT)

    win_w = NSA_WINDOW + tq
    k0 = jnp.maximum(t0 - NSA_WINDOW, 0)
    wr = lax.broadcasted_iota(jnp.int32, (tq, win_w), 0)
    wc = lax.broadcasted_iota(jnp.int32, (tq, win_w), 1)
    dist_w = (t0 - k0) + wr - wc
    ok_w = (dist_w >= 0) & (dist_w < NSA_WINDOW)
    dist_wf = dist_w.astype(F32)
    win_rows = pl.ds(pl.multiple_of(k0, tq), win_w)

    heads, o_cmp, o_win = [], [], []
    for hh in range(HEADS_PER_STEP):
        slope = slope_ref[hp * HEADS_PER_STEP + hh]
        qn = _rms(q_ref[0, :, _head_cols(hh)], qg_ref[...])
        qb = qn.astype(MXU_DTYPE)

        s_c = jnp.where(ok_c, _dot_nt(qb, kcmp_s[hh]) * scale - slope * dist_cf, NEG)
        m_c = jnp.max(s_c, axis=1, keepdims=True)
        e_c = jnp.where(ok_c, jnp.exp(s_c - m_c), 0.0)
        p_c = e_c / jnp.maximum(jnp.sum(e_c, axis=1, keepdims=True), 1e-30)
        o_cmp.append(_dot(p_c, vcmp_s[hh]))

        imp = _dot_split_nt(overlap_t, p_c)[0:n_sel, :]
        imp = jnp.where(rowb == cur, BIG, jnp.where(rowb < cur, imp, NEG))
        sel = (_rank_rows(imp) < NSA_SEL_TOPN) & (rowb <= cur)
        unsel = jnp.where(sel, 0.0, 1.0)
        heads.append(dict(qb=qb, k_s=ksn_s.at[hh], v_s=vsb_s.at[hh], slope=slope, blk_shift=NSA_SEL_SHIFT,
                          unsel=_pad_rows(unsel, HEAD_DIM).astype(BF16)))

        s_w = _dot_nt(qb, kwn_s[hh, win_rows, :]) * scale + jnp.where(ok_w, (-slope) * dist_wf, NEG)
        p_w = jnp.exp(s_w - jnp.max(s_w, axis=1, keepdims=True))
        o_win.append(_dot(p_w, vwb_s[hh, win_rows, :]) / jnp.sum(p_w, axis=1, keepdims=True))

    o_slc = _block_masked_attention(heads, qi, strip_s)

    zg = zg_ref[0]
    for hh in range(HEADS_PER_STEP):
        h = hp * HEADS_PER_STEP + hh
        g_cmp = _sigmoid(_lane_col(zg, GL_NG + h))
        g_slc = _sigmoid(_lane_col(zg, GL_NG + HEADS + h))
        g_win = _sigmoid(_lane_col(zg, GL_NG + 2 * HEADS + h))
        y_ref[0, :, _head_cols(hh)] = (g_cmp * o_cmp[hh] + g_slc * o_slc[hh] + g_win * o_win[hh]).astype(y_ref.dtype)


def _nsa(z3, zg3, q_g, k_g, cmp_pos, cmp_w1, cmp_w2, slopes):
    bsz, seq, _ = z3.shape
    assert seq // NSA_CMP_STRIDE == HEAD_DIM, "compressed blocks are laid out on the 128 lanes"
    assert seq % STRIP_W == 0 and seq >= NSA_WINDOW + ATT_TQ
    tq = ATT_TQ
    hps = HEADS_PER_STEP
    wide = hps * HEAD_DIM
    per_group = HEADS // hps
    kv = lambda group: pl.BlockSpec((1, seq, wide), lambda b, h, i: (b, 0, group * per_group + h))
    one = lambda group, hh: pl.BlockSpec((1, seq, HEAD_DIM), lambda b, h, i: (b, 0, group * HEADS + h * hps + hh))
    full = lambda shape: pl.BlockSpec(shape, lambda b, h, i: (0,) * len(shape))
    w1 = cmp_w1.reshape(2, NSA_CMP_LEN, HEAD_DIM, HEAD_DIM).astype(MXU_DTYPE)
    w2 = cmp_w2.astype(MXU_DTYPE)
    seq_buf = lambda: pltpu.VMEM((hps, seq, HEAD_DIM), MXU_DTYPE)
    cmp_buf = lambda: pltpu.VMEM((hps, HEAD_DIM, HEAD_DIM), MXU_DTYPE)
    return pl.pallas_call(
        _nsa_kernel,
        grid=(bsz, per_group, seq // tq),
        in_specs=[pl.BlockSpec(memory_space=pltpu.SMEM),
                  pl.BlockSpec((1, tq, wide), lambda b, h, i: (b, i, G_NQ * per_group + h)),
                  one(G_NKC, 0), one(G_NKC, 1), one(G_NVC, 0), one(G_NVC, 1),
                  kv(G_NKS), kv(G_NVS), kv(G_NKW), kv(G_NVW),
                  pl.BlockSpec((1, tq, GATE_W), lambda b, h, i: (b, i, 0)),
                  full((1, HEAD_DIM)), full((3, HEAD_DIM)), full((2, NSA_CMP_LEN, HEAD_DIM)),
                  full((2, NSA_CMP_LEN, HEAD_DIM, HEAD_DIM)), full((2, HEAD_DIM, HEAD_DIM))],
        out_specs=pl.BlockSpec((1, tq, wide), lambda b, h, i: (b, i, h)),
        out_shape=jax.ShapeDtypeStruct((bsz, seq, GROUP_W), BF16),
        scratch_shapes=[cmp_buf(), cmp_buf(), seq_buf(), seq_buf(), seq_buf(), seq_buf(),
                        pltpu.VMEM((hps, seq // STRIP_W, tq, STRIP_W), F32)],
        compiler_params=_cparams(("arbitrary", "arbitrary", "arbitrary")),
        name="nsa",
    )(slopes, z3, z3, z3, z3, z3, z3, z3, z3, z3, zg3, q_g.reshape(1, HEAD_DIM), k_g, cmp_pos, w1, w2)


def _out_proj_kernel(ym_ref, yb_ref, yr_ref, yn_ref, w_ref, x_ref, g_ref, o_ref):
    acc = jnp.dot(ym_ref[...], w_ref[0, 0:GROUP_W, :], preferred_element_type=F32)
    acc = acc + jnp.dot(yb_ref[...], w_ref[0, GROUP_W:2 * GROUP_W, :], preferred_element_type=F32)
    acc = acc + jnp.dot(yr_ref[...], w_ref[0, 2 * GROUP_W:3 * GROUP_W, :], preferred_element_type=F32)
    acc = acc + jnp.dot(yn_ref[...], w_ref[0, 3 * GROUP_W:4 * GROUP_W, :], preferred_element_type=F32)
    o_ref[...] = x_ref[...] + g_ref[0] * acc


def _out_proj(ys, w_out, layer, x2d, mod3, seq):
    n, d = x2d.shape
    tm, tn = 1024, 1024
    per_b = seq // tm
    y_spec = pl.BlockSpec((tm, GROUP_W), lambda i, j: (i, 0))
    return pl.pallas_call(
        _out_proj_kernel,
        grid=(n // tm, d // tn),
        in_specs=[y_spec, y_spec, y_spec, y_spec,
                  pl.BlockSpec((1, 4 * GROUP_W, tn), lambda i, j: (layer, 0, j)),
                  pl.BlockSpec((tm, tn), lambda i, j: (i, j)),
                  pl.BlockSpec((1, 1, tn), lambda i, j: (i // per_b, 0, 2 * (d // tn) + j))],
        out_specs=pl.BlockSpec((tm, tn), lambda i, j: (i, j)),
        out_shape=jax.ShapeDtypeStruct((n, d), F32),
        compiler_params=_cparams(("arbitrary", "arbitrary")),
        name="out_proj",
    )(*[y.reshape(n, GROUP_W) for y in ys], w_out, x2d, mod3)


SLAB_E0, SLAB_E1, SLAB_R0, SLAB_R1, SLAB_G0, SLAB_G1 = 0, 1, 2, 3, 4, 5


def _route_kernel(x_ref, g_ref, sc_ref, sh_ref, wr_ref, br_ref, h_ref, slab_ref, cnt_ref, carry_s):
    tm = x_ref.shape[0]

    @pl.when(pl.program_id(0) == 0)
    def _():
        carry_s[...] = jnp.zeros(carry_s.shape, F32)

    hmod = _rms(x_ref[...], g_ref[...]) * (1.0 + sc_ref[0]) + sh_ref[0]
    h_ref[...] = hmod
    logits = _dot(hmod, wr_ref[...]) + br_ref[...]
    lane = lax.broadcasted_iota(jnp.int32, logits.shape, 1).astype(F32)
    far = 4.0 * GATE_W

    in_g = lane < N_GROUPS
    lg = jnp.where(in_g, logits, NEG)
    g_max = jnp.max(lg, axis=1, keepdims=True)
    grp = jnp.min(jnp.where(in_g & (lg == g_max), lane, far), axis=1, keepdims=True)
    p_grp = 1.0 / jnp.sum(jnp.where(in_g, jnp.exp(lg - g_max), 0.0), axis=1, keepdims=True)

    lo = N_GROUPS + grp * EXPERTS_PER_GROUP
    in_e = (lane >= lo) & (lane < lo + EXPERTS_PER_GROUP)
    le = jnp.where(in_e, logits, NEG)
    e_max = jnp.max(le, axis=1, keepdims=True)
    ee = jnp.where(in_e, jnp.exp(le - e_max), 0.0)
    pe = jnp.where(in_e, ee / jnp.sum(ee, axis=1, keepdims=True), -1.0)
    p1 = jnp.max(pe, axis=1, keepdims=True)
    i1 = jnp.min(jnp.where(pe == p1, lane, far), axis=1, keepdims=True)
    pe2 = jnp.where(lane == i1, -1.0, pe)
    p2 = jnp.max(pe2, axis=1, keepdims=True)
    i2 = jnp.min(jnp.where(pe2 == p2, lane, far), axis=1, keepdims=True)
    e0 = i1 - N_GROUPS
    e1 = i2 - N_GROUPS
    g0 = p_grp * (p1 / (p1 + p2))
    g1 = p_grp * (p2 / (p1 + p2))

    onehot = ((lane == e0) | (lane == e1)).astype(BF16)
    rr = lax.broadcasted_iota(jnp.int32, (tm, tm), 0)
    cc = lax.broadcasted_iota(jnp.int32, (tm, tm), 1)
    before = jnp.dot((cc < rr).astype(BF16), onehot, preferred_element_type=F32) + carry_s[...]
    r0 = jnp.sum(jnp.where(lane == e0, before, 0.0), axis=1, keepdims=True)
    r1 = jnp.sum(jnp.where(lane == e1, before, 0.0), axis=1, keepdims=True)
    carry_s[...] = carry_s[...] + jnp.sum(onehot.astype(F32), axis=0, keepdims=True)
    cnt_ref[...] = carry_s[...]

    slab = jnp.where(lane == SLAB_E0, e0.astype(F32), 0.0)
    slab = jnp.where(lane == SLAB_E1, e1.astype(F32), slab)
    slab = jnp.where(lane == SLAB_R0, r0, slab)
    slab = jnp.where(lane == SLAB_R1, r1, slab)
    slab = jnp.where(lane == SLAB_G0, g0, slab)
    slab = jnp.where(lane == SLAB_G1, g1, slab)
    slab_ref[...] = slab


def _route(x2d, norm_g, mod3, w_router, b_router, seq):
    n, d = x2d.shape
    tm = 512
    per_b = seq // tm
    return pl.pallas_call(
        _route_kernel,
        grid=(n // tm,),
        in_specs=[pl.BlockSpec((tm, d), lambda i: (i, 0)),
                  pl.BlockSpec((1, d), lambda i: (0, 0)),
                  pl.BlockSpec((1, 1, d), lambda i: (i // per_b, 0, 4)),
                  pl.BlockSpec((1, 1, d), lambda i: (i // per_b, 0, 3)),
                  pl.BlockSpec((d, GATE_W), lambda i: (0, 0)),
                  pl.BlockSpec((1, GATE_W), lambda i: (0, 0))],
        out_specs=[pl.BlockSpec((tm, d), lambda i: (i, 0)),
                   pl.BlockSpec((tm, GATE_W), lambda i: (i, 0)),
                   pl.BlockSpec((1, GATE_W), lambda i: (0, 0))],
        out_shape=[jax.ShapeDtypeStruct((n, d), F32),
                   jax.ShapeDtypeStruct((n, GATE_W), F32),
                   jax.ShapeDtypeStruct((1, GATE_W), F32)],
        scratch_shapes=[pltpu.VMEM((1, GATE_W), F32)],
        compiler_params=_cparams(("arbitrary",)),
        name="route",
    )(x2d, norm_g.reshape(1, d), mod3, mod3, w_router, b_router)


def _invert_kernel(dest_ref, inv_ref):
    def clear(i, _):
        inv_ref[i] = 0
        return 0

    def put(a, _):
        inv_ref[dest_ref[a]] = jnp.right_shift(a, 1)
        return 0

    lax.fori_loop(0, inv_ref.shape[0], clear, 0, unroll=8)
    lax.fori_loop(0, dest_ref.shape[0], put, 0, unroll=8)


def _invert(dest_flat, cap):
    smem = pl.BlockSpec(memory_space=pltpu.SMEM)
    return pl.pallas_call(
        _invert_kernel,
        in_specs=[smem],
        out_specs=smem,
        out_shape=jax.ShapeDtypeStruct((cap,), jnp.int32),
        name="invert",
    )(dest_flat)


def _start_row_gather(idx_ref, src_hbm, dst, sem, n_rows):
    def start(r, _):
        pltpu.make_async_copy(src_hbm.at[pl.ds(idx_ref[r], 1)], dst.at[pl.ds(r, 1)], sem).start()
        return 0

    lax.fori_loop(0, n_rows, start, 0, unroll=8)


def _wait_row_gather(src_hbm, dst, sem, n_rows):
    pltpu.make_async_copy(src_hbm.at[pl.ds(0, n_rows)], dst, sem).wait()


def _expert_kernel(blk_e_ref, n_used_ref, inv_ref, h_hbm, w1_ref, w3_ref, w2_ref, y_ref,
                   x_s, w1_s, w3_s, w2_s, sem):
    s = pl.program_id(0)
    rows = x_s.shape[1]
    n_used = n_used_ref[0]
    slot = lax.rem(s, 2)

    @pl.when(s < n_used)
    def _():
        _start_row_gather(inv_ref, h_hbm, x_s.at[slot], sem.at[slot], rows)

    blk = s - 1
    prev = jnp.maximum(blk - 1, 0)

    @pl.when((s >= 1) & (blk < n_used))
    def _():
        @pl.when((blk == 0) | (blk_e_ref[blk] != blk_e_ref[prev]))
        def _():
            w1_s[...] = w1_ref[0, 0].astype(w1_s.dtype)
            w3_s[...] = w3_ref[0, 0].astype(w3_s.dtype)
            w2_s[...] = w2_ref[0, 0].astype(w2_s.dtype)

        _wait_row_gather(h_hbm, x_s.at[1 - slot], sem.at[1 - slot], rows)
        x = x_s[1 - slot].astype(MXU_DTYPE)
        a = jnp.dot(x, w1_s[...], preferred_element_type=F32)
        b = jnp.dot(x, w3_s[...], preferred_element_type=F32)
        y_ref[...] = _dot(a * _sigmoid(a) * b, w2_s[...])

    @pl.when((s >= 1) & (blk >= n_used))
    def _():
        y_ref[...] = jnp.zeros(y_ref.shape, y_ref.dtype)


def _experts(h2, inv, blk_e, n_used, w1, w3, w2, layer):
    n, d = h2.shape
    rows = EXPERT_ROWS
    n_blocks = inv.shape[0] // rows
    done = lambda s: jnp.maximum(s - 1, 0)
    w_map = lambda s, be, nu: (layer, be[done(s)], 0, 0)
    return pl.pallas_call(
        _expert_kernel,
        grid_spec=pltpu.PrefetchScalarGridSpec(
            num_scalar_prefetch=2,
            grid=(n_blocks + 1,),
            in_specs=[pl.BlockSpec((rows,), lambda s, be, nu: (jnp.minimum(s, n_blocks - 1),),
                                   memory_space=pltpu.SMEM),
                      pl.BlockSpec(memory_space=pl.ANY),
                      pl.BlockSpec((1, 1, d, D_EXPERT), w_map),
                      pl.BlockSpec((1, 1, d, D_EXPERT), w_map),
                      pl.BlockSpec((1, 1, D_EXPERT, d), w_map)],
            out_specs=pl.BlockSpec((rows, d), lambda s, be, nu: (done(s), 0)),
            scratch_shapes=[pltpu.VMEM((2, rows, d), F32),
                            pltpu.VMEM((d, D_EXPERT), MXU_DTYPE), pltpu.VMEM((d, D_EXPERT), MXU_DTYPE),
                            pltpu.VMEM((D_EXPERT, d), MXU_DTYPE),
                            pltpu.SemaphoreType.DMA((2,))]),
        out_shape=jax.ShapeDtypeStruct((n_blocks * rows, d), F32),
        compiler_params=_cparams(("arbitrary",)),
        name="experts",
    )(blk_e, n_used, inv, h2, w1, w3, w2)


def _combine_kernel(d0_ref, d1_ref, x_ref, g_ref, slab_ref, yb_hbm, o_ref, rows_s, sem):
    s = pl.program_id(0)
    n_tiles = pl.num_programs(0) - 1
    tm = x_ref.shape[0]
    slot = lax.rem(s, 2)

    @pl.when(s < n_tiles)
    def _():
        _start_row_gather(d0_ref, yb_hbm, rows_s.at[slot, 0], sem.at[slot], tm)
        _start_row_gather(d1_ref, yb_hbm, rows_s.at[slot, 1], sem.at[slot], tm)

    @pl.when(s >= 1)
    def _():
        _wait_row_gather(yb_hbm, rows_s.at[1 - slot, 0], sem.at[1 - slot], tm)
        _wait_row_gather(yb_hbm, rows_s.at[1 - slot, 1], sem.at[1 - slot], tm)
        slab = slab_ref[...]
        moe = (slab[:, SLAB_G0:SLAB_G0 + 1] * rows_s[1 - slot, 0]
               + slab[:, SLAB_G1:SLAB_G1 + 1] * rows_s[1 - slot, 1])
        o_ref[...] = x_ref[...] + g_ref[0] * moe


def _combine(x2d, mod3, slab, dest0, dest1, yb, seq):
    n, d = x2d.shape
    tm = 256
    per_b = seq // tm
    n_tiles = n // tm
    done = lambda s: jnp.maximum(s - 1, 0)
    idx_spec = pl.BlockSpec((tm,), lambda s: (jnp.minimum(s, n_tiles - 1),), memory_space=pltpu.SMEM)
    return pl.pallas_call(
        _combine_kernel,
        grid=(n_tiles + 1,),
        in_specs=[idx_spec, idx_spec,
                  pl.BlockSpec((tm, d), lambda s: (done(s), 0)),
                  pl.BlockSpec((1, 1, d), lambda s: (done(s) // per_b, 0, 5)),
                  pl.BlockSpec((tm, GATE_W), lambda s: (done(s), 0)),
                  pl.BlockSpec(memory_space=pl.ANY)],
        out_specs=pl.BlockSpec((tm, d), lambda s: (done(s), 0)),
        out_shape=jax.ShapeDtypeStruct((n, d), F32),
        scratch_shapes=[pltpu.VMEM((2, 2, tm, d), F32), pltpu.SemaphoreType.DMA((2,))],
        compiler_params=_cparams(("arbitrary",)),
        name="combine",
    )(dest0, dest1, x2d, mod3, slab, yb)


_OFF_MI = 4 * GROUP_W
_OFF_BQ = _OFF_MI + 2 * HEADS
_OFF_NG = _OFF_BQ + 14 * GROUP_W


def _pack_w_kernel(w_ref, o_ref):
    gap = _OFF_BQ - _OFF_MI
    o_ref[0, :, 0:_OFF_MI] = w_ref[0, :, 0:_OFF_MI].astype(o_ref.dtype)
    tail = w_ref[0, :, _OFF_MI:]
    width = tail.shape[1]
    o_ref[0, :, _OFF_MI:] = pltpu.roll(tail, width - gap, axis=1)[:, 0:D_WIDE - _OFF_MI].astype(o_ref.dtype)


def _pack_w_in(w_in):
    depth, d, d_in = w_in.shape
    tk = 128
    lanes_in = -(-d_in // GATE_W) * GATE_W
    return pl.pallas_call(
        _pack_w_kernel,
        grid=(depth, d // tk),
        in_specs=[pl.BlockSpec((1, tk, lanes_in), lambda l, i: (l, i, 0))],
        out_specs=pl.BlockSpec((1, tk, D_WIDE), lambda l, i: (l, i, 0)),
        out_shape=jax.ShapeDtypeStruct((depth, d, D_WIDE), MXU_DTYPE),
        compiler_params=_cparams(("arbitrary", "arbitrary")),
        name="pack_w_in",
    )(w_in)


def _gate_w_in(w_in):
    d = w_in.shape[0]
    n_gate = 2 * HEADS + 3 * HEADS
    w_gate = jnp.concatenate([w_in[:, _OFF_MI:_OFF_BQ], w_in[:, _OFF_NG:_OFF_NG + 3 * HEADS],
                              jnp.zeros((d, GATE_W - n_gate), w_in.dtype)], axis=1)
    return w_gate.astype(MXU_DTYPE)


def _moe(x2d, norm_g, mod3, wg, bg, we, be, w1, w3, w2, layer, seq):
    n, d = x2d.shape
    n_route = N_GROUPS + N_EXPERTS
    w_router = jnp.concatenate([wg, we, jnp.zeros((d, GATE_W - n_route), wg.dtype)], axis=1).astype(MXU_DTYPE)
    b_router = jnp.concatenate([bg, be, jnp.zeros((GATE_W - n_route,), bg.dtype)]).reshape(1, GATE_W)
    h2, slab, cnt = _route(x2d, norm_g, mod3, w_router, b_router, seq)

    rows = EXPERT_ROWS
    counts = cnt[0, :N_EXPERTS].astype(jnp.int32)
    pcounts = (counts + rows - 1) // rows * rows
    pends = jnp.cumsum(pcounts)
    pstarts = pends - pcounts
    eid = slab[:, SLAB_E0:SLAB_E1 + 1].astype(jnp.int32)
    rank = slab[:, SLAB_R0:SLAB_R1 + 1].astype(jnp.int32)
    dest = pstarts[eid] + rank
    n_blocks = -(-2 * n // rows) + N_EXPERTS
    blk_row0 = jnp.arange(n_blocks, dtype=jnp.int32) * rows
    blk_e = jnp.minimum(jnp.sum((pends[None, :] <= blk_row0[:, None]).astype(jnp.int32), axis=1), N_EXPERTS - 1)
    n_used = (pends[-1:] // rows).astype(jnp.int32)

    inv = _invert(dest.reshape(-1), n_blocks * rows)
    yb = _experts(h2, inv, blk_e, n_used, w1, w3, w2, layer)
    return _combine(x2d, mod3, slab, dest[:, 0], dest[:, 1], yb, seq)


def _layer(x2d, mod, bsz, seq, layer, norm1_g, norm2_g, w_wide, w_in, mlstm_gate_b, mlstm_conv_w, mlstm_out_g,
           moba_qk_g, ret_out_g, nsa_q_g, nsa_k_g, nsa_cmp_pos, nsa_cmp_w1, nsa_cmp_w2, w_out, router_g_w,
           router_g_b, router_e_w, router_e_b, exp_w1, exp_w3, exp_w2, slopes, log_gamma):
    n, d = x2d.shape
    mod3 = mod.reshape(bsz, 1, 6 * d)
    z, zg = _norm_in_proj(x2d, norm1_g, mod3, w_wide, layer, _gate_w_in(w_in), seq)
    z3 = z.reshape(bsz, seq, D_WIDE)
    zg3 = zg.reshape(bsz, seq, GATE_W)
    y_m = _mlstm(z3, zg3, mlstm_gate_b, mlstm_conv_w, mlstm_out_g)
    y_b = _moba(z3, moba_qk_g, slopes[0::2])
    y_r = _retention(z3, log_gamma, ret_out_g)
    y_n = _nsa(z3, zg3, nsa_q_g, nsa_k_g, nsa_cmp_pos, nsa_cmp_w1, nsa_cmp_w2, slopes[1::2])
    x2d = _out_proj((y_m, y_b, y_r, y_n), w_out, layer, x2d, mod3, seq)
    return _moe(x2d, norm2_g, mod3, router_g_w, router_g_b, router_e_w, router_e_b, exp_w1, exp_w3, exp_w2,
                layer, seq)


def kernel(x, c, norm1_g, norm2_g, ada_w, ada_b, w_in, mlstm_gate_b, mlstm_conv_w, mlstm_out_g, moba_qk_g,
           ret_out_g, nsa_q_g, nsa_k_g, nsa_cmp_pos, nsa_cmp_w1, nsa_cmp_w2, w_out, router_g_w, router_g_b,
           router_e_w, router_e_b, exp_w1, exp_w3, exp_w2):
    bsz, seq, d = x.shape
    depth = ada_w.shape[0]
    n_softmax_heads = 2 * HEADS
    slopes = jnp.exp2(-8.0 * jnp.arange(1, n_softmax_heads + 1, dtype=F32) / n_softmax_heads)
    log_gamma = jnp.log(1.0 - jnp.exp2(-5.0 - jnp.arange(HEADS, dtype=F32)))
    mod = _ada_mod(c, ada_w, ada_b)
    w_wide = _pack_w_in(w_in)
    w_out_b = w_out.astype(MXU_DTYPE)
    x2d = x.reshape(bsz * seq, d)
    for l in range(depth):
        x2d = _layer(x2d, mod[l], bsz, seq, l, norm1_g[l], norm2_g[l], w_wide, w_in[l], mlstm_gate_b[l],
                     mlstm_conv_w[l], mlstm_out_g[l], moba_qk_g[l], ret_out_g[l], nsa_q_g[l], nsa_k_g[l],
                     nsa_cmp_pos[l], nsa_cmp_w1[l], nsa_cmp_w2[l], w_out_b, router_g_w[l], router_g_b[l],
                     router_e_w[l], router_e_b[l], exp_w1, exp_w3, exp_w2, slopes, log_gamma)
    return x2d.reshape(bsz, seq, d)
```

```python
import functools

import jax
import jax.numpy as jnp
from jax import lax
from jax.experimental import pallas as pl
from jax.experimental.pallas import tpu as pltpu

F32 = jnp.float32
BF16 = jnp.bfloat16
MXU_DTYPE = jnp.bfloat16

D_MODEL = 2048
HEAD_DIM = 128
HEADS = 4
GROUP_W = HEADS * HEAD_DIM
N_WIDE_GROUPS = 18
D_WIDE = N_WIDE_GROUPS * GROUP_W
GATE_W = 128

MLSTM_CHUNK = 64
MLSTM_CONV = 4
MOBA_BLOCK = 256
MOBA_SHIFT = 8
MOBA_TOPK = 3
RET_CHUNK = 128
NSA_CMP_LEN = 32
NSA_CMP_STRIDE = 16
NSA_SEL_BLOCK = 64
NSA_SEL_SHIFT = 6
NSA_SEL_TOPN = 4
NSA_WINDOW = 512
N_GROUPS = 4
EXPERTS_PER_GROUP = 8
N_EXPERTS = N_GROUPS * EXPERTS_PER_GROUP
D_EXPERT = 512
EXPERT_ROWS = 256

NORM_EPS = 1e-6
NEG = -1e30
BIG = 1e9

G_MQ, G_MK, G_MV, G_MO = 0, 1, 2, 3
G_BQ, G_BK, G_BV = 4, 5, 6
G_RQ, G_RK, G_RV, G_RG = 7, 8, 9, 10
G_NQ, G_NKC, G_NVC, G_NKS, G_NVS, G_NKW, G_NVW = 11, 12, 13, 14, 15, 16, 17
GL_MI, GL_MF, GL_NG = 0, 4, 8

VMEM_LIMIT = 56 * 1024 * 1024


def _cparams(sem):
    return pltpu.CompilerParams(dimension_semantics=sem, vmem_limit_bytes=VMEM_LIMIT)


def _dot(a, b):
    return jnp.dot(a.astype(MXU_DTYPE), b.astype(MXU_DTYPE), preferred_element_type=F32)


def _dot_nt(a, b):
    return lax.dot_general(a.astype(MXU_DTYPE), b.astype(MXU_DTYPE), (((1,), (1,)), ((), ())),
                           preferred_element_type=F32)


def _dot_tn(a, b):
    return lax.dot_general(a.astype(MXU_DTYPE), b.astype(MXU_DTYPE), (((0,), (0,)), ((), ())),
                           preferred_element_type=F32)


def _dot_split_nt(b01, a):
    a_hi = a.astype(MXU_DTYPE)
    r1 = a - a_hi.astype(F32)
    a_mid = r1.astype(MXU_DTYPE)
    a_lo = (r1 - a_mid.astype(F32)).astype(MXU_DTYPE)
    return _dot_nt(b01, a_hi) + _dot_nt(b01, a_mid) + _dot_nt(b01, a_lo)


def _rms(x, g):
    return x * lax.rsqrt(jnp.mean(x * x, axis=-1, keepdims=True) + NORM_EPS) * g


def _sigmoid(x):
    return jax.nn.sigmoid(x)


def _lane_col(x, idx):
    lane = lax.broadcasted_iota(jnp.int32, x.shape, 1)
    return jnp.sum(jnp.where(lane == idx, x, 0.0), axis=1, keepdims=True)


def _ada_kernel(c_ref, w_ref, b_ref, o_ref):
    c = c_ref[...]
    o_ref[0] = _dot(c * _sigmoid(c), w_ref[0]) + b_ref[0]


def _ada_mod(c, ada_w, ada_b):
    depth, d, n6 = ada_w.shape
    b = c.shape[0]
    tn = 1024
    return pl.pallas_call(
        _ada_kernel,
        grid=(depth, n6 // tn),
        in_specs=[pl.BlockSpec((b, d), lambda l, j: (0, 0)),
                  pl.BlockSpec((1, d, tn), lambda l, j: (l, 0, j)),
                  pl.BlockSpec((1, 1, tn), lambda l, j: (l, 0, j))],
        out_specs=pl.BlockSpec((1, b, tn), lambda l, j: (l, 0, j)),
        out_shape=jax.ShapeDtypeStruct((depth, b, n6), F32),
        compiler_params=_cparams(("arbitrary", "arbitrary")),
        name="ada_mod",
    )(c, ada_w, ada_b.reshape(depth, 1, n6))


def _norm_in_kernel(x_ref, g_ref, sc_ref, sh_ref, w_ref, ws_ref, z_ref, zg_ref, h_s):
    @pl.when(pl.program_id(1) == 0)
    def _():
        h = _rms(x_ref[...], g_ref[...]) * (1.0 + sc_ref[0]) + sh_ref[0]
        hb = h.astype(MXU_DTYPE)
        h_s[...] = hb
        zg_ref[...] = jnp.dot(hb, ws_ref[0], preferred_element_type=F32)

    z_ref[...] = jnp.dot(h_s[...], w_ref[0], preferred_element_type=F32)


def _norm_in_proj(x2d, norm_g, mod3, w_wide, layer, w_gate, seq):
    n, d = x2d.shape
    tm, tn = 1024, 1024
    per_b = seq // tm
    return pl.pallas_call(
        _norm_in_kernel,
        grid=(n // tm, D_WIDE // tn),
        in_specs=[pl.BlockSpec((tm, d), lambda i, j: (i, 0)),
                  pl.BlockSpec((1, d), lambda i, j: (0, 0)),
                  pl.BlockSpec((1, 1, d), lambda i, j: (i // per_b, 0, 1)),
                  pl.BlockSpec((1, 1, d), lambda i, j: (i // per_b, 0, 0)),
                  pl.BlockSpec((1, d, tn), lambda i, j: (layer, 0, j)),
                  pl.BlockSpec((1, d, GATE_W), lambda i, j: (layer, 0, 0))],
        out_specs=[pl.BlockSpec((tm, tn), lambda i, j: (i, j)),
                   pl.BlockSpec((tm, GATE_W), lambda i, j: (i, 0))],
        out_shape=[jax.ShapeDtypeStruct((n, D_WIDE), F32),
                   jax.ShapeDtypeStruct((n, GATE_W), F32)],
        scratch_shapes=[pltpu.VMEM((tm, d), MXU_DTYPE)],
        compiler_params=_cparams(("arbitrary", "arbitrary")),
        name="norm_in_proj",
    )(x2d, norm_g.reshape(1, d), mod3, mod3, w_wide, w_gate)


REC_HEADS_PER_STEP = 2


def _mlstm_kernel(gb_ref, q_ref, k_ref, v_ref, o_ref, zg_ref, cwq_ref, cwk_ref, og_ref, y_ref,
                  pad_s, qs_s, ks_s, ic_s, fc_s):
    hp = pl.program_id(1)
    t = q_ref.shape[1]
    cl = MLSTM_CHUNK
    nc = t // cl
    hps = REC_HEADS_PER_STEP

    def conv_silu(src_ref, cw_ref, cols, dst_s, scale):
        pad_s[8:8 + t, :] = src_ref[0, :, cols]
        off = 8 - (MLSTM_CONV - 1)
        tile = 128
        for r0 in range(0, t, tile):
            acc = cw_ref[0:1, cols] * pad_s[r0 + off:r0 + off + tile, :]
            for j in range(1, MLSTM_CONV):
                acc = acc + cw_ref[j:j + 1, cols] * pad_s[r0 + off + j:r0 + off + j + tile, :]
            dst_s[r0:r0 + tile, :] = acc * _sigmoid(acc) * scale

    pad_s[0:8, :] = jnp.zeros((8, HEAD_DIM), F32)
    zg = zg_ref[0]
    for hh in range(hps):
        h = hp * hps + hh
        conv_silu(q_ref, cwq_ref, _head_cols(hh), qs_s.at[hh], HEAD_DIM ** -0.5)
        conv_silu(k_ref, cwk_ref, _head_cols(hh), ks_s.at[hh], 1.0)
        ic_s[hh] = _lane_col(zg, GL_MI + h) + gb_ref[0, h]
        f_pre = _lane_col(zg, GL_MF + h) + gb_ref[1, h]
        fc_s[hh] = jnp.minimum(f_pre, 0.0) - jnp.log1p(jnp.exp(-jnp.abs(f_pre)))

    rr = lax.broadcasted_iota(jnp.int32, (cl, cl), 0)
    cc = lax.broadcasted_iota(jnp.int32, (cl, cl), 1)
    eye = rr == cc
    causal = cc <= rr
    og = og_ref[...]

    def head_step(hh, c, carry):
        c_st, n_st, m_st = carry
        sl = pl.ds(pl.multiple_of(c * cl, cl), cl)
        cols = _head_cols(hh)
        qc = qs_s[hh, sl, :]
        kc = ks_s[hh, sl, :]
        vc = v_ref[0, sl, cols]
        i_col = ic_s[hh, sl, :]
        f_col = fc_s[hh, sl, :]
        f_row = jnp.sum(jnp.where(eye, f_col, 0.0), axis=0, keepdims=True)
        i_row = jnp.sum(jnp.where(eye, i_col, 0.0), axis=0, keepdims=True)
        a_col = jnp.sum(jnp.where(causal, f_row, 0.0), axis=1, keepdims=True)
        a_row = jnp.sum(jnp.where(rr <= cc, f_col, 0.0), axis=0, keepdims=True)
        log_d = jnp.where(causal, a_col - a_row + i_row, NEG)
        m_inter = a_col + m_st
        m_row = jnp.maximum(m_inter, jnp.max(log_d, axis=1, keepdims=True))
        s = _dot_nt(qc, kc) * jnp.exp(log_d - m_row)
        w_inter = jnp.exp(m_inter - m_row)
        num = _dot(s, vc) + w_inter * _dot(qc, c_st)
        den = jnp.sum(s, axis=1, keepdims=True) + w_inter * jnp.sum(qc * n_st, axis=1, keepdims=True)
        h_out = num / jnp.maximum(jnp.abs(den), jnp.exp(-m_row))
        a_last = jnp.sum(f_col, axis=0, keepdims=True)
        w_log = a_last - a_col + i_col
        m_new = jnp.maximum(a_last + m_st, jnp.max(w_log, axis=0, keepdims=True))
        w = jnp.exp(w_log - m_new)
        decay = jnp.exp(a_last + m_st - m_new)
        kw = kc * w
        c_new = decay * c_st + _dot_tn(kw, vc)
        n_new = decay * n_st + jnp.sum(kw, axis=0, keepdims=True)
        y = _rms(h_out, og) * _sigmoid(o_ref[0, sl, cols])
        y_ref[0, sl, cols] = y.astype(y_ref.dtype)
        return c_new, n_new, m_new

    def body(c, carry):
        return tuple(head_step(hh, c, carry[hh]) for hh in range(hps))

    carry0 = (jnp.zeros((HEAD_DIM, HEAD_DIM), F32), jnp.zeros((1, HEAD_DIM), F32), jnp.zeros((1, 1), F32))
    lax.fori_loop(0, nc, body, (carry0,) * hps)


def _col_spec(seq, group, hps):
    return pl.BlockSpec((1, seq, hps * HEAD_DIM), lambda b, h: (b, 0, group * (HEADS // hps) + h))


def _mlstm(z3, zg3, gate_b, conv_w, out_g):
    bsz, seq, _ = z3.shape
    hps = REC_HEADS_PER_STEP
    wide = hps * HEAD_DIM
    per_group = HEADS // hps
    smem = pl.BlockSpec(memory_space=pltpu.SMEM)
    return pl.pallas_call(
        _mlstm_kernel,
        grid=(bsz, per_group),
        in_specs=[smem,
                  _col_spec(seq, G_MQ, hps), _col_spec(seq, G_MK, hps), _col_spec(seq, G_MV, hps),
                  _col_spec(seq, G_MO, hps),
                  pl.BlockSpec((1, seq, GATE_W), lambda b, h: (b, 0, 0)),
                  pl.BlockSpec((MLSTM_CONV, wide), lambda b, h: (0, h)),
                  pl.BlockSpec((MLSTM_CONV, wide), lambda b, h: (0, per_group + h)),
                  pl.BlockSpec((1, HEAD_DIM), lambda b, h: (0, 0))],
        out_specs=pl.BlockSpec((1, seq, wide), lambda b, h: (b, 0, h)),
        out_shape=jax.ShapeDtypeStruct((bsz, seq, GROUP_W), BF16),
        scratch_shapes=[pltpu.VMEM((seq + 8, HEAD_DIM), F32), pltpu.VMEM((hps, seq, HEAD_DIM), F32),
                        pltpu.VMEM((hps, seq, HEAD_DIM), F32), pltpu.VMEM((hps, seq, 1), F32),
                        pltpu.VMEM((hps, seq, 1), F32)],
        compiler_params=_cparams(("arbitrary", "arbitrary")),
        name="mlstm",
    )(gate_b, z3, z3, z3, z3, zg3, conv_w, conv_w, out_g.reshape(1, HEAD_DIM))


def _ret_kernel(lg_ref, q_ref, k_ref, v_ref, g_ref, og_ref, y_ref):
    t = q_ref.shape[1]
    cl = RET_CHUNK
    rr = lax.broadcasted_iota(jnp.int32, (cl, cl), 0)
    cc = lax.broadcasted_iota(jnp.int32, (cl, cl), 1)
    diff = (rr - cc).astype(F32)
    jcol = lax.broadcasted_iota(jnp.int32, (cl, 1), 0).astype(F32)
    og = og_ref[...]
    consts = []
    for h in range(HEADS):
        lg = lg_ref[h]
        consts.append(dict(decay_in=jnp.where(diff >= 0, jnp.exp(lg * jnp.maximum(diff, 0.0)), 0.0),
                           zeta=jnp.exp(lg * (cl - 1.0 - jcol)), xi=jnp.exp(lg * (jcol + 1.0)),
                           g_chunk=jnp.exp(jnp.full((1, 1), lg * cl, F32))))

    def head_step(h, c, r_st):
        sl = pl.ds(pl.multiple_of(c * cl, cl), cl)
        cols = _head_cols(h)
        qc = q_ref[0, sl, cols]
        kc = k_ref[0, sl, cols] * HEAD_DIM ** -0.5
        vc = v_ref[0, sl, cols]
        scores = _dot_nt(qc, kc) * consts[h]["decay_in"]
        o = _dot(scores, vc) + _dot(qc, r_st) * consts[h]["xi"]
        r_new = consts[h]["g_chunk"] * r_st + _dot_tn(kc * consts[h]["zeta"], vc)
        gg = g_ref[0, sl, cols]
        y = _rms(o, og) * (gg * _sigmoid(gg))
        y_ref[0, sl, cols] = y.astype(y_ref.dtype)
        return r_new

    def body(c, carry):
        return tuple(head_step(h, c, carry[h]) for h in range(HEADS))

    lax.fori_loop(0, t // cl, body, (jnp.zeros((HEAD_DIM, HEAD_DIM), F32),) * HEADS)


def _retention(z3, log_gamma, out_g):
    bsz, seq, _ = z3.shape
    grp = lambda group: pl.BlockSpec((1, seq, GROUP_W), lambda b: (b, 0, group))
    return pl.pallas_call(
        _ret_kernel,
        grid=(bsz,),
        in_specs=[pl.BlockSpec(memory_space=pltpu.SMEM), grp(G_RQ), grp(G_RK), grp(G_RV), grp(G_RG),
                  pl.BlockSpec((1, HEAD_DIM), lambda b: (0, 0))],
        out_specs=pl.BlockSpec((1, seq, GROUP_W), lambda b: (b, 0, 0)),
        out_shape=jax.ShapeDtypeStruct((bsz, seq, GROUP_W), BF16),
        compiler_params=_cparams(("arbitrary",)),
        name="retention",
    )(log_gamma, z3, z3, z3, z3, out_g.reshape(1, HEAD_DIM))


ATT_TQ = 256
STRIP_W = 512
STRIP_SHIFT = 9
HEADS_PER_STEP = 2


def _rank_rows(vals):
    n = vals.shape[0]
    rowb = lax.broadcasted_iota(jnp.int32, vals.shape, 0)
    rank = jnp.zeros(vals.shape, jnp.int32)
    for jp in range(n):
        rv = vals[jp:jp + 1, :]
        beats = (rv > vals) | ((rv == vals) & (rowb > jp))
        rank = rank + beats.astype(jnp.int32)
    return rank


def _pad_rows(x, rows):
    return jnp.concatenate([x, jnp.zeros((rows - x.shape[0], x.shape[1]), x.dtype)], axis=0)


def _fold_lanes(op, acc, x):
    for b in range(x.shape[1] // HEAD_DIM):
        acc = op(acc, x[:, b * HEAD_DIM:(b + 1) * HEAD_DIM])
    return acc


def _block_masked_attention(heads, qi, strip_s):
    tq, w = ATT_TQ, STRIP_W
    scale = HEAD_DIM ** -0.5
    t0 = qi * tq
    last = jnp.right_shift(t0, STRIP_SHIFT)
    rr = lax.broadcasted_iota(jnp.int32, (tq, w), 0)
    cc = lax.broadcasted_iota(jnp.int32, (tq, w), 1)
    rel = rr - cc
    rel_f = rel.astype(F32)
    alibi = [(-hd["slope"]) * rel_f for hd in heads]
    erow = lax.broadcasted_iota(jnp.int32, (HEAD_DIM, w), 0)
    ecol = lax.broadcasted_iota(jnp.int32, (HEAD_DIM, w), 1)

    def scores(hd, c, bias):
        expand = jnp.where(erow == jnp.right_shift(c * w + ecol, hd["blk_shift"]), NEG, 0.0).astype(BF16)
        mask_bias = lax.dot_general(hd["unsel"], expand, (((0,), (0,)), ((), ())), preferred_element_type=F32)
        kc = hd["k_s"][pl.ds(pl.multiple_of(c * w, w), w), :]
        return _dot_nt(hd["qb"], kc) * scale + bias + mask_bias

    def first_pass(c, ms):
        out = []
        for hi, hd in enumerate(heads):
            s = scores(hd, c, alibi[hi] + (-hd["slope"]) * (t0 - c * w).astype(F32))
            strip_s[hi, c] = s
            out.append(_fold_lanes(jnp.maximum, ms[hi], s))
        return tuple(out)

    ms = lax.fori_loop(0, last, first_pass, tuple(jnp.full((tq, HEAD_DIM), NEG, F32) for _ in heads))
    dist = rel + (t0 - last * w)
    row_max = []
    for hi, hd in enumerate(heads):
        bias = jnp.where(dist >= 0, (-hd["slope"]) * dist.astype(F32), NEG)
        s = scores(hd, last, bias)
        strip_s[hi, last] = s
        row_max.append(jnp.max(_fold_lanes(jnp.maximum, ms[hi], s), axis=1, keepdims=True))

    def second_pass(c, carry):
        out = []
        for hi, hd in enumerate(heads):
            l_run, acc = carry[hi]
            p = jnp.exp(strip_s[hi, c] - row_max[hi])
            vc = hd["v_s"][pl.ds(pl.multiple_of(c * w, w), w), :]
            out.append((_fold_lanes(jnp.add, l_run, p), acc + _dot(p, vc)))
        return tuple(out)

    zero = jnp.zeros((tq, HEAD_DIM), F32)
    res = lax.fori_loop(0, last + 1, second_pass, tuple((zero, zero) for _ in heads))
    return [acc / jnp.sum(l_run, axis=1, keepdims=True) for l_run, acc in res]


def _head_cols(hh):
    return slice(hh * HEAD_DIM, (hh + 1) * HEAD_DIM)


def _moba_kernel(slope_ref, q_ref, k_ref, v_ref, g_ref, y_ref, kn_s, vb_s, kmean_s, strip_s):
    hp = pl.program_id(1)
    qi = pl.program_id(2)
    t = k_ref.shape[1]
    blk = MOBA_BLOCK
    nb = t // blk

    @pl.when(qi == 0)
    def _():
        for hh in range(HEADS_PER_STEP):
            kn = _rms(k_ref[0, :, _head_cols(hh)], g_ref[1:2, :])
            kn_s[hh] = kn.astype(kn_s.dtype)
            vb_s[hh] = v_ref[0, :, _head_cols(hh)].astype(vb_s.dtype)
            kmean_s[hh] = jnp.zeros(kmean_s.shape[1:], F32)
            for j in range(nb):
                kmean_s[hh, j:j + 1, :] = jnp.mean(kn[j * blk:(j + 1) * blk, :], axis=0, keepdims=True)

    rowb = lax.broadcasted_iota(jnp.int32, (nb, ATT_TQ), 0)
    heads = []
    for hh in range(HEADS_PER_STEP):
        qn = _rms(q_ref[0, :, _head_cols(hh)], g_ref[0:1, :])
        gate = jnp.where(rowb < qi, _dot_nt(kmean_s[hh], qn)[0:nb, :], NEG)
        sel = (_rank_rows(gate) < MOBA_TOPK) & (rowb < qi)
        unsel = jnp.where(sel | (rowb == qi), 0.0, 1.0)
        heads.append(dict(qb=qn.astype(MXU_DTYPE), k_s=kn_s.at[hh], v_s=vb_s.at[hh],
                          slope=slope_ref[hp * HEADS_PER_STEP + hh], blk_shift=MOBA_SHIFT,
                          unsel=_pad_rows(unsel, HEAD_DIM).astype(BF16)))
    outs = _block_masked_attention(heads, qi, strip_s)
    for hh in range(HEADS_PER_STEP):
        y_ref[0, :, _head_cols(hh)] = outs[hh].astype(y_ref.dtype)


def _moba(z3, qk_g, slopes):
    bsz, seq, _ = z3.shape
    assert MOBA_BLOCK == ATT_TQ and seq % STRIP_W == 0
    hps = HEADS_PER_STEP
    wide = hps * HEAD_DIM
    per_group = HEADS // hps
    kv = lambda group: pl.BlockSpec((1, seq, wide), lambda b, h, i: (b, 0, group * per_group + h))
    return pl.pallas_call(
        _moba_kernel,
        grid=(bsz, per_group, seq // ATT_TQ),
        in_specs=[pl.BlockSpec(memory_space=pltpu.SMEM),
                  pl.BlockSpec((1, ATT_TQ, wide), lambda b, h, i: (b, i, G_BQ * per_group + h)),
                  kv(G_BK), kv(G_BV),
                  pl.BlockSpec((2, HEAD_DIM), lambda b, h, i: (0, 0))],
        out_specs=pl.BlockSpec((1, ATT_TQ, wide), lambda b, h, i: (b, i, h)),
        out_shape=jax.ShapeDtypeStruct((bsz, seq, GROUP_W), BF16),
        scratch_shapes=[pltpu.VMEM((hps, seq, HEAD_DIM), MXU_DTYPE), pltpu.VMEM((hps, seq, HEAD_DIM), MXU_DTYPE),
                        pltpu.VMEM((hps, HEAD_DIM, HEAD_DIM), F32),
                        pltpu.VMEM((hps, seq // STRIP_W, ATT_TQ, STRIP_W), F32)],
        compiler_params=_cparams(("arbitrary", "arbitrary", "arbitrary")),
        name="moba",
    )(slopes, z3, z3, z3, qk_g)


def _gelu_tanh(x):
    return 0.5 * x * (1.0 + jnp.tanh(0.7978845608028654 * (x + 0.044715 * (x * x * x))))


def _nsa_kernel(slope_ref, q_ref, kc0_ref, kc1_ref, vc0_ref, vc1_ref, ks_ref, vs_ref, kw_ref, vw_ref, zg_ref,
                qg_ref, kg_ref, pe_ref, w1_ref, w2_ref, y_ref,
                kcmp_s, vcmp_s, ksn_s, vsb_s, kwn_s, vwb_s, strip_s):
    kc_refs, vc_refs = (kc0_ref, kc1_ref), (vc0_ref, vc1_ref)
    hp = pl.program_id(1)
    qi = pl.program_id(2)
    t = ks_ref.shape[1]
    tq = ATT_TQ
    nsub = t // NSA_CMP_STRIDE
    n_cmp = nsub - 1
    n_sel = t // NSA_SEL_BLOCK
    scale = HEAD_DIM ** -0.5

    @pl.when(qi == 0)
    def _():
        for hh in range(HEADS_PER_STEP):
            cols = _head_cols(hh)
            for cv, (src, dst) in enumerate(((kc_refs[hh], kcmp_s), (vc_refs[hh], vcmp_s))):
                acc_a = jnp.zeros((nsub, HEAD_DIM), F32)
                acc_b = jnp.zeros((nsub, HEAD_DIM), F32)
                for r in range(NSA_CMP_STRIDE):
                    zr = src[0, pl.ds(r, nsub, stride=NSA_CMP_STRIDE), :]
                    acc_a = acc_a + _dot(zr + pe_ref[cv, r:r + 1, :], w1_ref[cv, r])
                    rb = NSA_CMP_STRIDE + r
                    acc_b = acc_b + _dot(zr + pe_ref[cv, rb:rb + 1, :], w1_ref[cv, rb])
                hid = _gelu_tanh(acc_a + pltpu.roll(acc_b, nsub - 1, axis=0))
                cmp = _dot(hid, w2_ref[cv])
                if cv == 0:
                    cmp = _rms(cmp, kg_ref[0:1, :])
                dst[hh] = cmp.astype(dst.dtype)
            ksn_s[hh] = _rms(ks_ref[0, :, cols], kg_ref[1:2, :]).astype(ksn_s.dtype)
            vsb_s[hh] = vs_ref[0, :, cols].astype(vsb_s.dtype)
            kwn_s[hh] = _rms(kw_ref[0, :, cols], kg_ref[2:3, :]).astype(kwn_s.dtype)
            vwb_s[hh] = vw_ref[0, :, cols].astype(vwb_s.dtype)

    t0 = qi * tq
    rowi = lax.broadcasted_iota(jnp.int32, (tq, HEAD_DIM), 0)
    lane = lax.broadcasted_iota(jnp.int32, (tq, HEAD_DIM), 1)
    dist_c = (t0 + rowi) - (lane * NSA_CMP_STRIDE + (NSA_CMP_LEN - 1))
    ok_c = (dist_c >= 0) & (lane < n_cmp)
    dist_cf = dist_c.astype(F32)

    ob = lax.broadcasted_iota(jnp.int32, (HEAD_DIM, nsub), 0)
    oc = lax.broadcasted_iota(jnp.int32, (HEAD_DIM, nsub), 1)
    overlap_t = ((oc * NSA_CMP_STRIDE <= ob * NSA_SEL_BLOCK + (NSA_SEL_BLOCK - 1))
                 & (oc * NSA_CMP_STRIDE + (NSA_CMP_LEN - 1) >= ob * NSA_SEL_BLOCK)
                 & (oc < n_cmp) & (ob < n_sel)).astype(F32)
    rowb = lax.broadcasted_iota(jnp.int32, (n_sel, tq), 0)
    cur = jnp.right_shift(t0 + lax.broadcasted_iota(jnp.int32, (n_sel, tq), 1), NSA_SEL_SHIFT)

    win_w = NSA_WINDOW + tq
    k0 = jnp.maximum(t0 - NSA_WINDOW, 0)
    wr = lax.broadcasted_iota(jnp.int32, (tq, win_w), 0)
    wc = lax.broadcasted_iota(jnp.int32, (tq, win_w), 1)
    dist_w = (t0 - k0) + wr - wc
    ok_w = (dist_w >= 0) & (dist_w < NSA_WINDOW)
    dist_wf = dist_w.astype(F32)
    win_rows = pl.ds(pl.multiple_of(k0, tq), win_w)

    heads, o_cmp, o_win = [], [], []
    for hh in range(HEADS_PER_STEP):
        slope = slope_ref[hp * HEADS_PER_STEP + hh]
        qn = _rms(q_ref[0, :, _head_cols(hh)], qg_ref[...])
        qb = qn.astype(MXU_DTYPE)

        s_c = jnp.where(ok_c, _dot_nt(qb, kcmp_s[hh]) * scale - slope * dist_cf, NEG)
        m_c = jnp.max(s_c, axis=1, keepdims=True)
        e_c = jnp.where(ok_c, jnp.exp(s_c - m_c), 0.0)
        p_c = e_c / jnp.maximum(jnp.sum(e_c, axis=1, keepdims=True), 1e-30)
        o_cmp.append(_dot(p_c, vcmp_s[hh]))

        imp = _dot_split_nt(overlap_t, p_c)[0:n_sel, :]
        imp = jnp.where(rowb == cur, BIG, jnp.where(rowb < cur, imp, NEG))
        sel = (_rank_rows(imp) < NSA_SEL_TOPN) & (rowb <= cur)
        unsel = jnp.where(sel, 0.0, 1.0)
        heads.append(dict(qb=qb, k_s=ksn_s.at[hh], v_s=vsb_s.at[hh], slope=slope, blk_shift=NSA_SEL_SHIFT,
                          unsel=_pad_rows(unsel, HEAD_DIM).astype(BF16)))

        s_w = _dot_nt(qb, kwn_s[hh, win_rows, :]) * scale + jnp.where(ok_w, (-slope) * dist_wf, NEG)
        p_w = jnp.exp(s_w - jnp.max(s_w, axis=1, keepdims=True))
        o_win.append(_dot(p_w, vwb_s[hh, win_rows, :]) / jnp.sum(p_w, axis=1, keepdims=True))

    o_slc = _block_masked_attention(heads, qi, strip_s)

    zg = zg_ref[0]
    for hh in range(HEADS_PER_STEP):
        h = hp * HEADS_PER_STEP + hh
        g_cmp = _sigmoid(_lane_col(zg, GL_NG + h))
        g_slc = _sigmoid(_lane_col(zg, GL_NG + HEADS + h))
        g_win = _sigmoid(_lane_col(zg, GL_NG + 2 * HEADS + h))
        y_ref[0, :, _head_cols(hh)] = (g_cmp * o_cmp[hh] + g_slc * o_slc[hh] + g_win * o_win[hh]).astype(y_ref.dtype)


def _nsa(z3, zg3, q_g, k_g, cmp_pos, cmp_w1, cmp_w2, slopes):
    bsz, seq, _ = z3.shape
    assert seq // NSA_CMP_STRIDE == HEAD_DIM, "compressed blocks are laid out on the 128 lanes"
    assert seq % STRIP_W == 0 and seq >= NSA_WINDOW + ATT_TQ
    tq = ATT_TQ
    hps = HEADS_PER_STEP
    wide = hps * HEAD_DIM
    per_group = HEADS // hps
    kv = lambda group: pl.BlockSpec((1, seq, wide), lambda b, h, i: (b, 0, group * per_group + h))
    one = lambda group, hh: pl.BlockSpec((1, seq, HEAD_DIM), lambda b, h, i: (b, 0, group * HEADS + h * hps + hh))
    full = lambda shape: pl.BlockSpec(shape, lambda b, h, i: (0,) * len(shape))
    w1 = cmp_w1.reshape(2, NSA_CMP_LEN, HEAD_DIM, HEAD_DIM).astype(MXU_DTYPE)
    w2 = cmp_w2.astype(MXU_DTYPE)
    seq_buf = lambda: pltpu.VMEM((hps, seq, HEAD_DIM), MXU_DTYPE)
    cmp_buf = lambda: pltpu.VMEM((hps, HEAD_DIM, HEAD_DIM), MXU_DTYPE)
    return pl.pallas_call(
        _nsa_kernel,
        grid=(bsz, per_group, seq // tq),
        in_specs=[pl.BlockSpec(memory_space=pltpu.SMEM),
                  pl.BlockSpec((1, tq, wide), lambda b, h, i: (b, i, G_NQ * per_group + h)),
                  one(G_NKC, 0), one(G_NKC, 1), one(G_NVC, 0), one(G_NVC, 1),
                  kv(G_NKS), kv(G_NVS), kv(G_NKW), kv(G_NVW),
                  pl.BlockSpec((1, tq, GATE_W), lambda b, h, i: (b, i, 0)),
                  full((1, HEAD_DIM)), full((3, HEAD_DIM)), full((2, NSA_CMP_LEN, HEAD_DIM)),
                  full((2, NSA_CMP_LEN, HEAD_DIM, HEAD_DIM)), full((2, HEAD_DIM, HEAD_DIM))],
        out_specs=pl.BlockSpec((1, tq, wide), lambda b, h, i: (b, i, h)),
        out_shape=jax.ShapeDtypeStruct((bsz, seq, GROUP_W), BF16),
        scratch_shapes=[cmp_buf(), cmp_buf(), seq_buf(), seq_buf(), seq_buf(), seq_buf(),
                        pltpu.VMEM((hps, seq // STRIP_W, tq, STRIP_W), F32)],
        compiler_params=_cparams(("arbitrary", "arbitrary", "arbitrary")),
        name="nsa",
    )(slopes, z3, z3, z3, z3, z3, z3, z3, z3, z3, zg3, q_g.reshape(1, HEAD_DIM), k_g, cmp_pos, w1, w2)


def _out_proj_kernel(ym_ref, yb_ref, yr_ref, yn_ref, w_ref, x_ref, g_ref, o_ref):
    acc = jnp.dot(ym_ref[...], w_ref[0, 0:GROUP_W, :], preferred_element_type=F32)
    acc = acc + jnp.dot(yb_ref[...], w_ref[0, GROUP_W:2 * GROUP_W, :], preferred_element_type=F32)
    acc = acc + jnp.dot(yr_ref[...], w_ref[0, 2 * GROUP_W:3 * GROUP_W, :], preferred_element_type=F32)
    acc = acc + jnp.dot(yn_ref[...], w_ref[0, 3 * GROUP_W:4 * GROUP_W, :], preferred_element_type=F32)
    o_ref[...] = x_ref[...] + g_ref[0] * acc


def _out_proj(ys, w_out, layer, x2d, mod3, seq):
    n, d = x2d.shape
    tm, tn = 1024, 1024
    per_b = seq // tm
    y_spec = pl.BlockSpec((tm, GROUP_W), lambda i, j: (i, 0))
    return pl.pallas_call(
        _out_proj_kernel,
        grid=(n // tm, d // tn),
        in_specs=[y_spec, y_spec, y_spec, y_spec,
                  pl.BlockSpec((1, 4 * GROUP_W, tn), lambda i, j: (layer, 0, j)),
                  pl.BlockSpec((tm, tn), lambda i, j: (i, j)),
                  pl.BlockSpec((1, 1, tn), lambda i, j: (i // per_b, 0, 2 * (d // tn) + j))],
        out_specs=pl.BlockSpec((tm, tn), lambda i, j: (i, j)),
        out_shape=jax.ShapeDtypeStruct((n, d), F32),
        compiler_params=_cparams(("arbitrary", "arbitrary")),
        name="out_proj",
    )(*[y.reshape(n, GROUP_W) for y in ys], w_out, x2d, mod3)


SLAB_E0, SLAB_E1, SLAB_R0, SLAB_R1, SLAB_G0, SLAB_G1 = 0, 1, 2, 3, 4, 5


def _route_kernel(x_ref, g_ref, sc_ref, sh_ref, wr_ref, br_ref, h_ref, slab_ref, cnt_ref, carry_s):
    tm = x_ref.shape[0]

    @pl.when(pl.program_id(0) == 0)
    def _():
        carry_s[...] = jnp.zeros(carry_s.shape, F32)

    hmod = _rms(x_ref[...], g_ref[...]) * (1.0 + sc_ref[0]) + sh_ref[0]
    h_ref[...] = hmod
    logits = _dot(hmod, wr_ref[...]) + br_ref[...]
    lane = lax.broadcasted_iota(jnp.int32, logits.shape, 1).astype(F32)
    far = 4.0 * GATE_W

    in_g = lane < N_GROUPS
    lg = jnp.where(in_g, logits, NEG)
    g_max = jnp.max(lg, axis=1, keepdims=True)
    grp = jnp.min(jnp.where(in_g & (lg == g_max), lane, far), axis=1, keepdims=True)
    p_grp = 1.0 / jnp.sum(jnp.where(in_g, jnp.exp(lg - g_max), 0.0), axis=1, keepdims=True)

    lo = N_GROUPS + grp * EXPERTS_PER_GROUP
    in_e = (lane >= lo) & (lane < lo + EXPERTS_PER_GROUP)
    le = jnp.where(in_e, logits, NEG)
    e_max = jnp.max(le, axis=1, keepdims=True)
    ee = jnp.where(in_e, jnp.exp(le - e_max), 0.0)
    pe = jnp.where(in_e, ee / jnp.sum(ee, axis=1, keepdims=True), -1.0)
    p1 = jnp.max(pe, axis=1, keepdims=True)
    i1 = jnp.min(jnp.where(pe == p1, lane, far), axis=1, keepdims=True)
    pe2 = jnp.where(lane == i1, -1.0, pe)
    p2 = jnp.max(pe2, axis=1, keepdims=True)
    i2 = jnp.min(jnp.where(pe2 == p2, lane, far), axis=1, keepdims=True)
    e0 = i1 - N_GROUPS
    e1 = i2 - N_GROUPS
    g0 = p_grp * (p1 / (p1 + p2))
    g1 = p_grp * (p2 / (p1 + p2))

    onehot = ((lane == e0) | (lane == e1)).astype(BF16)
    rr = lax.broadcasted_iota(jnp.int32, (tm, tm), 0)
    cc = lax.broadcasted_iota(jnp.int32, (tm, tm), 1)
    before = jnp.dot((cc < rr).astype(BF16), onehot, preferred_element_type=F32) + carry_s[...]
    r0 = jnp.sum(jnp.where(lane == e0, before, 0.0), axis=1, keepdims=True)
    r1 = jnp.sum(jnp.where(lane == e1, before, 0.0), axis=1, keepdims=True)
    carry_s[...] = carry_s[...] + jnp.sum(onehot.astype(F32), axis=0, keepdims=True)
    cnt_ref[...] = carry_s[...]

    slab = jnp.where(lane == SLAB_E0, e0.astype(F32), 0.0)
    slab = jnp.where(lane == SLAB_E1, e1.astype(F32), slab)
    slab = jnp.where(lane == SLAB_R0, r0, slab)
    slab = jnp.where(lane == SLAB_R1, r1, slab)
    slab = jnp.where(lane == SLAB_G0, g0, slab)
    slab = jnp.where(lane == SLAB_G1, g1, slab)
    slab_ref[...] = slab


def _route(x2d, norm_g, mod3, w_router, b_router, seq):
    n, d = x2d.shape
    tm = 512
    per_b = seq // tm
    return pl.pallas_call(
        _route_kernel,
        grid=(n // tm,),
        in_specs=[pl.BlockSpec((tm, d), lambda i: (i, 0)),
                  pl.BlockSpec((1, d), lambda i: (0, 0)),
                  pl.BlockSpec((1, 1, d), lambda i: (i // per_b, 0, 4)),
                  pl.BlockSpec((1, 1, d), lambda i: (i // per_b, 0, 3)),
                  pl.BlockSpec((d, GATE_W), lambda i: (0, 0)),
                  pl.BlockSpec((1, GATE_W), lambda i: (0, 0))],
        out_specs=[pl.BlockSpec((tm, d), lambda i: (i, 0)),
                   pl.BlockSpec((tm, GATE_W), lambda i: (i, 0)),
                   pl.BlockSpec((1, GATE_W), lambda i: (0, 0))],
        out_shape=[jax.ShapeDtypeStruct((n, d), F32),
                   jax.ShapeDtypeStruct((n, GATE_W), F32),
                   jax.ShapeDtypeStruct((1, GATE_W), F32)],
        scratch_shapes=[pltpu.VMEM((1, GATE_W), F32)],
        compiler_params=_cparams(("arbitrary",)),
        name="route",
    )(x2d, norm_g.reshape(1, d), mod3, mod3, w_router, b_router)


def _invert_kernel(dest_ref, inv_ref):
    def clear(i, _):
        inv_ref[i] = 0
        return 0

    def put(a, _):
        inv_ref[dest_ref[a]] = jnp.right_shift(a, 1)
        return 0

    lax.fori_loop(0, inv_ref.shape[0], clear, 0, unroll=8)
    lax.fori_loop(0, dest_ref.shape[0], put, 0, unroll=8)


def _invert(dest_flat, cap):
    smem = pl.BlockSpec(memory_space=pltpu.SMEM)
    return pl.pallas_call(
        _invert_kernel,
        in_specs=[smem],
        out_specs=smem,
        out_shape=jax.ShapeDtypeStruct((cap,), jnp.int32),
        name="invert",
    )(dest_flat)


def _start_row_gather(idx_ref, src_hbm, dst, sem, n_rows, first=0):
    for r in range(first, n_rows):
        pltpu.make_async_copy(src_hbm.at[pl.ds(idx_ref[r], 1)], dst.at[pl.ds(r, 1)], sem).start()


def _wait_row_gather(src_hbm, dst, sem, n_rows):
    pltpu.make_async_copy(src_hbm.at[pl.ds(0, n_rows)], dst, sem).wait()


def _expert_kernel(blk_e_ref, n_used_ref, inv_ref, h_hbm, w1_ref, w3_ref, w2_ref, y_ref,
                   x_s, w1_s, w3_s, w2_s, sem):
    s = pl.program_id(0)
    rows = EXPERT_ROWS
    n_used = n_used_ref[0]
    slot = lax.rem(s, 2)

    blk = s - 1
    prev = jnp.maximum(blk - 1, 0)
    gather = s < n_used
    compute = (s >= 1) & (blk < n_used)

    def start_gather(first, stop):
        _start_row_gather(inv_ref, h_hbm, x_s.at[slot], sem.at[slot], stop, first)

    def expert_block(gather_too):
        @pl.when((blk == 0) | (blk_e_ref[blk] != blk_e_ref[prev]))
        def _():
            w1_s[...] = w1_ref[0, 0].astype(w1_s.dtype)
            w3_s[...] = w3_ref[0, 0].astype(w3_s.dtype)
            w2_s[...] = w2_ref[0, 0].astype(w2_s.dtype)

        cuts = (0, rows // 4, rows // 2, rows) if gather_too else (0, 0, 0, 0)
        _wait_row_gather(h_hbm, x_s.at[1 - slot], sem.at[1 - slot], rows)
        start_gather(cuts[0], cuts[1])
        x = x_s[1 - slot].astype(MXU_DTYPE)
        a = jnp.dot(x, w1_s[...], preferred_element_type=F32)
        start_gather(cuts[1], cuts[2])
        b = jnp.dot(x, w3_s[...], preferred_element_type=F32)
        start_gather(cuts[2], cuts[3])
        y_ref[...] = _dot(a * _sigmoid(a) * b, w2_s[...])

    @pl.when(gather & compute)
    def _():
        expert_block(True)

    @pl.when(gather & jnp.logical_not(compute))
    def _():
        start_gather(0, rows)

    @pl.when(compute & jnp.logical_not(gather))
    def _():
        expert_block(False)

    @pl.when((s >= 1) & (blk >= n_used))
    def _():
        y_ref[...] = jnp.zeros(y_ref.shape, y_ref.dtype)


def _experts(h2, inv, blk_e, n_used, w1, w3, w2, layer):
    d = D_MODEL
    rows = EXPERT_ROWS
    n_blocks = inv.shape[0] // rows
    done = lambda s: jnp.maximum(s - 1, 0)
    w_map = lambda s, be, nu: (layer, be[done(s)], 0, 0)
    return pl.pallas_call(
        _expert_kernel,
        grid_spec=pltpu.PrefetchScalarGridSpec(
            num_scalar_prefetch=2,
            grid=(n_blocks + 1,),
            in_specs=[pl.BlockSpec((rows,), lambda s, be, nu: (jnp.minimum(s, n_blocks - 1),),
                                   memory_space=pltpu.SMEM),
                      pl.BlockSpec(memory_space=pl.ANY),
                      pl.BlockSpec((1, 1, d, D_EXPERT), w_map),
                      pl.BlockSpec((1, 1, d, D_EXPERT), w_map),
                      pl.BlockSpec((1, 1, D_EXPERT, d), w_map)],
            out_specs=pl.BlockSpec((rows, d), lambda s, be, nu: (done(s), 0)),
            scratch_shapes=[pltpu.VMEM((2, rows, d), F32),
                            pltpu.VMEM((d, D_EXPERT), MXU_DTYPE), pltpu.VMEM((d, D_EXPERT), MXU_DTYPE),
                            pltpu.VMEM((D_EXPERT, d), MXU_DTYPE),
                            pltpu.SemaphoreType.DMA((2,))]),
        out_shape=jax.ShapeDtypeStruct((n_blocks * rows, d), F32),
        compiler_params=_cparams(("arbitrary",)),
        name="experts",
    )(blk_e, n_used, inv, h2, w1, w3, w2)


def _combine_kernel(d0_ref, d1_ref, x_ref, g_ref, slab_ref, yb_hbm, o_ref, rows_s, sem):
    s = pl.program_id(0)
    n_tiles = pl.num_programs(0) - 1
    tm = x_ref.shape[0]
    slot = lax.rem(s, 2)

    @pl.when(s < n_tiles)
    def _():
        _start_row_gather(d0_ref, yb_hbm, rows_s.at[slot, 0], sem.at[slot], tm)
        _start_row_gather(d1_ref, yb_hbm, rows_s.at[slot, 1], sem.at[slot], tm)

    @pl.when(s >= 1)
    def _():
        _wait_row_gather(yb_hbm, rows_s.at[1 - slot, 0], sem.at[1 - slot], tm)
        _wait_row_gather(yb_hbm, rows_s.at[1 - slot, 1], sem.at[1 - slot], tm)
        route = slab_ref[...]
        g0 = route[:, SLAB_G0:SLAB_G0 + 1]
        g1 = route[:, SLAB_G1:SLAB_G1 + 1]
        o_ref[...] = x_ref[...] + g_ref[0] * (g0 * rows_s[1 - slot, 0] + g1 * rows_s[1 - slot, 1])


def _combine(x2d, mod3, slab, dest0, dest1, yb, seq):
    n, d = x2d.shape
    tm = 256
    per_b = seq // tm
    n_tiles = n // tm
    done = lambda s: jnp.maximum(s - 1, 0)
    idx_spec = pl.BlockSpec((tm,), lambda s: (jnp.minimum(s, n_tiles - 1),), memory_space=pltpu.SMEM)
    return pl.pallas_call(
        _combine_kernel,
        grid=(n_tiles + 1,),
        in_specs=[idx_spec, idx_spec,
                  pl.BlockSpec((tm, d), lambda s: (done(s), 0)),
                  pl.BlockSpec((1, 1, d), lambda s: (done(s) // per_b, 0, 5)),
                  pl.BlockSpec((tm, GATE_W), lambda s: (done(s), 0)),
                  pl.BlockSpec(memory_space=pl.ANY)],
        out_specs=pl.BlockSpec((tm, d), lambda s: (done(s), 0)),
        out_shape=jax.ShapeDtypeStruct((n, d), F32),
        scratch_shapes=[pltpu.VMEM((2, 2, tm, d), F32), pltpu.SemaphoreType.DMA((2,))],
        compiler_params=_cparams(("arbitrary",)),
        name="combine",
    )(dest0, dest1, x2d, mod3, slab, yb)


_OFF_MI = 4 * GROUP_W
_OFF_BQ = _OFF_MI + 2 * HEADS
_OFF_NG = _OFF_BQ + 14 * GROUP_W


def _pack_w_kernel(w_ref, o_ref, og_ref):
    gap = _OFF_BQ - _OFF_MI
    o_ref[0, :, 0:_OFF_MI] = w_ref[0, :, 0:_OFF_MI].astype(o_ref.dtype)
    tail = w_ref[0, :, _OFF_MI:]
    width = tail.shape[1]
    o_ref[0, :, _OFF_MI:] = pltpu.roll(tail, width - gap, axis=1)[:, 0:D_WIDE - _OFF_MI].astype(o_ref.dtype)
    lane = lax.broadcasted_iota(jnp.int32, (w_ref.shape[1], GATE_W), 1)
    first = w_ref[0, :, _OFF_MI:_OFF_MI + GATE_W]
    ragged = w_ref[0, :, _OFF_NG - GL_NG:_OFF_NG - GL_NG + GATE_W]
    gate = jnp.where(lane < GL_NG, first, jnp.where(lane < GL_NG + 3 * HEADS, ragged, 0.0))
    og_ref[0] = gate.astype(og_ref.dtype)


def _pack_w_in(w_in):
    depth, d, d_in = w_in.shape
    tk = 128
    lanes_in = -(-d_in // GATE_W) * GATE_W
    assert (_OFF_NG - GL_NG) % GATE_W == 0 and _OFF_NG - GL_NG + GATE_W == lanes_in
    return pl.pallas_call(
        _pack_w_kernel,
        grid=(depth, d // tk),
        in_specs=[pl.BlockSpec((1, tk, lanes_in), lambda l, i: (l, i, 0))],
        out_specs=[pl.BlockSpec((1, tk, D_WIDE), lambda l, i: (l, i, 0)),
                   pl.BlockSpec((1, tk, GATE_W), lambda l, i: (l, i, 0))],
        out_shape=[jax.ShapeDtypeStruct((depth, d, D_WIDE), MXU_DTYPE),
                   jax.ShapeDtypeStruct((depth, d, GATE_W), MXU_DTYPE)],
        compiler_params=_cparams(("arbitrary", "arbitrary")),
        name="pack_w_in",
    )(w_in)


def _moe(x2d, norm_g, mod3, wg, bg, we, be, w1, w3, w2, layer, seq):
    n, d = x2d.shape
    n_route = N_GROUPS + N_EXPERTS
    w_router = jnp.concatenate([wg, we, jnp.zeros((d, GATE_W - n_route), wg.dtype)], axis=1).astype(MXU_DTYPE)
    b_router = jnp.concatenate([bg, be, jnp.zeros((GATE_W - n_route,), bg.dtype)]).reshape(1, GATE_W)
    h2, slab, cnt = _route(x2d, norm_g, mod3, w_router, b_router, seq)

    rows = EXPERT_ROWS
    counts = cnt[0, :N_EXPERTS].astype(jnp.int32)
    pcounts = (counts + rows - 1) // rows * rows
    pends = jnp.cumsum(pcounts)
    pstarts = pends - pcounts
    eid = slab[:, SLAB_E0:SLAB_E1 + 1].astype(jnp.int32)
    rank = slab[:, SLAB_R0:SLAB_R1 + 1].astype(jnp.int32)
    dest = pstarts[eid] + rank
    n_blocks = -(-2 * n // rows) + N_EXPERTS
    blk_row0 = jnp.arange(n_blocks, dtype=jnp.int32) * rows
    blk_e = jnp.minimum(jnp.sum((pends[None, :] <= blk_row0[:, None]).astype(jnp.int32), axis=1), N_EXPERTS - 1)
    n_used = (pends[-1:] // rows).astype(jnp.int32)

    inv = _invert(dest.reshape(-1), n_blocks * rows)
    yb = _experts(h2, inv, blk_e, n_used, w1, w3, w2, layer)
    return _combine(x2d, mod3, slab, dest[:, 0], dest[:, 1], yb, seq)


def _layer(x2d, mod, bsz, seq, layer, norm1_g, norm2_g, w_wide, w_gate, mlstm_gate_b, mlstm_conv_w, mlstm_out_g,
           moba_qk_g, ret_out_g, nsa_q_g, nsa_k_g, nsa_cmp_pos, nsa_cmp_w1, nsa_cmp_w2, w_out, router_g_w,
           router_g_b, router_e_w, router_e_b, exp_w1, exp_w3, exp_w2, slopes, log_gamma):
    n, d = x2d.shape
    mod3 = mod.reshape(bsz, 1, 6 * d)
    z, zg = _norm_in_proj(x2d, norm1_g, mod3, w_wide, layer, w_gate, seq)
    z3 = z.reshape(bsz, seq, D_WIDE)
    zg3 = zg.reshape(bsz, seq, GATE_W)
    y_m = _mlstm(z3, zg3, mlstm_gate_b, mlstm_conv_w, mlstm_out_g)
    y_b = _moba(z3, moba_qk_g, slopes[0::2])
    y_r = _retention(z3, log_gamma, ret_out_g)
    y_n = _nsa(z3, zg3, nsa_q_g, nsa_k_g, nsa_cmp_pos, nsa_cmp_w1, nsa_cmp_w2, slopes[1::2])
    x2d = _out_proj((y_m, y_b, y_r, y_n), w_out, layer, x2d, mod3, seq)
    return _moe(x2d, norm2_g, mod3, router_g_w, router_g_b, router_e_w, router_e_b, exp_w1, exp_w3, exp_w2,
                layer, seq)


def kernel(x, c, norm1_g, norm2_g, ada_w, ada_b, w_in, mlstm_gate_b, mlstm_conv_w, mlstm_out_g, moba_qk_g,
           ret_out_g, nsa_q_g, nsa_k_g, nsa_cmp_pos, nsa_cmp_w1, nsa_cmp_w2, w_out, router_g_w, router_g_b,
           router_e_w, router_e_b, exp_w1, exp_w3, exp_w2):
    bsz, seq, d = x.shape
    depth = ada_w.shape[0]
    n_softmax_heads = 2 * HEADS
    slopes = jnp.exp2(-8.0 * jnp.arange(1, n_softmax_heads + 1, dtype=F32) / n_softmax_heads)
    log_gamma = jnp.log(1.0 - jnp.exp2(-5.0 - jnp.arange(HEADS, dtype=F32)))
    mod = _ada_mod(c, ada_w, ada_b)
    w_wide, w_gate = _pack_w_in(w_in)
    w_out_b = w_out.astype(MXU_DTYPE)
    x2d = x.reshape(bsz * seq, d)
    for l in range(depth):
        x2d = _layer(x2d, mod[l], bsz, seq, l, norm1_g[l], norm2_g[l], w_wide, w_gate, mlstm_gate_b[l],
                     mlstm_conv_w[l], mlstm_out_g[l], moba_qk_g[l], ret_out_g[l], nsa_q_g[l], nsa_k_g[l],
                     nsa_cmp_pos[l], nsa_cmp_w1[l], nsa_cmp_w2[l], w_out_b, router_g_w[l], router_g_b[l],
                     router_e_w[l], router_e_b[l], exp_w1, exp_w3, exp_w2, slopes, log_gamma)
    return x2d.reshape(bsz, seq, d)
```

```python
import functools

import jax
import jax.numpy as jnp
from jax import lax
from jax.experimental import pallas as pl
from jax.experimental.pallas import tpu as pltpu

F32 = jnp.float32
BF16 = jnp.bfloat16
MXU_DTYPE = jnp.bfloat16

D_MODEL = 2048
HEAD_DIM = 128
HEADS = 4
GROUP_W = HEADS * HEAD_DIM
N_WIDE_GROUPS = 18
D_WIDE = N_WIDE_GROUPS * GROUP_W
GATE_W = 128

MLSTM_CHUNK = 64
MLSTM_CONV = 4
MOBA_BLOCK = 256
MOBA_SHIFT = 8
MOBA_TOPK = 3
RET_CHUNK = 128
NSA_CMP_LEN = 32
NSA_CMP_STRIDE = 16
NSA_SEL_BLOCK = 64
NSA_SEL_SHIFT = 6
NSA_SEL_TOPN = 4
NSA_WINDOW = 512
N_GROUPS = 4
EXPERTS_PER_GROUP = 8
N_EXPERTS = N_GROUPS * EXPERTS_PER_GROUP
D_EXPERT = 512
EXPERT_ROWS = 256
ROW_GATHER_DMA_QUEUE = 1

NORM_EPS = 1e-6
NEG = -1e30
BIG = 1e9

G_MQ, G_MK, G_MV, G_MO = 0, 1, 2, 3
G_BQ, G_BK, G_BV = 4, 5, 6
G_RQ, G_RK, G_RV, G_RG = 7, 8, 9, 10
G_NQ, G_NKC, G_NVC, G_NKS, G_NVS, G_NKW, G_NVW = 11, 12, 13, 14, 15, 16, 17
GL_MI, GL_MF, GL_NG = 0, 4, 8

VMEM_LIMIT = 56 * 1024 * 1024


def _cparams(sem):
    return pltpu.CompilerParams(dimension_semantics=sem, vmem_limit_bytes=VMEM_LIMIT)


def _dot(a, b):
    return jnp.dot(a.astype(MXU_DTYPE), b.astype(MXU_DTYPE), preferred_element_type=F32)


def _dot_nt(a, b):
    return lax.dot_general(a.astype(MXU_DTYPE), b.astype(MXU_DTYPE), (((1,), (1,)), ((), ())),
                           preferred_element_type=F32)


def _dot_tn(a, b):
    return lax.dot_general(a.astype(MXU_DTYPE), b.astype(MXU_DTYPE), (((0,), (0,)), ((), ())),
                           preferred_element_type=F32)


def _dot_split_nt(b01, a):
    a_hi = a.astype(MXU_DTYPE)
    r1 = a - a_hi.astype(F32)
    a_mid = r1.astype(MXU_DTYPE)
    a_lo = (r1 - a_mid.astype(F32)).astype(MXU_DTYPE)
    return _dot_nt(b01, a_hi) + _dot_nt(b01, a_mid) + _dot_nt(b01, a_lo)


def _rms(x, g):
    return x * lax.rsqrt(jnp.mean(x * x, axis=-1, keepdims=True) + NORM_EPS) * g


def _sigmoid(x):
    return jax.nn.sigmoid(x)


def _lane_col(x, idx):
    lane = lax.broadcasted_iota(jnp.int32, x.shape, 1)
    return jnp.sum(jnp.where(lane == idx, x, 0.0), axis=1, keepdims=True)


def _ada_kernel(c_ref, w_ref, b_ref, o_ref):
    c = c_ref[...]
    o_ref[0] = _dot(c * _sigmoid(c), w_ref[0]) + b_ref[0]


def _ada_mod(c, ada_w, ada_b):
    depth, d, n6 = ada_w.shape
    b = c.shape[0]
    tn = 1024
    return pl.pallas_call(
        _ada_kernel,
        grid=(depth, n6 // tn),
        in_specs=[pl.BlockSpec((b, d), lambda l, j: (0, 0)),
                  pl.BlockSpec((1, d, tn), lambda l, j: (l, 0, j)),
                  pl.BlockSpec((1, 1, tn), lambda l, j: (l, 0, j))],
        out_specs=pl.BlockSpec((1, b, tn), lambda l, j: (l, 0, j)),
        out_shape=jax.ShapeDtypeStruct((depth, b, n6), F32),
        compiler_params=_cparams(("arbitrary", "arbitrary")),
        name="ada_mod",
    )(c, ada_w, ada_b.reshape(depth, 1, n6))


def _norm_in_kernel(x_ref, g_ref, sc_ref, sh_ref, w_ref, ws_ref, z_ref, zg_ref, h_s):
    @pl.when(pl.program_id(1) == 0)
    def _():
        h = _rms(x_ref[...], g_ref[...]) * (1.0 + sc_ref[0]) + sh_ref[0]
        hb = h.astype(MXU_DTYPE)
        h_s[...] = hb
        zg_ref[...] = jnp.dot(hb, ws_ref[0], preferred_element_type=F32)

    z_ref[...] = jnp.dot(h_s[...], w_ref[0], preferred_element_type=F32)


def _norm_in_proj(x2d, norm_g, mod3, w_wide, layer, w_gate, seq):
    n, d = x2d.shape
    tm, tn = 1024, 1024
    per_b = seq // tm
    return pl.pallas_call(
        _norm_in_kernel,
        grid=(n // tm, D_WIDE // tn),
        in_specs=[pl.BlockSpec((tm, d), lambda i, j: (i, 0)),
                  pl.BlockSpec((1, d), lambda i, j: (0, 0)),
                  pl.BlockSpec((1, 1, d), lambda i, j: (i // per_b, 0, 1)),
                  pl.BlockSpec((1, 1, d), lambda i, j: (i // per_b, 0, 0)),
                  pl.BlockSpec((1, d, tn), lambda i, j: (layer, 0, j)),
                  pl.BlockSpec((1, d, GATE_W), lambda i, j: (layer, 0, 0))],
        out_specs=[pl.BlockSpec((tm, tn), lambda i, j: (i, j)),
                   pl.BlockSpec((tm, GATE_W), lambda i, j: (i, 0))],
        out_shape=[jax.ShapeDtypeStruct((n, D_WIDE), F32),
                   jax.ShapeDtypeStruct((n, GATE_W), F32)],
        scratch_shapes=[pltpu.VMEM((tm, d), MXU_DTYPE)],
        compiler_params=_cparams(("arbitrary", "arbitrary")),
        name="norm_in_proj",
    )(x2d, norm_g.reshape(1, d), mod3, mod3, w_wide, w_gate)


REC_HEADS_PER_STEP = 2


def _mlstm_kernel(gb_ref, q_ref, k_ref, v_ref, o_ref, zg_ref, cwq_ref, cwk_ref, og_ref, y_ref,
                  pad_s, qs_s, ks_s, ic_s, fc_s):
    hp = pl.program_id(1)
    t = q_ref.shape[1]
    cl = MLSTM_CHUNK
    nc = t // cl
    hps = REC_HEADS_PER_STEP

    def conv_silu(src_ref, cw_ref, cols, dst_s, scale):
        pad_s[8:8 + t, :] = src_ref[0, :, cols]
        off = 8 - (MLSTM_CONV - 1)
        tile = 128
        for r0 in range(0, t, tile):
            acc = cw_ref[0:1, cols] * pad_s[r0 + off:r0 + off + tile, :]
            for j in range(1, MLSTM_CONV):
                acc = acc + cw_ref[j:j + 1, cols] * pad_s[r0 + off + j:r0 + off + j + tile, :]
            dst_s[r0:r0 + tile, :] = acc * _sigmoid(acc) * scale

    pad_s[0:8, :] = jnp.zeros((8, HEAD_DIM), F32)
    zg = zg_ref[0]
    for hh in range(hps):
        h = hp * hps + hh
        conv_silu(q_ref, cwq_ref, _head_cols(hh), qs_s.at[hh], HEAD_DIM ** -0.5)
        conv_silu(k_ref, cwk_ref, _head_cols(hh), ks_s.at[hh], 1.0)
        ic_s[hh] = _lane_col(zg, GL_MI + h) + gb_ref[0, h]
        f_pre = _lane_col(zg, GL_MF + h) + gb_ref[1, h]
        fc_s[hh] = jnp.minimum(f_pre, 0.0) - jnp.log1p(jnp.exp(-jnp.abs(f_pre)))

    rr = lax.broadcasted_iota(jnp.int32, (cl, cl), 0)
    cc = lax.broadcasted_iota(jnp.int32, (cl, cl), 1)
    eye = rr == cc
    causal = cc <= rr
    og = og_ref[...]

    def head_step(hh, c, carry):
        c_st, n_st, m_st = carry
        sl = pl.ds(pl.multiple_of(c * cl, cl), cl)
        cols = _head_cols(hh)
        qc = qs_s[hh, sl, :]
        kc = ks_s[hh, sl, :]
        vc = v_ref[0, sl, cols]
        i_col = ic_s[hh, sl, :]
        f_col = fc_s[hh, sl, :]
        f_row = jnp.sum(jnp.where(eye, f_col, 0.0), axis=0, keepdims=True)
        i_row = jnp.sum(jnp.where(eye, i_col, 0.0), axis=0, keepdims=True)
        a_col = jnp.sum(jnp.where(causal, f_row, 0.0), axis=1, keepdims=True)
        a_row = jnp.sum(jnp.where(rr <= cc, f_col, 0.0), axis=0, keepdims=True)
        log_d = jnp.where(causal, a_col - a_row + i_row, NEG)
        m_inter = a_col + m_st
        m_row = jnp.maximum(m_inter, jnp.max(log_d, axis=1, keepdims=True))
        s = _dot_nt(qc, kc) * jnp.exp(log_d - m_row)
        w_inter = jnp.exp(m_inter - m_row)
        num = _dot(s, vc) + w_inter * _dot(qc, c_st)
        den = jnp.sum(s, axis=1, keepdims=True) + w_inter * jnp.sum(qc * n_st, axis=1, keepdims=True)
        h_out = num / jnp.maximum(jnp.abs(den), jnp.exp(-m_row))
        a_last = jnp.sum(f_col, axis=0, keepdims=True)
        w_log = a_last - a_col + i_col
        m_new = jnp.maximum(a_last + m_st, jnp.max(w_log, axis=0, keepdims=True))
        w = jnp.exp(w_log - m_new)
        decay = jnp.exp(a_last + m_st - m_new)
        kw = kc * w
        c_new = decay * c_st + _dot_tn(kw, vc)
        n_new = decay * n_st + jnp.sum(kw, axis=0, keepdims=True)
        y = _rms(h_out, og) * _sigmoid(o_ref[0, sl, cols])
        y_ref[0, sl, cols] = y.astype(y_ref.dtype)
        return c_new, n_new, m_new

    def body(c, carry):
        return tuple(head_step(hh, c, carry[hh]) for hh in range(hps))

    carry0 = (jnp.zeros((HEAD_DIM, HEAD_DIM), F32), jnp.zeros((1, HEAD_DIM), F32), jnp.zeros((1, 1), F32))
    lax.fori_loop(0, nc, body, (carry0,) * hps)


def _col_spec(seq, group, hps):
    return pl.BlockSpec((1, seq, hps * HEAD_DIM), lambda b, h: (b, 0, group * (HEADS // hps) + h))


def _mlstm(z3, zg3, gate_b, conv_w, out_g):
    bsz, seq, _ = z3.shape
    hps = REC_HEADS_PER_STEP
    wide = hps * HEAD_DIM
    per_group = HEADS // hps
    smem = pl.BlockSpec(memory_space=pltpu.SMEM)
    return pl.pallas_call(
        _mlstm_kernel,
        grid=(bsz, per_group),
        in_specs=[smem,
                  _col_spec(seq, G_MQ, hps), _col_spec(seq, G_MK, hps), _col_spec(seq, G_MV, hps),
                  _col_spec(seq, G_MO, hps),
                  pl.BlockSpec((1, seq, GATE_W), lambda b, h: (b, 0, 0)),
                  pl.BlockSpec((MLSTM_CONV, wide), lambda b, h: (0, h)),
                  pl.BlockSpec((MLSTM_CONV, wide), lambda b, h: (0, per_group + h)),
                  pl.BlockSpec((1, HEAD_DIM), lambda b, h: (0, 0))],
        out_specs=pl.BlockSpec((1, seq, wide), lambda b, h: (b, 0, h)),
        out_shape=jax.ShapeDtypeStruct((bsz, seq, GROUP_W), BF16),
        scratch_shapes=[pltpu.VMEM((seq + 8, HEAD_DIM), F32), pltpu.VMEM((hps, seq, HEAD_DIM), F32),
                        pltpu.VMEM((hps, seq, HEAD_DIM), F32), pltpu.VMEM((hps, seq, 1), F32),
                        pltpu.VMEM((hps, seq, 1), F32)],
        compiler_params=_cparams(("arbitrary", "arbitrary")),
        name="mlstm",
    )(gate_b, z3, z3, z3, z3, zg3, conv_w, conv_w, out_g.reshape(1, HEAD_DIM))


def _ret_kernel(lg_ref, q_ref, k_ref, v_ref, g_ref, og_ref, y_ref):
    t = q_ref.shape[1]
    cl = RET_CHUNK
    rr = lax.broadcasted_iota(jnp.int32, (cl, cl), 0)
    cc = lax.broadcasted_iota(jnp.int32, (cl, cl), 1)
    diff = (rr - cc).astype(F32)
    jcol = lax.broadcasted_iota(jnp.int32, (cl, 1), 0).astype(F32)
    og = og_ref[...]
    consts = []
    for h in range(HEADS):
        lg = lg_ref[h]
        consts.append(dict(decay_in=jnp.where(diff >= 0, jnp.exp(lg * jnp.maximum(diff, 0.0)), 0.0),
                           zeta=jnp.exp(lg * (cl - 1.0 - jcol)), xi=jnp.exp(lg * (jcol + 1.0)),
                           g_chunk=jnp.exp(jnp.full((1, 1), lg * cl, F32))))

    def head_step(h, c, r_st):
        sl = pl.ds(pl.multiple_of(c * cl, cl), cl)
        cols = _head_cols(h)
        qc = q_ref[0, sl, cols]
        kc = k_ref[0, sl, cols] * HEAD_DIM ** -0.5
        vc = v_ref[0, sl, cols]
        scores = _dot_nt(qc, kc) * consts[h]["decay_in"]
        o = _dot(scores, vc) + _dot(qc, r_st) * consts[h]["xi"]
        r_new = consts[h]["g_chunk"] * r_st + _dot_tn(kc * consts[h]["zeta"], vc)
        gg = g_ref[0, sl, cols]
        y = _rms(o, og) * (gg * _sigmoid(gg))
        y_ref[0, sl, cols] = y.astype(y_ref.dtype)
        return r_new

    def body(c, carry):
        return tuple(head_step(h, c, carry[h]) for h in range(HEADS))

    lax.fori_loop(0, t // cl, body, (jnp.zeros((HEAD_DIM, HEAD_DIM), F32),) * HEADS)


def _retention(z3, log_gamma, out_g):
    bsz, seq, _ = z3.shape
    grp = lambda group: pl.BlockSpec((1, seq, GROUP_W), lambda b: (b, 0, group))
    return pl.pallas_call(
        _ret_kernel,
        grid=(bsz,),
        in_specs=[pl.BlockSpec(memory_space=pltpu.SMEM), grp(G_RQ), grp(G_RK), grp(G_RV), grp(G_RG),
                  pl.BlockSpec((1, HEAD_DIM), lambda b: (0, 0))],
        out_specs=pl.BlockSpec((1, seq, GROUP_W), lambda b: (b, 0, 0)),
        out_shape=jax.ShapeDtypeStruct((bsz, seq, GROUP_W), BF16),
        compiler_params=_cparams(("arbitrary",)),
        name="retention",
    )(log_gamma, z3, z3, z3, z3, out_g.reshape(1, HEAD_DIM))


BF16_SUBLANES = 16
ATT_TQ = 256
STRIP_W = 512
STRIP_SHIFT = 9
HEADS_PER_STEP = 2


def _rank_rows(vals):
    n = vals.shape[0]
    rowb = lax.broadcasted_iota(jnp.int32, vals.shape, 0)
    rank = jnp.zeros(vals.shape, jnp.int32)
    for jp in range(n):
        rv = vals[jp:jp + 1, :]
        beats = (rv > vals) | ((rv == vals) & (rowb > jp))
        rank = rank + beats.astype(jnp.int32)
    return rank


def _pad_rows(x, rows):
    return jnp.concatenate([x, jnp.zeros((rows - x.shape[0], x.shape[1]), x.dtype)], axis=0)


def _fold_lanes(op, acc, x):
    for b in range(x.shape[1] // HEAD_DIM):
        acc = op(acc, x[:, b * HEAD_DIM:(b + 1) * HEAD_DIM])
    return acc


def _block_masked_attention(heads, qi, strip_s):
    tq, w = ATT_TQ, STRIP_W
    scale = HEAD_DIM ** -0.5
    t0 = qi * tq
    last = jnp.right_shift(t0, STRIP_SHIFT)
    rr = lax.broadcasted_iota(jnp.int32, (tq, w), 0)
    cc = lax.broadcasted_iota(jnp.int32, (tq, w), 1)
    rel = rr - cc
    rel_f = rel.astype(F32)
    alibi = [(-hd["slope"]) * rel_f for hd in heads]
    n_blk = heads[0]["unsel"].shape[0]
    erow = lax.broadcasted_iota(jnp.int32, (n_blk, w), 0)
    ecol = lax.broadcasted_iota(jnp.int32, (n_blk, w), 1)

    def scores(hd, c, bias):
        expand = jnp.where(erow == jnp.right_shift(c * w + ecol, hd["blk_shift"]), NEG, 0.0).astype(BF16)
        mask_bias = lax.dot_general(hd["unsel"], expand, (((0,), (0,)), ((), ())), preferred_element_type=F32)
        kc = hd["k_s"][pl.ds(pl.multiple_of(c * w, w), w), :]
        return _dot_nt(hd["qb"], kc) * scale + bias + mask_bias

    def first_pass(c, ms):
        out = []
        for hi, hd in enumerate(heads):
            s = scores(hd, c, alibi[hi] + (-hd["slope"]) * (t0 - c * w).astype(F32))
            strip_s[hi, c] = s
            out.append(_fold_lanes(jnp.maximum, ms[hi], s))
        return tuple(out)

    ms = lax.fori_loop(0, last, first_pass, tuple(jnp.full((tq, HEAD_DIM), NEG, F32) for _ in heads))
    dist = rel + (t0 - last * w)
    row_max = []
    for hi, hd in enumerate(heads):
        bias = jnp.where(dist >= 0, (-hd["slope"]) * dist.astype(F32), NEG)
        s = scores(hd, last, bias)
        strip_s[hi, last] = s
        row_max.append(jnp.max(_fold_lanes(jnp.maximum, ms[hi], s), axis=1, keepdims=True))

    def second_pass(c, carry):
        out = []
        for hi, hd in enumerate(heads):
            l_run, acc = carry[hi]
            p = jnp.exp(strip_s[hi, c] - row_max[hi])
            vc = hd["v_s"][pl.ds(pl.multiple_of(c * w, w), w), :]
            out.append((_fold_lanes(jnp.add, l_run, p), acc + _dot(p, vc)))
        return tuple(out)

    zero = jnp.zeros((tq, HEAD_DIM), F32)
    res = lax.fori_loop(0, last + 1, second_pass, tuple((zero, zero) for _ in heads))
    return [acc / jnp.sum(l_run, axis=1, keepdims=True) for l_run, acc in res]


def _head_cols(hh):
    return slice(hh * HEAD_DIM, (hh + 1) * HEAD_DIM)


def _moba_kernel(slope_ref, q_ref, k_ref, v_ref, g_ref, y_ref, kn_s, vb_s, kmean_s, strip_s):
    hp = pl.program_id(1)
    qi = pl.program_id(2)
    t = k_ref.shape[1]
    blk = MOBA_BLOCK
    nb = t // blk

    @pl.when(qi == 0)
    def _():
        for hh in range(HEADS_PER_STEP):
            kn = _rms(k_ref[0, :, _head_cols(hh)], g_ref[1:2, :])
            kn_s[hh] = kn.astype(kn_s.dtype)
            vb_s[hh] = v_ref[0, :, _head_cols(hh)].astype(vb_s.dtype)
            kmean_s[hh] = jnp.zeros(kmean_s.shape[1:], F32)
            for j in range(nb):
                kmean_s[hh, j:j + 1, :] = jnp.mean(kn[j * blk:(j + 1) * blk, :], axis=0, keepdims=True)

    rowb = lax.broadcasted_iota(jnp.int32, (nb, ATT_TQ), 0)
    heads = []
    for hh in range(HEADS_PER_STEP):
        qn = _rms(q_ref[0, :, _head_cols(hh)], g_ref[0:1, :])
        gate = jnp.where(rowb < qi, _dot_nt(kmean_s[hh], qn)[0:nb, :], NEG)
        sel = (_rank_rows(gate) < MOBA_TOPK) & (rowb < qi)
        unsel = jnp.where(sel | (rowb == qi), 0.0, 1.0)
        heads.append(dict(qb=qn.astype(MXU_DTYPE), k_s=kn_s.at[hh], v_s=vb_s.at[hh],
                          slope=slope_ref[hp * HEADS_PER_STEP + hh], blk_shift=MOBA_SHIFT,
                          unsel=_pad_rows(unsel, BF16_SUBLANES).astype(BF16)))
    outs = _block_masked_attention(heads, qi, strip_s)
    for hh in range(HEADS_PER_STEP):
        y_ref[0, :, _head_cols(hh)] = outs[hh].astype(y_ref.dtype)


def _moba(z3, qk_g, slopes):
    bsz, seq, _ = z3.shape
    assert MOBA_BLOCK == ATT_TQ and seq % STRIP_W == 0
    hps = HEADS_PER_STEP
    wide = hps * HEAD_DIM
    per_group = HEADS // hps
    kv = lambda group: pl.BlockSpec((1, seq, wide), lambda b, h, i: (b, 0, group * per_group + h))
    return pl.pallas_call(
        _moba_kernel,
        grid=(bsz, per_group, seq // ATT_TQ),
        in_specs=[pl.BlockSpec(memory_space=pltpu.SMEM),
                  pl.BlockSpec((1, ATT_TQ, wide), lambda b, h, i: (b, i, G_BQ * per_group + h)),
                  kv(G_BK), kv(G_BV),
                  pl.BlockSpec((2, HEAD_DIM), lambda b, h, i: (0, 0))],
        out_specs=pl.BlockSpec((1, ATT_TQ, wide), lambda b, h, i: (b, i, h)),
        out_shape=jax.ShapeDtypeStruct((bsz, seq, GROUP_W), BF16),
        scratch_shapes=[pltpu.VMEM((hps, seq, HEAD_DIM), MXU_DTYPE), pltpu.VMEM((hps, seq, HEAD_DIM), MXU_DTYPE),
                        pltpu.VMEM((hps, HEAD_DIM, HEAD_DIM), F32),
                        pltpu.VMEM((hps, seq // STRIP_W, ATT_TQ, STRIP_W), F32)],
        compiler_params=_cparams(("arbitrary", "arbitrary", "arbitrary")),
        name="moba",
    )(slopes, z3, z3, z3, qk_g)


def _gelu_tanh(x):
    return 0.5 * x * (1.0 + jnp.tanh(0.7978845608028654 * (x + 0.044715 * (x * x * x))))


def _nsa_kernel(slope_ref, q_ref, kc0_ref, kc1_ref, vc0_ref, vc1_ref, ks_ref, vs_ref, kw_ref, vw_ref, zg_ref,
                qg_ref, kg_ref, pe_ref, w1_ref, w2_ref, y_ref,
                kcmp_s, vcmp_s, ksn_s, vsb_s, kwn_s, vwb_s, strip_s):
    kc_refs, vc_refs = (kc0_ref, kc1_ref), (vc0_ref, vc1_ref)
    hp = pl.program_id(1)
    qi = pl.program_id(2)
    t = ks_ref.shape[1]
    tq = ATT_TQ
    nsub = t // NSA_CMP_STRIDE
    n_cmp = nsub - 1
    n_sel = t // NSA_SEL_BLOCK
    scale = HEAD_DIM ** -0.5

    @pl.when(qi == 0)
    def _():
        for hh in range(HEADS_PER_STEP):
            cols = _head_cols(hh)
            for cv, (src, dst) in enumerate(((kc_refs[hh], kcmp_s), (vc_refs[hh], vcmp_s))):
                acc_a = jnp.zeros((nsub, HEAD_DIM), F32)
                acc_b = jnp.zeros((nsub, HEAD_DIM), F32)
                for r in range(NSA_CMP_STRIDE):
                    zr = src[0, pl.ds(r, nsub, stride=NSA_CMP_STRIDE), :]
                    acc_a = acc_a + _dot(zr + pe_ref[cv, r:r + 1, :], w1_ref[cv, r])
                    rb = NSA_CMP_STRIDE + r
                    acc_b = acc_b + _dot(zr + pe_ref[cv, rb:rb + 1, :], w1_ref[cv, rb])
                hid = _gelu_tanh(acc_a + pltpu.roll(acc_b, nsub - 1, axis=0))
                cmp = _dot(hid, w2_ref[cv])
                if cv == 0:
                    cmp = _rms(cmp, kg_ref[0:1, :])
                dst[hh] = cmp.astype(dst.dtype)
            ksn_s[hh] = _rms(ks_ref[0, :, cols], kg_ref[1:2, :]).astype(ksn_s.dtype)
            vsb_s[hh] = vs_ref[0, :, cols].astype(vsb_s.dtype)
            kwn_s[hh] = _rms(kw_ref[0, :, cols], kg_ref[2:3, :]).astype(kwn_s.dtype)
            vwb_s[hh] = vw_ref[0, :, cols].astype(vwb_s.dtype)

    t0 = qi * tq
    rowi = lax.broadcasted_iota(jnp.int32, (tq, HEAD_DIM), 0)
    lane = lax.broadcasted_iota(jnp.int32, (tq, HEAD_DIM), 1)
    dist_c = (t0 + rowi) - (lane * NSA_CMP_STRIDE + (NSA_CMP_LEN - 1))
    ok_c = (dist_c >= 0) & (lane < n_cmp)
    dist_cf = dist_c.astype(F32)

    ob = lax.broadcasted_iota(jnp.int32, (HEAD_DIM, nsub), 0)
    oc = lax.broadcasted_iota(jnp.int32, (HEAD_DIM, nsub), 1)
    overlap_t = ((oc * NSA_CMP_STRIDE <= ob * NSA_SEL_BLOCK + (NSA_SEL_BLOCK - 1))
                 & (oc * NSA_CMP_STRIDE + (NSA_CMP_LEN - 1) >= ob * NSA_SEL_BLOCK)
                 & (oc < n_cmp) & (ob < n_sel)).astype(F32)
    rowb = lax.broadcasted_iota(jnp.int32, (n_sel, tq), 0)
    cur = jnp.right_shift(t0 + lax.broadcasted_iota(jnp.int32, (n_sel, tq), 1), NSA_SEL_SHIFT)

    win_w = NSA_WINDOW + tq
    k0 = jnp.maximum(t0 - NSA_WINDOW, 0)
    wr = lax.broadcasted_iota(jnp.int32, (tq, win_w), 0)
    wc = lax.broadcasted_iota(jnp.int32, (tq, win_w), 1)
    dist_w = (t0 - k0) + wr - wc
    ok_w = (dist_w >= 0) & (dist_w < NSA_WINDOW)
    dist_wf = dist_w.astype(F32)
    win_rows = pl.ds(pl.multiple_of(k0, tq), win_w)

    heads, o_cmp, o_win = [], [], []
    for hh in range(HEADS_PER_STEP):
        slope = slope_ref[hp * HEADS_PER_STEP + hh]
        qn = _rms(q_ref[0, :, _head_cols(hh)], qg_ref[...])
        qb = qn.astype(MXU_DTYPE)

        s_c = jnp.where(ok_c, _dot_nt(qb, kcmp_s[hh]) * scale - slope * dist_cf, NEG)
        m_c = jnp.max(s_c, axis=1, keepdims=True)
        e_c = jnp.where(ok_c, jnp.exp(s_c - m_c), 0.0)
        p_c = e_c / jnp.maximum(jnp.sum(e_c, axis=1, keepdims=True), 1e-30)
        o_cmp.append(_dot(p_c, vcmp_s[hh]))

        imp = _dot_split_nt(overlap_t, p_c)[0:n_sel, :]
        imp = jnp.where(rowb == cur, BIG, jnp.where(rowb < cur, imp, NEG))
        sel = (_rank_rows(imp) < NSA_SEL_TOPN) & (rowb <= cur)
        unsel = jnp.where(sel, 0.0, 1.0)
        heads.append(dict(qb=qb, k_s=ksn_s.at[hh], v_s=vsb_s.at[hh], slope=slope, blk_shift=NSA_SEL_SHIFT,
                          unsel=unsel.astype(BF16)))

        s_w = _dot_nt(qb, kwn_s[hh, win_rows, :]) * scale + jnp.where(ok_w, (-slope) * dist_wf, NEG)
        p_w = jnp.exp(s_w - jnp.max(s_w, axis=1, keepdims=True))
        o_win.append(_dot(p_w, vwb_s[hh, win_rows, :]) / jnp.sum(p_w, axis=1, keepdims=True))

    o_slc = _block_masked_attention(heads, qi, strip_s)

    zg = zg_ref[0]
    for hh in range(HEADS_PER_STEP):
        h = hp * HEADS_PER_STEP + hh
        g_cmp = _sigmoid(_lane_col(zg, GL_NG + h))
        g_slc = _sigmoid(_lane_col(zg, GL_NG + HEADS + h))
        g_win = _sigmoid(_lane_col(zg, GL_NG + 2 * HEADS + h))
        y_ref[0, :, _head_cols(hh)] = (g_cmp * o_cmp[hh] + g_slc * o_slc[hh] + g_win * o_win[hh]).astype(y_ref.dtype)


def _nsa(z3, zg3, q_g, k_g, cmp_pos, cmp_w1, cmp_w2, slopes):
    bsz, seq, _ = z3.shape
    assert seq // NSA_CMP_STRIDE == HEAD_DIM, "compressed blocks are laid out on the 128 lanes"
    assert seq % STRIP_W == 0 and seq >= NSA_WINDOW + ATT_TQ
    tq = ATT_TQ
    hps = HEADS_PER_STEP
    wide = hps * HEAD_DIM
    per_group = HEADS // hps
    kv = lambda group: pl.BlockSpec((1, seq, wide), lambda b, h, i: (b, 0, group * per_group + h))
    one = lambda group, hh: pl.BlockSpec((1, seq, HEAD_DIM), lambda b, h, i: (b, 0, group * HEADS + h * hps + hh))
    full = lambda shape: pl.BlockSpec(shape, lambda b, h, i: (0,) * len(shape))
    w1 = cmp_w1.reshape(2, NSA_CMP_LEN, HEAD_DIM, HEAD_DIM).astype(MXU_DTYPE)
    w2 = cmp_w2.astype(MXU_DTYPE)
    seq_buf = lambda: pltpu.VMEM((hps, seq, HEAD_DIM), MXU_DTYPE)
    cmp_buf = lambda: pltpu.VMEM((hps, HEAD_DIM, HEAD_DIM), MXU_DTYPE)
    return pl.pallas_call(
        _nsa_kernel,
        grid=(bsz, per_group, seq // tq),
        in_specs=[pl.BlockSpec(memory_space=pltpu.SMEM),
                  pl.BlockSpec((1, tq, wide), lambda b, h, i: (b, i, G_NQ * per_group + h)),
                  one(G_NKC, 0), one(G_NKC, 1), one(G_NVC, 0), one(G_NVC, 1),
                  kv(G_NKS), kv(G_NVS), kv(G_NKW), kv(G_NVW),
                  pl.BlockSpec((1, tq, GATE_W), lambda b, h, i: (b, i, 0)),
                  full((1, HEAD_DIM)), full((3, HEAD_DIM)), full((2, NSA_CMP_LEN, HEAD_DIM)),
                  full((2, NSA_CMP_LEN, HEAD_DIM, HEAD_DIM)), full((2, HEAD_DIM, HEAD_DIM))],
        out_specs=pl.BlockSpec((1, tq, wide), lambda b, h, i: (b, i, h)),
        out_shape=jax.ShapeDtypeStruct((bsz, seq, GROUP_W), BF16),
        scratch_shapes=[cmp_buf(), cmp_buf(), seq_buf(), seq_buf(), seq_buf(), seq_buf(),
                        pltpu.VMEM((hps, seq // STRIP_W, tq, STRIP_W), F32)],
        compiler_params=_cparams(("arbitrary", "arbitrary", "arbitrary")),
        name="nsa",
    )(slopes, z3, z3, z3, z3, z3, z3, z3, z3, z3, zg3, q_g.reshape(1, HEAD_DIM), k_g, cmp_pos, w1, w2)


def _out_proj_kernel(ym_ref, yb_ref, yr_ref, yn_ref, w_ref, x_ref, g_ref, o_ref):
    acc = jnp.dot(ym_ref[...], w_ref[0, 0:GROUP_W, :], preferred_element_type=F32)
    acc = acc + jnp.dot(yb_ref[...], w_ref[0, GROUP_W:2 * GROUP_W, :], preferred_element_type=F32)
    acc = acc + jnp.dot(yr_ref[...], w_ref[0, 2 * GROUP_W:3 * GROUP_W, :], preferred_element_type=F32)
    acc = acc + jnp.dot(yn_ref[...], w_ref[0, 3 * GROUP_W:4 * GROUP_W, :], preferred_element_type=F32)
    o_ref[...] = x_ref[...] + g_ref[0] * acc


def _out_proj(ys, w_out, layer, x2d, mod3, seq):
    n, d = x2d.shape
    tm, tn = 1024, 1024
    per_b = seq // tm
    y_spec = pl.BlockSpec((tm, GROUP_W), lambda i, j: (i, 0))
    return pl.pallas_call(
        _out_proj_kernel,
        grid=(n // tm, d // tn),
        in_specs=[y_spec, y_spec, y_spec, y_spec,
                  pl.BlockSpec((1, 4 * GROUP_W, tn), lambda i, j: (layer, 0, j)),
                  pl.BlockSpec((tm, tn), lambda i, j: (i, j)),
                  pl.BlockSpec((1, 1, tn), lambda i, j: (i // per_b, 0, 2 * (d // tn) + j))],
        out_specs=pl.BlockSpec((tm, tn), lambda i, j: (i, j)),
        out_shape=jax.ShapeDtypeStruct((n, d), F32),
        compiler_params=_cparams(("arbitrary", "arbitrary")),
        name="out_proj",
    )(*[y.reshape(n, GROUP_W) for y in ys], w_out, x2d, mod3)


SLAB_E0, SLAB_E1, SLAB_R0, SLAB_R1, SLAB_G0, SLAB_G1 = 0, 1, 2, 3, 4, 5


def _route_kernel(x_ref, g_ref, sc_ref, sh_ref, wr_ref, br_ref, h_ref, slab_ref, cnt_ref, carry_s):
    tm = x_ref.shape[0]

    @pl.when(pl.program_id(0) == 0)
    def _():
        carry_s[...] = jnp.zeros(carry_s.shape, F32)

    hmod = _rms(x_ref[...], g_ref[...]) * (1.0 + sc_ref[0]) + sh_ref[0]
    h_ref[...] = hmod
    logits = _dot(hmod, wr_ref[...]) + br_ref[...]
    lane = lax.broadcasted_iota(jnp.int32, logits.shape, 1).astype(F32)
    far = 4.0 * GATE_W

    in_g = lane < N_GROUPS
    lg = jnp.where(in_g, logits, NEG)
    g_max = jnp.max(lg, axis=1, keepdims=True)
    grp = jnp.min(jnp.where(in_g & (lg == g_max), lane, far), axis=1, keepdims=True)
    p_grp = 1.0 / jnp.sum(jnp.where(in_g, jnp.exp(lg - g_max), 0.0), axis=1, keepdims=True)

    lo = N_GROUPS + grp * EXPERTS_PER_GROUP
    in_e = (lane >= lo) & (lane < lo + EXPERTS_PER_GROUP)
    le = jnp.where(in_e, logits, NEG)
    e_max = jnp.max(le, axis=1, keepdims=True)
    ee = jnp.where(in_e, jnp.exp(le - e_max), 0.0)
    pe = jnp.where(in_e, ee / jnp.sum(ee, axis=1, keepdims=True), -1.0)
    p1 = jnp.max(pe, axis=1, keepdims=True)
    i1 = jnp.min(jnp.where(pe == p1, lane, far), axis=1, keepdims=True)
    pe2 = jnp.where(lane == i1, -1.0, pe)
    p2 = jnp.max(pe2, axis=1, keepdims=True)
    i2 = jnp.min(jnp.where(pe2 == p2, lane, far), axis=1, keepdims=True)
    e0 = i1 - N_GROUPS
    e1 = i2 - N_GROUPS
    g0 = p_grp * (p1 / (p1 + p2))
    g1 = p_grp * (p2 / (p1 + p2))

    onehot = ((lane == e0) | (lane == e1)).astype(BF16)
    rr = lax.broadcasted_iota(jnp.int32, (tm, tm), 0)
    cc = lax.broadcasted_iota(jnp.int32, (tm, tm), 1)
    before = jnp.dot((cc < rr).astype(BF16), onehot, preferred_element_type=F32) + carry_s[...]
    r0 = jnp.sum(jnp.where(lane == e0, before, 0.0), axis=1, keepdims=True)
    r1 = jnp.sum(jnp.where(lane == e1, before, 0.0), axis=1, keepdims=True)
    carry_s[...] = carry_s[...] + jnp.sum(onehot.astype(F32), axis=0, keepdims=True)
    cnt_ref[...] = carry_s[...]

    slab = jnp.where(lane == SLAB_E0, e0.astype(F32), 0.0)
    slab = jnp.where(lane == SLAB_E1, e1.astype(F32), slab)
    slab = jnp.where(lane == SLAB_R0, r0, slab)
    slab = jnp.where(lane == SLAB_R1, r1, slab)
    slab = jnp.where(lane == SLAB_G0, g0, slab)
    slab = jnp.where(lane == SLAB_G1, g1, slab)
    slab_ref[...] = slab


def _route(x2d, norm_g, mod3, w_router, b_router, seq):
    n, d = x2d.shape
    tm = 512
    per_b = seq // tm
    return pl.pallas_call(
        _route_kernel,
        grid=(n // tm,),
        in_specs=[pl.BlockSpec((tm, d), lambda i: (i, 0)),
                  pl.BlockSpec((1, d), lambda i: (0, 0)),
                  pl.BlockSpec((1, 1, d), lambda i: (i // per_b, 0, 4)),
                  pl.BlockSpec((1, 1, d), lambda i: (i // per_b, 0, 3)),
                  pl.BlockSpec((d, GATE_W), lambda i: (0, 0)),
                  pl.BlockSpec((1, GATE_W), lambda i: (0, 0))],
        out_specs=[pl.BlockSpec((tm, d), lambda i: (i, 0)),
                   pl.BlockSpec((tm, GATE_W), lambda i: (i, 0)),
                   pl.BlockSpec((1, GATE_W), lambda i: (0, 0))],
        out_shape=[jax.ShapeDtypeStruct((n, d), F32),
                   jax.ShapeDtypeStruct((n, GATE_W), F32),
                   jax.ShapeDtypeStruct((1, GATE_W), F32)],
        scratch_shapes=[pltpu.VMEM((1, GATE_W), F32)],
        compiler_params=_cparams(("arbitrary",)),
        name="route",
    )(x2d, norm_g.reshape(1, d), mod3, mod3, w_router, b_router)


def _invert_kernel(dest_ref, pad_lo_ref, pad_hi_ref, inv_ref):
    def clear(i, _):
        inv_ref[i] = 0
        return 0

    def clear_segment(g, _):
        lax.fori_loop(pad_lo_ref[g], pad_hi_ref[g], clear, 0)
        return 0

    def put(a, _):
        inv_ref[dest_ref[a]] = jnp.right_shift(a, 1)
        return 0

    lax.fori_loop(0, pad_lo_ref.shape[0], clear_segment, 0)
    lax.fori_loop(0, dest_ref.shape[0], put, 0, unroll=32)


def _invert(dest_flat, pad_lo, pad_hi, cap):
    smem = pl.BlockSpec(memory_space=pltpu.SMEM)
    return pl.pallas_call(
        _invert_kernel,
        in_specs=[smem, smem, smem],
        out_specs=smem,
        out_shape=jax.ShapeDtypeStruct((cap,), jnp.int32),
        name="invert",
    )(dest_flat, pad_lo, pad_hi)


def _start_row_gather(idx_ref, src_hbm, dst, sem, n_rows, first=0):
    for r in range(first, n_rows):
        pltpu.make_async_copy(src_hbm.at[pl.ds(idx_ref[r], 1)], dst.at[pl.ds(r, 1)], sem).start(
            priority=ROW_GATHER_DMA_QUEUE)


def _wait_row_gather(src_hbm, dst, sem, n_rows):
    pltpu.make_async_copy(src_hbm.at[pl.ds(0, n_rows)], dst, sem).wait()


def _expert_kernel(blk_e_ref, n_used_ref, run_slot_ref, next_e_ref, next_ok_ref, inv_ref, h_hbm,
                   w1_hbm, w3_hbm, w2_hbm, y_ref,
                   x_s, w1_f, w3_f, w2_f, w1_s, w3_s, w2_s, sem, wsem, *, layer):
    s = pl.program_id(0)
    rows = EXPERT_ROWS
    n_used = n_used_ref[0]
    slot = lax.rem(s, 2)

    blk = s - 1
    prev = jnp.maximum(blk - 1, 0)
    gather = s < n_used
    compute = (s >= 1) & (blk < n_used)

    def start_gather(first, stop):
        _start_row_gather(inv_ref, h_hbm, x_s.at[slot], sem.at[slot], stop, first)

    def weight_copies(e, wslot):
        return (pltpu.make_async_copy(w1_hbm.at[layer, e], w1_f.at[wslot], wsem.at[wslot, 0]),
                pltpu.make_async_copy(w3_hbm.at[layer, e], w3_f.at[wslot], wsem.at[wslot, 1]),
                pltpu.make_async_copy(w2_hbm.at[layer, e], w2_f.at[wslot], wsem.at[wslot, 2]))

    def expert_block(gather_too):
        @pl.when((blk == 0) | (blk_e_ref[blk] != blk_e_ref[prev]))
        def _():
            wslot = run_slot_ref[blk]
            for cp in weight_copies(blk_e_ref[blk], wslot):
                cp.wait()

            @pl.when(next_ok_ref[blk] == 1)
            def _():
                for cp in weight_copies(next_e_ref[blk], 1 - wslot):
                    cp.start()

            w1_s[...] = w1_f[wslot].astype(w1_s.dtype)
            w3_s[...] = w3_f[wslot].astype(w3_s.dtype)
            w2_s[...] = w2_f[wslot].astype(w2_s.dtype)

        cuts = (0, rows // 4, rows // 2, rows) if gather_too else (0, 0, 0, 0)
        _wait_row_gather(h_hbm, x_s.at[1 - slot], sem.at[1 - slot], rows)
        start_gather(cuts[0], cuts[1])
        x = x_s[1 - slot].astype(MXU_DTYPE)
        a = jnp.dot(x, w1_s[...], preferred_element_type=F32)
        start_gather(cuts[1], cuts[2])
        b = jnp.dot(x, w3_s[...], preferred_element_type=F32)
        start_gather(cuts[2], cuts[3])
        y_ref[...] = _dot(a * _sigmoid(a) * b, w2_s[...])

    @pl.when(gather & compute)
    def _():
        expert_block(True)

    @pl.when(gather & jnp.logical_not(compute))
    def _():
        for cp in weight_copies(blk_e_ref[0], run_slot_ref[0]):
            cp.start()
        start_gather(0, rows)

    @pl.when(compute & jnp.logical_not(gather))
    def _():
        expert_block(False)

    @pl.when((s >= 1) & (blk >= n_used))
    def _():
        y_ref[...] = jnp.zeros(y_ref.shape, y_ref.dtype)


def _experts(h2, inv, blk_e, n_used, w1, w3, w2, layer):
    d = D_MODEL
    rows = EXPERT_ROWS
    n_blocks = inv.shape[0] // rows
    idx = jnp.arange(n_blocks, dtype=jnp.int32)
    run_start = jnp.concatenate([jnp.ones((1,), jnp.int32), (blk_e[1:] != blk_e[:-1]).astype(jnp.int32)])
    run_slot = (jnp.cumsum(run_start) - 1) % 2
    later_run = (idx[None, :] > idx[:, None]) & (blk_e[None, :] != blk_e[:, None])
    next_start = jnp.min(jnp.where(later_run, idx[None, :], n_blocks), axis=1)
    next_e = blk_e[jnp.minimum(next_start, n_blocks - 1)]
    next_ok = (next_start < n_used[0]).astype(jnp.int32)

    done = lambda s: jnp.maximum(s - 1, 0)
    hbm = pl.BlockSpec(memory_space=pl.ANY)
    return pl.pallas_call(
        functools.partial(_expert_kernel, layer=layer),
        grid_spec=pltpu.PrefetchScalarGridSpec(
            num_scalar_prefetch=5,
            grid=(n_blocks + 1,),
            in_specs=[pl.BlockSpec((rows,), lambda s, *_: (jnp.minimum(s, n_blocks - 1),), memory_space=pltpu.SMEM),
                      hbm, hbm, hbm, hbm],
            out_specs=pl.BlockSpec((rows, d), lambda s, *_: (done(s), 0)),
            scratch_shapes=[pltpu.VMEM((2, rows, d), F32),
                            pltpu.VMEM((2, d, D_EXPERT), F32), pltpu.VMEM((2, d, D_EXPERT), F32),
                            pltpu.VMEM((2, D_EXPERT, d), F32),
                            pltpu.VMEM((d, D_EXPERT), MXU_DTYPE), pltpu.VMEM((d, D_EXPERT), MXU_DTYPE),
                            pltpu.VMEM((D_EXPERT, d), MXU_DTYPE),
                            pltpu.SemaphoreType.DMA((2,)), pltpu.SemaphoreType.DMA((2, 3))]),
        out_shape=jax.ShapeDtypeStruct((n_blocks * rows, d), F32),
        compiler_params=_cparams(("arbitrary",)),
        name="experts",
    )(blk_e, n_used, run_slot.astype(jnp.int32), next_e.astype(jnp.int32), next_ok, inv, h2, w1, w3, w2)


def _combine_kernel(d0_ref, d1_ref, x_ref, g_ref, slab_ref, yb_hbm, o_ref, rows_s, sem):
    s = pl.program_id(0)
    n_tiles = pl.num_programs(0) - 1
    tm = x_ref.shape[0]
    slot = lax.rem(s, 2)

    @pl.when(s < n_tiles)
    def _():
        _start_row_gather(d0_ref, yb_hbm, rows_s.at[slot, 0], sem.at[slot], tm)
        _start_row_gather(d1_ref, yb_hbm, rows_s.at[slot, 1], sem.at[slot], tm)

    @pl.when(s >= 1)
    def _():
        _wait_row_gather(yb_hbm, rows_s.at[1 - slot, 0], sem.at[1 - slot], tm)
        _wait_row_gather(yb_hbm, rows_s.at[1 - slot, 1], sem.at[1 - slot], tm)
        route = slab_ref[...]
        g0 = route[:, SLAB_G0:SLAB_G0 + 1]
        g1 = route[:, SLAB_G1:SLAB_G1 + 1]
        o_ref[...] = x_ref[...] + g_ref[0] * (g0 * rows_s[1 - slot, 0] + g1 * rows_s[1 - slot, 1])


def _combine(x2d, mod3, slab, dest0, dest1, yb, seq):
    n, d = x2d.shape
    tm = 256
    per_b = seq // tm
    n_tiles = n // tm
    done = lambda s: jnp.maximum(s - 1, 0)
    idx_spec = pl.BlockSpec((tm,), lambda s: (jnp.minimum(s, n_tiles - 1),), memory_space=pltpu.SMEM)
    return pl.pallas_call(
        _combine_kernel,
        grid=(n_tiles + 1,),
        in_specs=[idx_spec, idx_spec,
                  pl.BlockSpec((tm, d), lambda s: (done(s), 0)),
                  pl.BlockSpec((1, 1, d), lambda s: (done(s) // per_b, 0, 5)),
                  pl.BlockSpec((tm, GATE_W), lambda s: (done(s), 0)),
                  pl.BlockSpec(memory_space=pl.ANY)],
        out_specs=pl.BlockSpec((tm, d), lambda s: (done(s), 0)),
        out_shape=jax.ShapeDtypeStruct((n, d), F32),
        scratch_shapes=[pltpu.VMEM((2, 2, tm, d), F32), pltpu.SemaphoreType.DMA((2,))],
        compiler_params=_cparams(("arbitrary",)),
        name="combine",
    )(dest0, dest1, x2d, mod3, slab, yb)


_OFF_MI = 4 * GROUP_W
_OFF_BQ = _OFF_MI + 2 * HEADS
_OFF_NG = _OFF_BQ + 14 * GROUP_W


def _pack_w_kernel(w_ref, o_ref, og_ref):
    gap = _OFF_BQ - _OFF_MI
    o_ref[0, :, 0:_OFF_MI] = w_ref[0, :, 0:_OFF_MI].astype(o_ref.dtype)
    tail = w_ref[0, :, _OFF_MI:]
    width = tail.shape[1]
    o_ref[0, :, _OFF_MI:] = pltpu.roll(tail, width - gap, axis=1)[:, 0:D_WIDE - _OFF_MI].astype(o_ref.dtype)
    lane = lax.broadcasted_iota(jnp.int32, (w_ref.shape[1], GATE_W), 1)
    first = w_ref[0, :, _OFF_MI:_OFF_MI + GATE_W]
    ragged = w_ref[0, :, _OFF_NG - GL_NG:_OFF_NG - GL_NG + GATE_W]
    gate = jnp.where(lane < GL_NG, first, jnp.where(lane < GL_NG + 3 * HEADS, ragged, 0.0))
    og_ref[0] = gate.astype(og_ref.dtype)


def _pack_w_in(w_in):
    depth, d, d_in = w_in.shape
    tk = 128
    lanes_in = -(-d_in // GATE_W) * GATE_W
    assert (_OFF_NG - GL_NG) % GATE_W == 0 and _OFF_NG - GL_NG + GATE_W == lanes_in
    return pl.pallas_call(
        _pack_w_kernel,
        grid=(depth, d // tk),
        in_specs=[pl.BlockSpec((1, tk, lanes_in), lambda l, i: (l, i, 0))],
        out_specs=[pl.BlockSpec((1, tk, D_WIDE), lambda l, i: (l, i, 0)),
                   pl.BlockSpec((1, tk, GATE_W), lambda l, i: (l, i, 0))],
        out_shape=[jax.ShapeDtypeStruct((depth, d, D_WIDE), MXU_DTYPE),
                   jax.ShapeDtypeStruct((depth, d, GATE_W), MXU_DTYPE)],
        compiler_params=_cparams(("arbitrary", "arbitrary")),
        name="pack_w_in",
    )(w_in)


def _moe(x2d, norm_g, mod3, wg, bg, we, be, w1, w3, w2, layer, seq):
    n, d = x2d.shape
    n_route = N_GROUPS + N_EXPERTS
    w_router = jnp.concatenate([wg, we, jnp.zeros((d, GATE_W - n_route), wg.dtype)], axis=1).astype(MXU_DTYPE)
    b_router = jnp.concatenate([bg, be, jnp.zeros((GATE_W - n_route,), bg.dtype)]).reshape(1, GATE_W)
    h2, slab, cnt = _route(x2d, norm_g, mod3, w_router, b_router, seq)

    rows = EXPERT_ROWS
    counts = cnt[0, :N_EXPERTS].astype(jnp.int32)
    pcounts = (counts + rows - 1) // rows * rows
    pends = jnp.cumsum(pcounts)
    pstarts = pends - pcounts
    eid = slab[:, SLAB_E0:SLAB_E1 + 1].astype(jnp.int32)
    rank = slab[:, SLAB_R0:SLAB_R1 + 1].astype(jnp.int32)
    dest = pstarts[eid] + rank
    n_blocks = -(-2 * n // rows) + N_EXPERTS
    blk_row0 = jnp.arange(n_blocks, dtype=jnp.int32) * rows
    blk_e = jnp.minimum(jnp.sum((pends[None, :] <= blk_row0[:, None]).astype(jnp.int32), axis=1), N_EXPERTS - 1)
    n_used = (pends[-1:] // rows).astype(jnp.int32)

    cap = n_blocks * rows
    pad_lo = jnp.concatenate([pstarts + counts, pends[-1:]]).astype(jnp.int32)
    pad_hi = jnp.concatenate([pends, jnp.full((1,), cap, jnp.int32)]).astype(jnp.int32)
    inv = _invert(dest.reshape(-1), pad_lo, pad_hi, cap)
    yb = _experts(h2, inv, blk_e, n_used, w1, w3, w2, layer)
    return _combine(x2d, mod3, slab, dest[:, 0], dest[:, 1], yb, seq)


def _layer(x2d, mod, bsz, seq, layer, norm1_g, norm2_g, w_wide, w_gate, mlstm_gate_b, mlstm_conv_w, mlstm_out_g,
           moba_qk_g, ret_out_g, nsa_q_g, nsa_k_g, nsa_cmp_pos, nsa_cmp_w1, nsa_cmp_w2, w_out, router_g_w,
           router_g_b, router_e_w, router_e_b, exp_w1, exp_w3, exp_w2, slopes, log_gamma):
    n, d = x2d.shape
    mod3 = mod.reshape(bsz, 1, 6 * d)
    z, zg = _norm_in_proj(x2d, norm1_g, mod3, w_wide, layer, w_gate, seq)
    z3 = z.reshape(bsz, seq, D_WIDE)
    zg3 = zg.reshape(bsz, seq, GATE_W)
    y_m = _mlstm(z3, zg3, mlstm_gate_b, mlstm_conv_w, mlstm_out_g)
    y_b = _moba(z3, moba_qk_g, slopes[0::2])
    y_r = _retention(z3, log_gamma, ret_out_g)
    y_n = _nsa(z3, zg3, nsa_q_g, nsa_k_g, nsa_cmp_pos, nsa_cmp_w1, nsa_cmp_w2, slopes[1::2])
    x2d = _out_proj((y_m, y_b, y_r, y_n), w_out, layer, x2d, mod3, seq)
    return _moe(x2d, norm2_g, mod3, router_g_w, router_g_b, router_e_w, router_e_b, exp_w1, exp_w3, exp_w2,
                layer, seq)


def kernel(x, c, norm1_g, norm2_g, ada_w, ada_b, w_in, mlstm_gate_b, mlstm_conv_w, mlstm_out_g, moba_qk_g,
           ret_out_g, nsa_q_g, nsa_k_g, nsa_cmp_pos, nsa_cmp_w1, nsa_cmp_w2, w_out, router_g_w, router_g_b,
           router_e_w, router_e_b, exp_w1, exp_w3, exp_w2):
    bsz, seq, d = x.shape
    depth = ada_w.shape[0]
    n_softmax_heads = 2 * HEADS
    slopes = jnp.exp2(-8.0 * jnp.arange(1, n_softmax_heads + 1, dtype=F32) / n_softmax_heads)
    log_gamma = jnp.log(1.0 - jnp.exp2(-5.0 - jnp.arange(HEADS, dtype=F32)))
    mod = _ada_mod(c, ada_w, ada_b)
    w_wide, w_gate = _pack_w_in(w_in)
    w_out_b = w_out.astype(MXU_DTYPE)
    x2d = x.reshape(bsz * seq, d)
    for l in range(depth):
        x2d = _layer(x2d, mod[l], bsz, seq, l, norm1_g[l], norm2_g[l], w_wide, w_gate, mlstm_gate_b[l],
                     mlstm_conv_w[l], mlstm_out_g[l], moba_qk_g[l], ret_out_g[l], nsa_q_g[l], nsa_k_g[l],
                     nsa_cmp_pos[l], nsa_cmp_w1[l], nsa_cmp_w2[l], w_out_b, router_g_w[l], router_g_b[l],
                     router_e_w[l], router_e_b[l], exp_w1, exp_w3, exp_w2, slopes, log_gamma)
    return x2d.reshape(bsz, seq, d)
```

```python
import functools

import jax
import jax.numpy as jnp
from jax import lax
from jax.experimental import pallas as pl
from jax.experimental.pallas import tpu as pltpu

F32 = jnp.float32
BF16 = jnp.bfloat16
MXU_DTYPE = jnp.bfloat16

D_MODEL = 2048
HEAD_DIM = 128
HEADS = 4
GROUP_W = HEADS * HEAD_DIM
N_WIDE_GROUPS = 18
D_WIDE = N_WIDE_GROUPS * GROUP_W
GATE_W = 128

MLSTM_CHUNK = 64
MLSTM_CONV = 4
MOBA_BLOCK = 256
MOBA_SHIFT = 8
MOBA_TOPK = 3
RET_CHUNK = 128
NSA_CMP_LEN = 32
NSA_CMP_STRIDE = 16
NSA_SEL_BLOCK = 64
NSA_SEL_SHIFT = 6
NSA_SEL_TOPN = 4
NSA_WINDOW = 512
N_GROUPS = 4
EXPERTS_PER_GROUP = 8
N_EXPERTS = N_GROUPS * EXPERTS_PER_GROUP
D_EXPERT = 512
EXPERT_ROWS = 256
EXPERT_GATHER_AHEAD = 2

NORM_EPS = 1e-6
NEG = -1e30
BIG = 1e9

G_MQ, G_MK, G_MV, G_MO = 0, 1, 2, 3
G_BQ, G_BK, G_BV = 4, 5, 6
G_RQ, G_RK, G_RV, G_RG = 7, 8, 9, 10
G_NQ, G_NKC, G_NVC, G_NKS, G_NVS, G_NKW, G_NVW = 11, 12, 13, 14, 15, 16, 17
GL_MI, GL_MF, GL_NG = 0, 4, 8

VMEM_LIMIT = 56 * 1024 * 1024


def _cparams(sem):
    return pltpu.CompilerParams(dimension_semantics=sem, vmem_limit_bytes=VMEM_LIMIT)


def _dot(a, b):
    return jnp.dot(a.astype(MXU_DTYPE), b.astype(MXU_DTYPE), preferred_element_type=F32)


def _dot_nt(a, b):
    return lax.dot_general(a.astype(MXU_DTYPE), b.astype(MXU_DTYPE), (((1,), (1,)), ((), ())),
                           preferred_element_type=F32)


def _dot_tn(a, b):
    return lax.dot_general(a.astype(MXU_DTYPE), b.astype(MXU_DTYPE), (((0,), (0,)), ((), ())),
                           preferred_element_type=F32)


def _dot_split_nt(b01, a):
    a_hi = a.astype(MXU_DTYPE)
    r1 = a - a_hi.astype(F32)
    a_mid = r1.astype(MXU_DTYPE)
    a_lo = (r1 - a_mid.astype(F32)).astype(MXU_DTYPE)
    return _dot_nt(b01, a_hi) + _dot_nt(b01, a_mid) + _dot_nt(b01, a_lo)


def _rms(x, g):
    return x * lax.rsqrt(jnp.mean(x * x, axis=-1, keepdims=True) + NORM_EPS) * g


def _sigmoid(x):
    return jax.nn.sigmoid(x)


def _lane_col(x, idx):
    lane = lax.broadcasted_iota(jnp.int32, x.shape, 1)
    return jnp.sum(jnp.where(lane == idx, x, 0.0), axis=1, keepdims=True)


def _ada_kernel(c_ref, w_ref, b_ref, o_ref):
    c = c_ref[...]
    o_ref[0] = _dot(c * _sigmoid(c), w_ref[0]) + b_ref[0]


def _ada_mod(c, ada_w, ada_b):
    depth, d, n6 = ada_w.shape
    b = c.shape[0]
    tn = 1024
    return pl.pallas_call(
        _ada_kernel,
        grid=(depth, n6 // tn),
        in_specs=[pl.BlockSpec((b, d), lambda l, j: (0, 0)),
                  pl.BlockSpec((1, d, tn), lambda l, j: (l, 0, j)),
                  pl.BlockSpec((1, 1, tn), lambda l, j: (l, 0, j))],
        out_specs=pl.BlockSpec((1, b, tn), lambda l, j: (l, 0, j)),
        out_shape=jax.ShapeDtypeStruct((depth, b, n6), F32),
        compiler_params=_cparams(("arbitrary", "arbitrary")),
        name="ada_mod",
    )(c, ada_w, ada_b.reshape(depth, 1, n6))


def _norm_in_kernel(x_ref, g_ref, sc_ref, sh_ref, w_ref, ws_ref, z_ref, zg_ref, h_s):
    @pl.when(pl.program_id(1) == 0)
    def _():
        h = _rms(x_ref[...], g_ref[...]) * (1.0 + sc_ref[0]) + sh_ref[0]
        hb = h.astype(MXU_DTYPE)
        h_s[...] = hb
        zg_ref[...] = jnp.dot(hb, ws_ref[0], preferred_element_type=F32)

    z_ref[...] = jnp.dot(h_s[...], w_ref[0], preferred_element_type=F32)


def _norm_in_proj(x2d, norm_g, mod3, w_wide, layer, w_gate, seq):
    n, d = x2d.shape
    tm, tn = 1024, 1024
    per_b = seq // tm
    return pl.pallas_call(
        _norm_in_kernel,
        grid=(n // tm, D_WIDE // tn),
        in_specs=[pl.BlockSpec((tm, d), lambda i, j: (i, 0)),
                  pl.BlockSpec((1, d), lambda i, j: (0, 0)),
                  pl.BlockSpec((1, 1, d), lambda i, j: (i // per_b, 0, 1)),
                  pl.BlockSpec((1, 1, d), lambda i, j: (i // per_b, 0, 0)),
                  pl.BlockSpec((1, d, tn), lambda i, j: (layer, 0, j)),
                  pl.BlockSpec((1, d, GATE_W), lambda i, j: (layer, 0, 0))],
        out_specs=[pl.BlockSpec((tm, tn), lambda i, j: (i, j)),
                   pl.BlockSpec((tm, GATE_W), lambda i, j: (i, 0))],
        out_shape=[jax.ShapeDtypeStruct((n, D_WIDE), F32),
                   jax.ShapeDtypeStruct((n, GATE_W), F32)],
        scratch_shapes=[pltpu.VMEM((tm, d), MXU_DTYPE)],
        compiler_params=_cparams(("arbitrary", "arbitrary")),
        name="norm_in_proj",
    )(x2d, norm_g.reshape(1, d), mod3, mod3, w_wide, w_gate)


REC_HEADS_PER_STEP = 2


def _mlstm_kernel(gb_ref, q_ref, k_ref, v_ref, o_ref, zg_ref, cwq_ref, cwk_ref, og_ref, y_ref,
                  pad_s, qs_s, ks_s, ic_s, fc_s):
    hp = pl.program_id(1)
    t = q_ref.shape[1]
    cl = MLSTM_CHUNK
    nc = t // cl
    hps = REC_HEADS_PER_STEP

    def conv_silu(src_ref, cw_ref, cols, dst_s, scale):
        pad_s[8:8 + t, :] = src_ref[0, :, cols]
        off = 8 - (MLSTM_CONV - 1)
        tile = 128
        for r0 in range(0, t, tile):
            acc = cw_ref[0:1, cols] * pad_s[r0 + off:r0 + off + tile, :]
            for j in range(1, MLSTM_CONV):
                acc = acc + cw_ref[j:j + 1, cols] * pad_s[r0 + off + j:r0 + off + j + tile, :]
            dst_s[r0:r0 + tile, :] = acc * _sigmoid(acc) * scale

    pad_s[0:8, :] = jnp.zeros((8, HEAD_DIM), F32)
    zg = zg_ref[0]
    for hh in range(hps):
        h = hp * hps + hh
        conv_silu(q_ref, cwq_ref, _head_cols(hh), qs_s.at[hh], HEAD_DIM ** -0.5)
        conv_silu(k_ref, cwk_ref, _head_cols(hh), ks_s.at[hh], 1.0)
        ic_s[hh] = _lane_col(zg, GL_MI + h) + gb_ref[0, h]
        f_pre = _lane_col(zg, GL_MF + h) + gb_ref[1, h]
        fc_s[hh] = jnp.minimum(f_pre, 0.0) - jnp.log1p(jnp.exp(-jnp.abs(f_pre)))

    rr = lax.broadcasted_iota(jnp.int32, (cl, cl), 0)
    cc = lax.broadcasted_iota(jnp.int32, (cl, cl), 1)
    eye = rr == cc
    causal = cc <= rr
    og = og_ref[...]

    def head_step(hh, c, carry):
        c_st, n_st, m_st = carry
        sl = pl.ds(pl.multiple_of(c * cl, cl), cl)
        cols = _head_cols(hh)
        qc = qs_s[hh, sl, :]
        kc = ks_s[hh, sl, :]
        vc = v_ref[0, sl, cols]
        i_col = ic_s[hh, sl, :]
        f_col = fc_s[hh, sl, :]
        f_row = jnp.sum(jnp.where(eye, f_col, 0.0), axis=0, keepdims=True)
        i_row = jnp.sum(jnp.where(eye, i_col, 0.0), axis=0, keepdims=True)
        a_col = jnp.sum(jnp.where(causal, f_row, 0.0), axis=1, keepdims=True)
        a_row = jnp.sum(jnp.where(rr <= cc, f_col, 0.0), axis=0, keepdims=True)
        log_d = jnp.where(causal, a_col - a_row + i_row, NEG)
        m_inter = a_col + m_st
        m_row = jnp.maximum(m_inter, jnp.max(log_d, axis=1, keepdims=True))
        s = _dot_nt(qc, kc) * jnp.exp(log_d - m_row)
        w_inter = jnp.exp(m_inter - m_row)
        num = _dot(s, vc) + w_inter * _dot(qc, c_st)
        den = jnp.sum(s, axis=1, keepdims=True) + w_inter * jnp.sum(qc * n_st, axis=1, keepdims=True)
        h_out = num / jnp.maximum(jnp.abs(den), jnp.exp(-m_row))
        a_last = jnp.sum(f_col, axis=0, keepdims=True)
        w_log = a_last - a_col + i_col
        m_new = jnp.maximum(a_last + m_st, jnp.max(w_log, axis=0, keepdims=True))
        w = jnp.exp(w_log - m_new)
        decay = jnp.exp(a_last + m_st - m_new)
        kw = kc * w
        c_new = decay * c_st + _dot_tn(kw, vc)
        n_new = decay * n_st + jnp.sum(kw, axis=0, keepdims=True)
        y = _rms(h_out, og) * _sigmoid(o_ref[0, sl, cols])
        y_ref[0, sl, cols] = y.astype(y_ref.dtype)
        return c_new, n_new, m_new

    def body(c, carry):
        return tuple(head_step(hh, c, carry[hh]) for hh in range(hps))

    carry0 = (jnp.zeros((HEAD_DIM, HEAD_DIM), F32), jnp.zeros((1, HEAD_DIM), F32), jnp.zeros((1, 1), F32))
    lax.fori_loop(0, nc, body, (carry0,) * hps)


def _col_spec(seq, group, hps):
    return pl.BlockSpec((1, seq, hps * HEAD_DIM), lambda b, h: (b, 0, group * (HEADS // hps) + h))


def _mlstm(z3, zg3, gate_b, conv_w, out_g):
    bsz, seq, _ = z3.shape
    hps = REC_HEADS_PER_STEP
    wide = hps * HEAD_DIM
    per_group = HEADS // hps
    smem = pl.BlockSpec(memory_space=pltpu.SMEM)
    return pl.pallas_call(
        _mlstm_kernel,
        grid=(bsz, per_group),
        in_specs=[smem,
                  _col_spec(seq, G_MQ, hps), _col_spec(seq, G_MK, hps), _col_spec(seq, G_MV, hps),
                  _col_spec(seq, G_MO, hps),
                  pl.BlockSpec((1, seq, GATE_W), lambda b, h: (b, 0, 0)),
                  pl.BlockSpec((MLSTM_CONV, wide), lambda b, h: (0, h)),
                  pl.BlockSpec((MLSTM_CONV, wide), lambda b, h: (0, per_group + h)),
                  pl.BlockSpec((1, HEAD_DIM), lambda b, h: (0, 0))],
        out_specs=pl.BlockSpec((1, seq, wide), lambda b, h: (b, 0, h)),
        out_shape=jax.ShapeDtypeStruct((bsz, seq, GROUP_W), BF16),
        scratch_shapes=[pltpu.VMEM((seq + 8, HEAD_DIM), F32), pltpu.VMEM((hps, seq, HEAD_DIM), F32),
                        pltpu.VMEM((hps, seq, HEAD_DIM), F32), pltpu.VMEM((hps, seq, 1), F32),
                        pltpu.VMEM((hps, seq, 1), F32)],
        compiler_params=_cparams(("arbitrary", "arbitrary")),
        name="mlstm",
    )(gate_b, z3, z3, z3, z3, zg3, conv_w, conv_w, out_g.reshape(1, HEAD_DIM))


def _ret_kernel(lg_ref, q_ref, k_ref, v_ref, g_ref, og_ref, y_ref):
    t = q_ref.shape[1]
    cl = RET_CHUNK
    rr = lax.broadcasted_iota(jnp.int32, (cl, cl), 0)
    cc = lax.broadcasted_iota(jnp.int32, (cl, cl), 1)
    diff = (rr - cc).astype(F32)
    jcol = lax.broadcasted_iota(jnp.int32, (cl, 1), 0).astype(F32)
    og = og_ref[...]
    consts = []
    for h in range(HEADS):
        lg = lg_ref[h]
        consts.append(dict(decay_in=jnp.where(diff >= 0, jnp.exp(lg * jnp.maximum(diff, 0.0)), 0.0),
                           zeta=jnp.exp(lg * (cl - 1.0 - jcol)), xi=jnp.exp(lg * (jcol + 1.0)),
                           g_chunk=jnp.exp(jnp.full((1, 1), lg * cl, F32))))

    def head_step(h, c, r_st):
        sl = pl.ds(pl.multiple_of(c * cl, cl), cl)
        cols = _head_cols(h)
        qc = q_ref[0, sl, cols]
        kc = k_ref[0, sl, cols] * HEAD_DIM ** -0.5
        vc = v_ref[0, sl, cols]
        scores = _dot_nt(qc, kc) * consts[h]["decay_in"]
        o = _dot(scores, vc) + _dot(qc, r_st) * consts[h]["xi"]
        r_new = consts[h]["g_chunk"] * r_st + _dot_tn(kc * consts[h]["zeta"], vc)
        gg = g_ref[0, sl, cols]
        y = _rms(o, og) * (gg * _sigmoid(gg))
        y_ref[0, sl, cols] = y.astype(y_ref.dtype)
        return r_new

    def body(c, carry):
        return tuple(head_step(h, c, carry[h]) for h in range(HEADS))

    lax.fori_loop(0, t // cl, body, (jnp.zeros((HEAD_DIM, HEAD_DIM), F32),) * HEADS)


def _retention(z3, log_gamma, out_g):
    bsz, seq, _ = z3.shape
    grp = lambda group: pl.BlockSpec((1, seq, GROUP_W), lambda b: (b, 0, group))
    return pl.pallas_call(
        _ret_kernel,
        grid=(bsz,),
        in_specs=[pl.BlockSpec(memory_space=pltpu.SMEM), grp(G_RQ), grp(G_RK), grp(G_RV), grp(G_RG),
                  pl.BlockSpec((1, HEAD_DIM), lambda b: (0, 0))],
        out_specs=pl.BlockSpec((1, seq, GROUP_W), lambda b: (b, 0, 0)),
        out_shape=jax.ShapeDtypeStruct((bsz, seq, GROUP_W), BF16),
        compiler_params=_cparams(("arbitrary",)),
        name="retention",
    )(log_gamma, z3, z3, z3, z3, out_g.reshape(1, HEAD_DIM))


BF16_SUBLANES = 16
ATT_TQ = 256
STRIP_W = 512
STRIP_SHIFT = 9
HEADS_PER_STEP = 2


def _rank_rows(vals):
    n = vals.shape[0]
    rowb = lax.broadcasted_iota(jnp.int32, vals.shape, 0)
    rank = jnp.zeros(vals.shape, jnp.int32)
    for jp in range(n):
        rv = vals[jp:jp + 1, :]
        beats = (rv > vals) | ((rv == vals) & (rowb > jp))
        rank = rank + beats.astype(jnp.int32)
    return rank


def _pad_rows(x, rows):
    return jnp.concatenate([x, jnp.zeros((rows - x.shape[0], x.shape[1]), x.dtype)], axis=0)


def _fold_lanes(op, acc, x):
    for b in range(x.shape[1] // HEAD_DIM):
        acc = op(acc, x[:, b * HEAD_DIM:(b + 1) * HEAD_DIM])
    return acc


def _block_masked_attention(heads, qi, strip_s):
    tq, w = ATT_TQ, STRIP_W
    scale = HEAD_DIM ** -0.5
    t0 = qi * tq
    last = jnp.right_shift(t0, STRIP_SHIFT)
    rr = lax.broadcasted_iota(jnp.int32, (tq, w), 0)
    cc = lax.broadcasted_iota(jnp.int32, (tq, w), 1)
    rel = rr - cc
    rel_f = rel.astype(F32)
    alibi = [(-hd["slope"]) * rel_f for hd in heads]
    n_blk = heads[0]["unsel"].shape[0]
    erow = lax.broadcasted_iota(jnp.int32, (n_blk, w), 0)
    ecol = lax.broadcasted_iota(jnp.int32, (n_blk, w), 1)

    def scores(hd, c, bias):
        expand = jnp.where(erow == jnp.right_shift(c * w + ecol, hd["blk_shift"]), NEG, 0.0).astype(BF16)
        mask_bias = lax.dot_general(hd["unsel"], expand, (((0,), (0,)), ((), ())), preferred_element_type=F32)
        kc = hd["k_s"][pl.ds(pl.multiple_of(c * w, w), w), :]
        return _dot_nt(hd["qb"], kc) * scale + bias + mask_bias

    def first_pass(c, ms):
        out = []
        for hi, hd in enumerate(heads):
            s = scores(hd, c, alibi[hi] + (-hd["slope"]) * (t0 - c * w).astype(F32))
            strip_s[hi, c] = s
            out.append(_fold_lanes(jnp.maximum, ms[hi], s))
        return tuple(out)

    ms = lax.fori_loop(0, last, first_pass, tuple(jnp.full((tq, HEAD_DIM), NEG, F32) for _ in heads))
    dist = rel + (t0 - last * w)
    row_max = []
    for hi, hd in enumerate(heads):
        bias = jnp.where(dist >= 0, (-hd["slope"]) * dist.astype(F32), NEG)
        s = scores(hd, last, bias)
        strip_s[hi, last] = s
        row_max.append(jnp.max(_fold_lanes(jnp.maximum, ms[hi], s), axis=1, keepdims=True))

    def second_pass(c, carry):
        out = []
        for hi, hd in enumerate(heads):
            l_run, acc = carry[hi]
            p = jnp.exp(strip_s[hi, c] - row_max[hi])
            vc = hd["v_s"][pl.ds(pl.multiple_of(c * w, w), w), :]
            out.append((_fold_lanes(jnp.add, l_run, p), acc + _dot(p, vc)))
        return tuple(out)

    zero = jnp.zeros((tq, HEAD_DIM), F32)
    res = lax.fori_loop(0, last + 1, second_pass, tuple((zero, zero) for _ in heads))
    return [acc / jnp.sum(l_run, axis=1, keepdims=True) for l_run, acc in res]


def _head_cols(hh):
    return slice(hh * HEAD_DIM, (hh + 1) * HEAD_DIM)


def _moba_kernel(slope_ref, q_ref, k_ref, v_ref, g_ref, y_ref, kn_s, vb_s, kmean_s, strip_s):
    hp = pl.program_id(1)
    qi = pl.program_id(2)
    t = k_ref.shape[1]
    blk = MOBA_BLOCK
    nb = t // blk

    @pl.when(qi == 0)
    def _():
        for hh in range(HEADS_PER_STEP):
            kn = _rms(k_ref[0, :, _head_cols(hh)], g_ref[1:2, :])
            kn_s[hh] = kn.astype(kn_s.dtype)
            vb_s[hh] = v_ref[0, :, _head_cols(hh)].astype(vb_s.dtype)
            kmean_s[hh] = jnp.zeros(kmean_s.shape[1:], F32)
            for j in range(nb):
                kmean_s[hh, j:j + 1, :] = jnp.mean(kn[j * blk:(j + 1) * blk, :], axis=0, keepdims=True)

    rowb = lax.broadcasted_iota(jnp.int32, (nb, ATT_TQ), 0)
    heads = []
    for hh in range(HEADS_PER_STEP):
        qn = _rms(q_ref[0, :, _head_cols(hh)], g_ref[0:1, :])
        gate = jnp.where(rowb < qi, _dot_nt(kmean_s[hh], qn)[0:nb, :], NEG)
        sel = (_rank_rows(gate) < MOBA_TOPK) & (rowb < qi)
        unsel = jnp.where(sel | (rowb == qi), 0.0, 1.0)
        heads.append(dict(qb=qn.astype(MXU_DTYPE), k_s=kn_s.at[hh], v_s=vb_s.at[hh],
                          slope=slope_ref[hp * HEADS_PER_STEP + hh], blk_shift=MOBA_SHIFT,
                          unsel=_pad_rows(unsel, BF16_SUBLANES).astype(BF16)))
    outs = _block_masked_attention(heads, qi, strip_s)
    for hh in range(HEADS_PER_STEP):
        y_ref[0, :, _head_cols(hh)] = outs[hh].astype(y_ref.dtype)


def _moba(z3, qk_g, slopes):
    bsz, seq, _ = z3.shape
    assert MOBA_BLOCK == ATT_TQ and seq % STRIP_W == 0
    hps = HEADS_PER_STEP
    wide = hps * HEAD_DIM
    per_group = HEADS // hps
    kv = lambda group: pl.BlockSpec((1, seq, wide), lambda b, h, i: (b, 0, group * per_group + h))
    return pl.pallas_call(
        _moba_kernel,
        grid=(bsz, per_group, seq // ATT_TQ),
        in_specs=[pl.BlockSpec(memory_space=pltpu.SMEM),
                  pl.BlockSpec((1, ATT_TQ, wide), lambda b, h, i: (b, i, G_BQ * per_group + h)),
                  kv(G_BK), kv(G_BV),
                  pl.BlockSpec((2, HEAD_DIM), lambda b, h, i: (0, 0))],
        out_specs=pl.BlockSpec((1, ATT_TQ, wide), lambda b, h, i: (b, i, h)),
        out_shape=jax.ShapeDtypeStruct((bsz, seq, GROUP_W), BF16),
        scratch_shapes=[pltpu.VMEM((hps, seq, HEAD_DIM), MXU_DTYPE), pltpu.VMEM((hps, seq, HEAD_DIM), MXU_DTYPE),
                        pltpu.VMEM((hps, HEAD_DIM, HEAD_DIM), F32),
                        pltpu.VMEM((hps, seq // STRIP_W, ATT_TQ, STRIP_W), F32)],
        compiler_params=_cparams(("arbitrary", "arbitrary", "arbitrary")),
        name="moba",
    )(slopes, z3, z3, z3, qk_g)


def _gelu_tanh(x):
    return 0.5 * x * (1.0 + jnp.tanh(0.7978845608028654 * (x + 0.044715 * (x * x * x))))


def _nsa_kernel(slope_ref, q_ref, kc0_ref, kc1_ref, vc0_ref, vc1_ref, ks_ref, vs_ref, kw_ref, vw_ref, zg_ref,
                qg_ref, kg_ref, pe_ref, w1_ref, w2_ref, y_ref,
                kcmp_s, vcmp_s, ksn_s, vsb_s, kwn_s, vwb_s, strip_s):
    kc_refs, vc_refs = (kc0_ref, kc1_ref), (vc0_ref, vc1_ref)
    hp = pl.program_id(1)
    qi = pl.program_id(2)
    t = ks_ref.shape[1]
    tq = ATT_TQ
    nsub = t // NSA_CMP_STRIDE
    n_cmp = nsub - 1
    n_sel = t // NSA_SEL_BLOCK
    scale = HEAD_DIM ** -0.5

    @pl.when(qi == 0)
    def _():
        for hh in range(HEADS_PER_STEP):
            cols = _head_cols(hh)
            for cv, (src, dst) in enumerate(((kc_refs[hh], kcmp_s), (vc_refs[hh], vcmp_s))):
                acc_a = jnp.zeros((nsub, HEAD_DIM), F32)
                acc_b = jnp.zeros((nsub, HEAD_DIM), F32)
                for r in range(NSA_CMP_STRIDE):
                    zr = src[0, pl.ds(r, nsub, stride=NSA_CMP_STRIDE), :]
                    acc_a = acc_a + _dot(zr + pe_ref[cv, r:r + 1, :], w1_ref[cv, r])
                    rb = NSA_CMP_STRIDE + r
                    acc_b = acc_b + _dot(zr + pe_ref[cv, rb:rb + 1, :], w1_ref[cv, rb])
                hid = _gelu_tanh(acc_a + pltpu.roll(acc_b, nsub - 1, axis=0))
                cmp = _dot(hid, w2_ref[cv])
                if cv == 0:
                    cmp = _rms(cmp, kg_ref[0:1, :])
                dst[hh] = cmp.astype(dst.dtype)
            ksn_s[hh] = _rms(ks_ref[0, :, cols], kg_ref[1:2, :]).astype(ksn_s.dtype)
            vsb_s[hh] = vs_ref[0, :, cols].astype(vsb_s.dtype)
            kwn_s[hh] = _rms(kw_ref[0, :, cols], kg_ref[2:3, :]).astype(kwn_s.dtype)
            vwb_s[hh] = vw_ref[0, :, cols].astype(vwb_s.dtype)

    t0 = qi * tq
    rowi = lax.broadcasted_iota(jnp.int32, (tq, HEAD_DIM), 0)
    lane = lax.broadcasted_iota(jnp.int32, (tq, HEAD_DIM), 1)
    dist_c = (t0 + rowi) - (lane * NSA_CMP_STRIDE + (NSA_CMP_LEN - 1))
    ok_c = (dist_c >= 0) & (lane < n_cmp)
    dist_cf = dist_c.astype(F32)

    ob = lax.broadcasted_iota(jnp.int32, (HEAD_DIM, nsub), 0)
    oc = lax.broadcasted_iota(jnp.int32, (HEAD_DIM, nsub), 1)
    overlap_t = ((oc * NSA_CMP_STRIDE <= ob * NSA_SEL_BLOCK + (NSA_SEL_BLOCK - 1))
                 & (oc * NSA_CMP_STRIDE + (NSA_CMP_LEN - 1) >= ob * NSA_SEL_BLOCK)
                 & (oc < n_cmp) & (ob < n_sel)).astype(F32)
    rowb = lax.broadcasted_iota(jnp.int32, (n_sel, tq), 0)
    cur = jnp.right_shift(t0 + lax.broadcasted_iota(jnp.int32, (n_sel, tq), 1), NSA_SEL_SHIFT)

    win_w = NSA_WINDOW + tq
    k0 = jnp.maximum(t0 - NSA_WINDOW, 0)
    wr = lax.broadcasted_iota(jnp.int32, (tq, win_w), 0)
    wc = lax.broadcasted_iota(jnp.int32, (tq, win_w), 1)
    dist_w = (t0 - k0) + wr - wc
    ok_w = (dist_w >= 0) & (dist_w < NSA_WINDOW)
    dist_wf = dist_w.astype(F32)
    win_rows = pl.ds(pl.multiple_of(k0, tq), win_w)

    heads, o_cmp, o_win = [], [], []
    for hh in range(HEADS_PER_STEP):
        slope = slope_ref[hp * HEADS_PER_STEP + hh]
        qn = _rms(q_ref[0, :, _head_cols(hh)], qg_ref[...])
        qb = qn.astype(MXU_DTYPE)

        s_c = jnp.where(ok_c, _dot_nt(qb, kcmp_s[hh]) * scale - slope * dist_cf, NEG)
        m_c = jnp.max(s_c, axis=1, keepdims=True)
        e_c = jnp.where(ok_c, jnp.exp(s_c - m_c), 0.0)
        p_c = e_c / jnp.maximum(jnp.sum(e_c, axis=1, keepdims=True), 1e-30)
        o_cmp.append(_dot(p_c, vcmp_s[hh]))

        imp = _dot_split_nt(overlap_t, p_c)[0:n_sel, :]
        imp = jnp.where(rowb == cur, BIG, jnp.where(rowb < cur, imp, NEG))
        sel = (_rank_rows(imp) < NSA_SEL_TOPN) & (rowb <= cur)
        unsel = jnp.where(sel, 0.0, 1.0)
        heads.append(dict(qb=qb, k_s=ksn_s.at[hh], v_s=vsb_s.at[hh], slope=slope, blk_shift=NSA_SEL_SHIFT,
                          unsel=unsel.astype(BF16)))

        s_w = _dot_nt(qb, kwn_s[hh, win_rows, :]) * scale + jnp.where(ok_w, (-slope) * dist_wf, NEG)
        p_w = jnp.exp(s_w - jnp.max(s_w, axis=1, keepdims=True))
        o_win.append(_dot(p_w, vwb_s[hh, win_rows, :]) / jnp.sum(p_w, axis=1, keepdims=True))

    o_slc = _block_masked_attention(heads, qi, strip_s)

    zg = zg_ref[0]
    for hh in range(HEADS_PER_STEP):
        h = hp * HEADS_PER_STEP + hh
        g_cmp = _sigmoid(_lane_col(zg, GL_NG + h))
        g_slc = _sigmoid(_lane_col(zg, GL_NG + HEADS + h))
        g_win = _sigmoid(_lane_col(zg, GL_NG + 2 * HEADS + h))
        y_ref[0, :, _head_cols(hh)] = (g_cmp * o_cmp[hh] + g_slc * o_slc[hh] + g_win * o_win[hh]).astype(y_ref.dtype)


def _nsa(z3, zg3, q_g, k_g, cmp_pos, cmp_w1, cmp_w2, slopes):
    bsz, seq, _ = z3.shape
    assert seq // NSA_CMP_STRIDE == HEAD_DIM, "compressed blocks are laid out on the 128 lanes"
    assert seq % STRIP_W == 0 and seq >= NSA_WINDOW + ATT_TQ
    tq = ATT_TQ
    hps = HEADS_PER_STEP
    wide = hps * HEAD_DIM
    per_group = HEADS // hps
    kv = lambda group: pl.BlockSpec((1, seq, wide), lambda b, h, i: (b, 0, group * per_group + h))
    one = lambda group, hh: pl.BlockSpec((1, seq, HEAD_DIM), lambda b, h, i: (b, 0, group * HEADS + h * hps + hh))
    full = lambda shape: pl.BlockSpec(shape, lambda b, h, i: (0,) * len(shape))
    w1 = cmp_w1.reshape(2, NSA_CMP_LEN, HEAD_DIM, HEAD_DIM).astype(MXU_DTYPE)
    w2 = cmp_w2.astype(MXU_DTYPE)
    seq_buf = lambda: pltpu.VMEM((hps, seq, HEAD_DIM), MXU_DTYPE)
    cmp_buf = lambda: pltpu.VMEM((hps, HEAD_DIM, HEAD_DIM), MXU_DTYPE)
    return pl.pallas_call(
        _nsa_kernel,
        grid=(bsz, per_group, seq // tq),
        in_specs=[pl.BlockSpec(memory_space=pltpu.SMEM),
                  pl.BlockSpec((1, tq, wide), lambda b, h, i: (b, i, G_NQ * per_group + h)),
                  one(G_NKC, 0), one(G_NKC, 1), one(G_NVC, 0), one(G_NVC, 1),
                  kv(G_NKS), kv(G_NVS), kv(G_NKW), kv(G_NVW),
                  pl.BlockSpec((1, tq, GATE_W), lambda b, h, i: (b, i, 0)),
                  full((1, HEAD_DIM)), full((3, HEAD_DIM)), full((2, NSA_CMP_LEN, HEAD_DIM)),
                  full((2, NSA_CMP_LEN, HEAD_DIM, HEAD_DIM)), full((2, HEAD_DIM, HEAD_DIM))],
        out_specs=pl.BlockSpec((1, tq, wide), lambda b, h, i: (b, i, h)),
        out_shape=jax.ShapeDtypeStruct((bsz, seq, GROUP_W), BF16),
        scratch_shapes=[cmp_buf(), cmp_buf(), seq_buf(), seq_buf(), seq_buf(), seq_buf(),
                        pltpu.VMEM((hps, seq // STRIP_W, tq, STRIP_W), F32)],
        compiler_params=_cparams(("arbitrary", "arbitrary", "arbitrary")),
        name="nsa",
    )(slopes, z3, z3, z3, z3, z3, z3, z3, z3, z3, zg3, q_g.reshape(1, HEAD_DIM), k_g, cmp_pos, w1, w2)


def _out_proj_kernel(ym_ref, yb_ref, yr_ref, yn_ref, w_ref, x_ref, g_ref, o_ref):
    acc = jnp.dot(ym_ref[...], w_ref[0, 0:GROUP_W, :], preferred_element_type=F32)
    acc = acc + jnp.dot(yb_ref[...], w_ref[0, GROUP_W:2 * GROUP_W, :], preferred_element_type=F32)
    acc = acc + jnp.dot(yr_ref[...], w_ref[0, 2 * GROUP_W:3 * GROUP_W, :], preferred_element_type=F32)
    acc = acc + jnp.dot(yn_ref[...], w_ref[0, 3 * GROUP_W:4 * GROUP_W, :], preferred_element_type=F32)
    o_ref[...] = x_ref[...] + g_ref[0] * acc


def _out_proj(ys, w_out, layer, x2d, mod3, seq):
    n, d = x2d.shape
    tm, tn = 1024, 1024
    per_b = seq // tm
    y_spec = pl.BlockSpec((tm, GROUP_W), lambda i, j: (i, 0))
    return pl.pallas_call(
        _out_proj_kernel,
        grid=(n // tm, d // tn),
        in_specs=[y_spec, y_spec, y_spec, y_spec,
                  pl.BlockSpec((1, 4 * GROUP_W, tn), lambda i, j: (layer, 0, j)),
                  pl.BlockSpec((tm, tn), lambda i, j: (i, j)),
                  pl.BlockSpec((1, 1, tn), lambda i, j: (i // per_b, 0, 2 * (d // tn) + j))],
        out_specs=pl.BlockSpec((tm, tn), lambda i, j: (i, j)),
        out_shape=jax.ShapeDtypeStruct((n, d), F32),
        compiler_params=_cparams(("arbitrary", "arbitrary")),
        name="out_proj",
    )(*[y.reshape(n, GROUP_W) for y in ys], w_out, x2d, mod3)


SLAB_E0, SLAB_E1, SLAB_R0, SLAB_R1, SLAB_G0, SLAB_G1 = 0, 1, 2, 3, 4, 5


def _route_kernel(x_ref, g_ref, sc_ref, sh_ref, wr_ref, br_ref, h_ref, slab_ref, cnt_ref, carry_s):
    tm = x_ref.shape[0]

    @pl.when(pl.program_id(0) == 0)
    def _():
        carry_s[...] = jnp.zeros(carry_s.shape, F32)

    hmod = _rms(x_ref[...], g_ref[...]) * (1.0 + sc_ref[0]) + sh_ref[0]
    h_ref[...] = hmod
    logits = _dot(hmod, wr_ref[...]) + br_ref[...]
    lane = lax.broadcasted_iota(jnp.int32, logits.shape, 1).astype(F32)
    far = 4.0 * GATE_W

    in_g = lane < N_GROUPS
    lg = jnp.where(in_g, logits, NEG)
    g_max = jnp.max(lg, axis=1, keepdims=True)
    grp = jnp.min(jnp.where(in_g & (lg == g_max), lane, far), axis=1, keepdims=True)
    p_grp = 1.0 / jnp.sum(jnp.where(in_g, jnp.exp(lg - g_max), 0.0), axis=1, keepdims=True)

    lo = N_GROUPS + grp * EXPERTS_PER_GROUP
    in_e = (lane >= lo) & (lane < lo + EXPERTS_PER_GROUP)
    le = jnp.where(in_e, logits, NEG)
    e_max = jnp.max(le, axis=1, keepdims=True)
    ee = jnp.where(in_e, jnp.exp(le - e_max), 0.0)
    pe = jnp.where(in_e, ee / jnp.sum(ee, axis=1, keepdims=True), -1.0)
    p1 = jnp.max(pe, axis=1, keepdims=True)
    i1 = jnp.min(jnp.where(pe == p1, lane, far), axis=1, keepdims=True)
    pe2 = jnp.where(lane == i1, -1.0, pe)
    p2 = jnp.max(pe2, axis=1, keepdims=True)
    i2 = jnp.min(jnp.where(pe2 == p2, lane, far), axis=1, keepdims=True)
    e0 = i1 - N_GROUPS
    e1 = i2 - N_GROUPS
    g0 = p_grp * (p1 / (p1 + p2))
    g1 = p_grp * (p2 / (p1 + p2))

    onehot = ((lane == e0) | (lane == e1)).astype(BF16)
    rr = lax.broadcasted_iota(jnp.int32, (tm, tm), 0)
    cc = lax.broadcasted_iota(jnp.int32, (tm, tm), 1)
    before = jnp.dot((cc < rr).astype(BF16), onehot, preferred_element_type=F32) + carry_s[...]
    r0 = jnp.sum(jnp.where(lane == e0, before, 0.0), axis=1, keepdims=True)
    r1 = jnp.sum(jnp.where(lane == e1, before, 0.0), axis=1, keepdims=True)
    carry_s[...] = carry_s[...] + jnp.sum(onehot.astype(F32), axis=0, keepdims=True)
    cnt_ref[...] = carry_s[...]

    slab = jnp.where(lane == SLAB_E0, e0.astype(F32), 0.0)
    slab = jnp.where(lane == SLAB_E1, e1.astype(F32), slab)
    slab = jnp.where(lane == SLAB_R0, r0, slab)
    slab = jnp.where(lane == SLAB_R1, r1, slab)
    slab = jnp.where(lane == SLAB_G0, g0, slab)
    slab = jnp.where(lane == SLAB_G1, g1, slab)
    slab_ref[...] = slab


def _route(x2d, norm_g, mod3, w_router, b_router, seq):
    n, d = x2d.shape
    tm = 512
    per_b = seq // tm
    return pl.pallas_call(
        _route_kernel,
        grid=(n // tm,),
        in_specs=[pl.BlockSpec((tm, d), lambda i: (i, 0)),
                  pl.BlockSpec((1, d), lambda i: (0, 0)),
                  pl.BlockSpec((1, 1, d), lambda i: (i // per_b, 0, 4)),
                  pl.BlockSpec((1, 1, d), lambda i: (i // per_b, 0, 3)),
                  pl.BlockSpec((d, GATE_W), lambda i: (0, 0)),
                  pl.BlockSpec((1, GATE_W), lambda i: (0, 0))],
        out_specs=[pl.BlockSpec((tm, d), lambda i: (i, 0)),
                   pl.BlockSpec((tm, GATE_W), lambda i: (i, 0)),
                   pl.BlockSpec((1, GATE_W), lambda i: (0, 0))],
        out_shape=[jax.ShapeDtypeStruct((n, d), F32),
                   jax.ShapeDtypeStruct((n, GATE_W), F32),
                   jax.ShapeDtypeStruct((1, GATE_W), F32)],
        scratch_shapes=[pltpu.VMEM((1, GATE_W), F32)],
        compiler_params=_cparams(("arbitrary",)),
        name="route",
    )(x2d, norm_g.reshape(1, d), mod3, mod3, w_router, b_router)


def _invert_kernel(dest_ref, pad_lo_ref, pad_hi_ref, inv_ref):
    def clear(i, _):
        inv_ref[i] = 0
        return 0

    def clear_segment(g, _):
        lax.fori_loop(pad_lo_ref[g], pad_hi_ref[g], clear, 0)
        return 0

    def put(a, _):
        inv_ref[dest_ref[a]] = jnp.right_shift(a, 1)
        return 0

    lax.fori_loop(0, pad_lo_ref.shape[0], clear_segment, 0)
    lax.fori_loop(0, dest_ref.shape[0], put, 0, unroll=32)


def _invert(dest_flat, pad_lo, pad_hi, cap):
    smem = pl.BlockSpec(memory_space=pltpu.SMEM)
    return pl.pallas_call(
        _invert_kernel,
        in_specs=[smem, smem, smem],
        out_specs=smem,
        out_shape=jax.ShapeDtypeStruct((cap,), jnp.int32),
        name="invert",
    )(dest_flat, pad_lo, pad_hi)


def _start_row_gather(idx_ref, src_hbm, dst, sem, n_rows, first=0):
    for r in range(first, n_rows):
        pltpu.make_async_copy(src_hbm.at[pl.ds(idx_ref[r], 1)], dst.at[pl.ds(r, 1)], sem).start(priority=r % 2)


def _wait_row_gather(src_hbm, dst, sem, n_rows):
    pltpu.make_async_copy(src_hbm.at[pl.ds(0, n_rows)], dst, sem).wait()


def _expert_kernel(blk_e_ref, n_used_ref, run_slot_ref, next_e_ref, next_ok_ref, inv_ref, h_hbm,
                   w1_hbm, w3_hbm, w2_hbm, y_ref,
                   x_s, w1_f, w3_f, w2_f, w1_s, w3_s, w2_s, sem, wsem, *, layer):
    s = pl.program_id(0)
    rows = EXPERT_ROWS
    n_used = n_used_ref[0]
    ahead = EXPERT_GATHER_AHEAD
    slot = lax.rem(s, ahead + 1)
    cslot = lax.rem(s + 1, ahead + 1)

    blk = s - ahead
    prev = jnp.maximum(blk - 1, 0)
    gather = s < n_used
    compute = (s >= ahead) & (blk < n_used)

    def start_gather(first, stop):
        _start_row_gather(inv_ref, h_hbm, x_s.at[slot], sem.at[slot], stop, first)

    def weight_copies(e, wslot):
        return (pltpu.make_async_copy(w1_hbm.at[layer, e], w1_f.at[wslot], wsem.at[wslot, 0]),
                pltpu.make_async_copy(w3_hbm.at[layer, e], w3_f.at[wslot], wsem.at[wslot, 1]),
                pltpu.make_async_copy(w2_hbm.at[layer, e], w2_f.at[wslot], wsem.at[wslot, 2]))

    def expert_block(gather_too):
        @pl.when((blk == 0) | (blk_e_ref[blk] != blk_e_ref[prev]))
        def _():
            wslot = run_slot_ref[blk]
            for cp in weight_copies(blk_e_ref[blk], wslot):
                cp.wait()

            @pl.when(next_ok_ref[blk] == 1)
            def _():
                for cp in weight_copies(next_e_ref[blk], 1 - wslot):
                    cp.start()

            w1_s[...] = w1_f[wslot].astype(w1_s.dtype)
            w3_s[...] = w3_f[wslot].astype(w3_s.dtype)
            w2_s[...] = w2_f[wslot].astype(w2_s.dtype)

        cuts = (0, rows // 4, rows // 2, rows) if gather_too else (0, 0, 0, 0)
        _wait_row_gather(h_hbm, x_s.at[cslot], sem.at[cslot], rows)
        start_gather(cuts[0], cuts[1])
        x = x_s[cslot].astype(MXU_DTYPE)
        a = jnp.dot(x, w1_s[...], preferred_element_type=F32)
        start_gather(cuts[1], cuts[2])
        b = jnp.dot(x, w3_s[...], preferred_element_type=F32)
        start_gather(cuts[2], cuts[3])
        y_ref[...] = _dot(a * _sigmoid(a) * b, w2_s[...])

    @pl.when(gather & compute)
    def _():
        expert_block(True)

    @pl.when(gather & jnp.logical_not(compute))
    def _():
        @pl.when(s == 0)
        def _():
            for cp in weight_copies(blk_e_ref[0], run_slot_ref[0]):
                cp.start()

        start_gather(0, rows)

    @pl.when(compute & jnp.logical_not(gather))
    def _():
        expert_block(False)

    @pl.when((s >= ahead) & (blk >= n_used))
    def _():
        y_ref[...] = jnp.zeros(y_ref.shape, y_ref.dtype)


def _experts(h2, inv, blk_e, n_used, w1, w3, w2, layer):
    d = D_MODEL
    rows = EXPERT_ROWS
    n_blocks = inv.shape[0] // rows
    idx = jnp.arange(n_blocks, dtype=jnp.int32)
    run_start = jnp.concatenate([jnp.ones((1,), jnp.int32), (blk_e[1:] != blk_e[:-1]).astype(jnp.int32)])
    run_slot = (jnp.cumsum(run_start) - 1) % 2
    later_run = (idx[None, :] > idx[:, None]) & (blk_e[None, :] != blk_e[:, None])
    next_start = jnp.min(jnp.where(later_run, idx[None, :], n_blocks), axis=1)
    next_e = blk_e[jnp.minimum(next_start, n_blocks - 1)]
    next_ok = (next_start < n_used[0]).astype(jnp.int32)

    ahead = EXPERT_GATHER_AHEAD
    done = lambda s: jnp.maximum(s - ahead, 0)
    hbm = pl.BlockSpec(memory_space=pl.ANY)
    return pl.pallas_call(
        functools.partial(_expert_kernel, layer=layer),
        grid_spec=pltpu.PrefetchScalarGridSpec(
            num_scalar_prefetch=5,
            grid=(n_blocks + ahead,),
            in_specs=[pl.BlockSpec((rows,), lambda s, *_: (jnp.minimum(s, n_blocks - 1),), memory_space=pltpu.SMEM),
                      hbm, hbm, hbm, hbm],
            out_specs=pl.BlockSpec((rows, d), lambda s, *_: (done(s), 0)),
            scratch_shapes=[pltpu.VMEM((ahead + 1, rows, d), F32),
                            pltpu.VMEM((2, d, D_EXPERT), F32), pltpu.VMEM((2, d, D_EXPERT), F32),
                            pltpu.VMEM((2, D_EXPERT, d), F32),
                            pltpu.VMEM((d, D_EXPERT), MXU_DTYPE), pltpu.VMEM((d, D_EXPERT), MXU_DTYPE),
                            pltpu.VMEM((D_EXPERT, d), MXU_DTYPE),
                            pltpu.SemaphoreType.DMA((ahead + 1,)), pltpu.SemaphoreType.DMA((2, 3))]),
        out_shape=jax.ShapeDtypeStruct((n_blocks * rows, d), F32),
        compiler_params=_cparams(("arbitrary",)),
        name="experts",
    )(blk_e, n_used, run_slot.astype(jnp.int32), next_e.astype(jnp.int32), next_ok, inv, h2, w1, w3, w2)


def _combine_kernel(d0_ref, d1_ref, x_ref, g_ref, slab_ref, yb_hbm, o_ref, rows_s, sem):
    s = pl.program_id(0)
    n_tiles = pl.num_programs(0) - 1
    tm = x_ref.shape[0]
    slot = lax.rem(s, 2)

    @pl.when(s < n_tiles)
    def _():
        _start_row_gather(d0_ref, yb_hbm, rows_s.at[slot, 0], sem.at[slot], tm)
        _start_row_gather(d1_ref, yb_hbm, rows_s.at[slot, 1], sem.at[slot], tm)

    @pl.when(s >= 1)
    def _():
        _wait_row_gather(yb_hbm, rows_s.at[1 - slot, 0], sem.at[1 - slot], tm)
        _wait_row_gather(yb_hbm, rows_s.at[1 - slot, 1], sem.at[1 - slot], tm)
        route = slab_ref[...]
        g0 = route[:, SLAB_G0:SLAB_G0 + 1]
        g1 = route[:, SLAB_G1:SLAB_G1 + 1]
        o_ref[...] = x_ref[...] + g_ref[0] * (g0 * rows_s[1 - slot, 0] + g1 * rows_s[1 - slot, 1])


def _combine(x2d, mod3, slab, dest0, dest1, yb, seq):
    n, d = x2d.shape
    tm = 256
    per_b = seq // tm
    n_tiles = n // tm
    done = lambda s: jnp.maximum(s - 1, 0)
    idx_spec = pl.BlockSpec((tm,), lambda s: (jnp.minimum(s, n_tiles - 1),), memory_space=pltpu.SMEM)
    return pl.pallas_call(
        _combine_kernel,
        grid=(n_tiles + 1,),
        in_specs=[idx_spec, idx_spec,
                  pl.BlockSpec((tm, d), lambda s: (done(s), 0)),
                  pl.BlockSpec((1, 1, d), lambda s: (done(s) // per_b, 0, 5)),
                  pl.BlockSpec((tm, GATE_W), lambda s: (done(s), 0)),
                  pl.BlockSpec(memory_space=pl.ANY)],
        out_specs=pl.BlockSpec((tm, d), lambda s: (done(s), 0)),
        out_shape=jax.ShapeDtypeStruct((n, d), F32),
        scratch_shapes=[pltpu.VMEM((2, 2, tm, d), F32), pltpu.SemaphoreType.DMA((2,))],
        compiler_params=_cparams(("arbitrary",)),
        name="combine",
    )(dest0, dest1, x2d, mod3, slab, yb)


_OFF_MI = 4 * GROUP_W
_OFF_BQ = _OFF_MI + 2 * HEADS
_OFF_NG = _OFF_BQ + 14 * GROUP_W


def _pack_w_kernel(w_ref, o_ref, og_ref):
    gap = _OFF_BQ - _OFF_MI
    o_ref[0, :, 0:_OFF_MI] = w_ref[0, :, 0:_OFF_MI].astype(o_ref.dtype)
    tail = w_ref[0, :, _OFF_MI:]
    width = tail.shape[1]
    o_ref[0, :, _OFF_MI:] = pltpu.roll(tail, width - gap, axis=1)[:, 0:D_WIDE - _OFF_MI].astype(o_ref.dtype)
    lane = lax.broadcasted_iota(jnp.int32, (w_ref.shape[1], GATE_W), 1)
    first = w_ref[0, :, _OFF_MI:_OFF_MI + GATE_W]
    ragged = w_ref[0, :, _OFF_NG - GL_NG:_OFF_NG - GL_NG + GATE_W]
    gate = jnp.where(lane < GL_NG, first, jnp.where(lane < GL_NG + 3 * HEADS, ragged, 0.0))
    og_ref[0] = gate.astype(og_ref.dtype)


def _pack_w_in(w_in):
    depth, d, d_in = w_in.shape
    tk = 128
    lanes_in = -(-d_in // GATE_W) * GATE_W
    assert (_OFF_NG - GL_NG) % GATE_W == 0 and _OFF_NG - GL_NG + GATE_W == lanes_in
    return pl.pallas_call(
        _pack_w_kernel,
        grid=(depth, d // tk),
        in_specs=[pl.BlockSpec((1, tk, lanes_in), lambda l, i: (l, i, 0))],
        out_specs=[pl.BlockSpec((1, tk, D_WIDE), lambda l, i: (l, i, 0)),
                   pl.BlockSpec((1, tk, GATE_W), lambda l, i: (l, i, 0))],
        out_shape=[jax.ShapeDtypeStruct((depth, d, D_WIDE), MXU_DTYPE),
                   jax.ShapeDtypeStruct((depth, d, GATE_W), MXU_DTYPE)],
        compiler_params=_cparams(("arbitrary", "arbitrary")),
        name="pack_w_in",
    )(w_in)


def _moe(x2d, norm_g, mod3, wg, bg, we, be, w1, w3, w2, layer, seq):
    n, d = x2d.shape
    n_route = N_GROUPS + N_EXPERTS
    w_router = jnp.concatenate([wg, we, jnp.zeros((d, GATE_W - n_route), wg.dtype)], axis=1).astype(MXU_DTYPE)
    b_router = jnp.concatenate([bg, be, jnp.zeros((GATE_W - n_route,), bg.dtype)]).reshape(1, GATE_W)
    h2, slab, cnt = _route(x2d, norm_g, mod3, w_router, b_router, seq)

    rows = EXPERT_ROWS
    counts = cnt[0, :N_EXPERTS].astype(jnp.int32)
    pcounts = (counts + rows - 1) // rows * rows
    pends = jnp.cumsum(pcounts)
    pstarts = pends - pcounts
    eid = slab[:, SLAB_E0:SLAB_E1 + 1].astype(jnp.int32)
    rank = slab[:, SLAB_R0:SLAB_R1 + 1].astype(jnp.int32)
    dest = pstarts[eid] + rank
    n_blocks = -(-2 * n // rows) + N_EXPERTS
    blk_row0 = jnp.arange(n_blocks, dtype=jnp.int32) * rows
    blk_e = jnp.minimum(jnp.sum((pends[None, :] <= blk_row0[:, None]).astype(jnp.int32), axis=1), N_EXPERTS - 1)
    n_used = (pends[-1:] // rows).astype(jnp.int32)

    cap = n_blocks * rows
    pad_lo = jnp.concatenate([pstarts + counts, pends[-1:]]).astype(jnp.int32)
    pad_hi = jnp.concatenate([pends, jnp.full((1,), cap, jnp.int32)]).astype(jnp.int32)
    inv = _invert(dest.reshape(-1), pad_lo, pad_hi, cap)
    yb = _experts(h2, inv, blk_e, n_used, w1, w3, w2, layer)
    return _combine(x2d, mod3, slab, dest[:, 0], dest[:, 1], yb, seq)


def _layer(x2d, mod, bsz, seq, layer, norm1_g, norm2_g, w_wide, w_gate, mlstm_gate_b, mlstm_conv_w, mlstm_out_g,
           moba_qk_g, ret_out_g, nsa_q_g, nsa_k_g, nsa_cmp_pos, nsa_cmp_w1, nsa_cmp_w2, w_out, router_g_w,
           router_g_b, router_e_w, router_e_b, exp_w1, exp_w3, exp_w2, slopes, log_gamma):
    n, d = x2d.shape
    mod3 = mod.reshape(bsz, 1, 6 * d)
    z, zg = _norm_in_proj(x2d, norm1_g, mod3, w_wide, layer, w_gate, seq)
    z3 = z.reshape(bsz, seq, D_WIDE)
    zg3 = zg.reshape(bsz, seq, GATE_W)
    y_m = _mlstm(z3, zg3, mlstm_gate_b, mlstm_conv_w, mlstm_out_g)
    y_b = _moba(z3, moba_qk_g, slopes[0::2])
    y_r = _retention(z3, log_gamma, ret_out_g)
    y_n = _nsa(z3, zg3, nsa_q_g, nsa_k_g, nsa_cmp_pos, nsa_cmp_w1, nsa_cmp_w2, slopes[1::2])
    x2d = _out_proj((y_m, y_b, y_r, y_n), w_out, layer, x2d, mod3, seq)
    return _moe(x2d, norm2_g, mod3, router_g_w, router_g_b, router_e_w, router_e_b, exp_w1, exp_w3, exp_w2,
                layer, seq)


def kernel(x, c, norm1_g, norm2_g, ada_w, ada_b, w_in, mlstm_gate_b, mlstm_conv_w, mlstm_out_g, moba_qk_g,
           ret_out_g, nsa_q_g, nsa_k_g, nsa_cmp_pos, nsa_cmp_w1, nsa_cmp_w2, w_out, router_g_w, router_g_b,
           router_e_w, router_e_b, exp_w1, exp_w3, exp_w2):
    bsz, seq, d = x.shape
    depth = ada_w.shape[0]
    n_softmax_heads = 2 * HEADS
    slopes = jnp.exp2(-8.0 * jnp.arange(1, n_softmax_heads + 1, dtype=F32) / n_softmax_heads)
    log_gamma = jnp.log(1.0 - jnp.exp2(-5.0 - jnp.arange(HEADS, dtype=F32)))
    mod = _ada_mod(c, ada_w, ada_b)
    w_wide, w_gate = _pack_w_in(w_in)
    w_out_b = w_out.astype(MXU_DTYPE)
    x2d = x.reshape(bsz * seq, d)
    for l in range(depth):
        x2d = _layer(x2d, mod[l], bsz, seq, l, norm1_g[l], norm2_g[l], w_wide, w_gate, mlstm_gate_b[l],
                     mlstm_conv_w[l], mlstm_out_g[l], moba_qk_g[l], ret_out_g[l], nsa_q_g[l], nsa_k_g[l],
                     nsa_cmp_pos[l], nsa_cmp_w1[l], nsa_cmp_w2[l], w_out_b, router_g_w[l], router_g_b[l],
                     router_e_w[l], router_e_b[l], exp_w1, exp_w3, exp_w2, slopes, log_gamma)
    return x2d.reshape(bsz, seq, d)
```

```python
import functools

import jax
import jax.numpy as jnp
from jax import lax
from jax.experimental import pallas as pl
from jax.experimental.pallas import tpu as pltpu

F32 = jnp.float32
BF16 = jnp.bfloat16
MXU_DTYPE = jnp.bfloat16

D_MODEL = 2048
HEAD_DIM = 128
HEADS = 4
GROUP_W = HEADS * HEAD_DIM
N_WIDE_GROUPS = 18
D_WIDE = N_WIDE_GROUPS * GROUP_W
GATE_W = 128

MLSTM_CHUNK = 64
MLSTM_CONV = 4
MOBA_BLOCK = 256
MOBA_SHIFT = 8
MOBA_TOPK = 3
RET_CHUNK = 128
NSA_CMP_LEN = 32
NSA_CMP_STRIDE = 16
NSA_SEL_BLOCK = 64
NSA_SEL_SHIFT = 6
NSA_SEL_TOPN = 4
NSA_WINDOW = 512
N_GROUPS = 4
EXPERTS_PER_GROUP = 8
N_EXPERTS = N_GROUPS * EXPERTS_PER_GROUP
D_EXPERT = 512
EXPERT_ROWS = 256
EXPERT_GATHER_AHEAD = 2

NORM_EPS = 1e-6
NEG = -1e30
BIG = 1e9

G_MQ, G_MK, G_MV, G_MO = 0, 1, 2, 3
G_BQ, G_BK, G_BV = 4, 5, 6
G_RQ, G_RK, G_RV, G_RG = 7, 8, 9, 10
G_NQ, G_NKC, G_NVC, G_NKS, G_NVS, G_NKW, G_NVW = 11, 12, 13, 14, 15, 16, 17
GL_MI, GL_MF, GL_NG = 0, 4, 8

VMEM_LIMIT = 56 * 1024 * 1024


def _cparams(sem):
    return pltpu.CompilerParams(dimension_semantics=sem, vmem_limit_bytes=VMEM_LIMIT)


def _dot(a, b):
    return jnp.dot(a.astype(MXU_DTYPE), b.astype(MXU_DTYPE), preferred_element_type=F32)


def _dot_nt(a, b):
    return lax.dot_general(a.astype(MXU_DTYPE), b.astype(MXU_DTYPE), (((1,), (1,)), ((), ())),
                           preferred_element_type=F32)


def _dot_tn(a, b):
    return lax.dot_general(a.astype(MXU_DTYPE), b.astype(MXU_DTYPE), (((0,), (0,)), ((), ())),
                           preferred_element_type=F32)


def _dot_split_nt(b01, a):
    a_hi = a.astype(MXU_DTYPE)
    r1 = a - a_hi.astype(F32)
    a_mid = r1.astype(MXU_DTYPE)
    a_lo = (r1 - a_mid.astype(F32)).astype(MXU_DTYPE)
    return _dot_nt(b01, a_hi) + _dot_nt(b01, a_mid) + _dot_nt(b01, a_lo)


def _rms(x, g):
    return x * lax.rsqrt(jnp.mean(x * x, axis=-1, keepdims=True) + NORM_EPS) * g


def _sigmoid(x):
    return jax.nn.sigmoid(x)


def _lane_col(x, idx):
    lane = lax.broadcasted_iota(jnp.int32, x.shape, 1)
    return jnp.sum(jnp.where(lane == idx, x, 0.0), axis=1, keepdims=True)


def _ada_kernel(c_ref, w_ref, b_ref, o_ref):
    c = c_ref[...]
    o_ref[0] = _dot(c * _sigmoid(c), w_ref[0]) + b_ref[0]


def _ada_mod(c, ada_w, ada_b):
    depth, d, n6 = ada_w.shape
    b = c.shape[0]
    tn = 1024
    return pl.pallas_call(
        _ada_kernel,
        grid=(depth, n6 // tn),
        in_specs=[pl.BlockSpec((b, d), lambda l, j: (0, 0)),
                  pl.BlockSpec((1, d, tn), lambda l, j: (l, 0, j)),
                  pl.BlockSpec((1, 1, tn), lambda l, j: (l, 0, j))],
        out_specs=pl.BlockSpec((1, b, tn), lambda l, j: (l, 0, j)),
        out_shape=jax.ShapeDtypeStruct((depth, b, n6), F32),
        compiler_params=_cparams(("arbitrary", "arbitrary")),
        name="ada_mod",
    )(c, ada_w, ada_b.reshape(depth, 1, n6))


def _norm_in_kernel(x_ref, g_ref, sc_ref, sh_ref, w_ref, ws_ref, z_ref, zg_ref, h_s):
    @pl.when(pl.program_id(1) == 0)
    def _():
        h = _rms(x_ref[...], g_ref[...]) * (1.0 + sc_ref[0]) + sh_ref[0]
        hb = h.astype(MXU_DTYPE)
        h_s[...] = hb
        zg_ref[...] = jnp.dot(hb, ws_ref[0], preferred_element_type=F32)

    z_ref[...] = jnp.dot(h_s[...], w_ref[0], preferred_element_type=F32)


def _norm_in_proj(x2d, norm_g, mod3, w_wide, layer, w_gate, seq):
    n, d = x2d.shape
    tm, tn = 1024, 1024
    per_b = seq // tm
    return pl.pallas_call(
        _norm_in_kernel,
        grid=(n // tm, D_WIDE // tn),
        in_specs=[pl.BlockSpec((tm, d), lambda i, j: (i, 0)),
                  pl.BlockSpec((1, d), lambda i, j: (0, 0)),
                  pl.BlockSpec((1, 1, d), lambda i, j: (i // per_b, 0, 1)),
                  pl.BlockSpec((1, 1, d), lambda i, j: (i // per_b, 0, 0)),
                  pl.BlockSpec((1, d, tn), lambda i, j: (layer, 0, j)),
                  pl.BlockSpec((1, d, GATE_W), lambda i, j: (layer, 0, 0))],
        out_specs=[pl.BlockSpec((tm, tn), lambda i, j: (i, j)),
                   pl.BlockSpec((tm, GATE_W), lambda i, j: (i, 0))],
        out_shape=[jax.ShapeDtypeStruct((n, D_WIDE), F32),
                   jax.ShapeDtypeStruct((n, GATE_W), F32)],
        scratch_shapes=[pltpu.VMEM((tm, d), MXU_DTYPE)],
        compiler_params=_cparams(("arbitrary", "arbitrary")),
        name="norm_in_proj",
    )(x2d, norm_g.reshape(1, d), mod3, mod3, w_wide, w_gate)


REC_HEADS_PER_STEP = 2


def _mlstm_kernel(gb_ref, q_ref, k_ref, v_ref, o_ref, zg_ref, cwq_ref, cwk_ref, og_ref, y_ref,
                  pad_s, qs_s, ks_s, ic_s, fc_s):
    hp = pl.program_id(1)
    t = q_ref.shape[1]
    cl = MLSTM_CHUNK
    nc = t // cl
    hps = REC_HEADS_PER_STEP

    def conv_silu(src_ref, cw_ref, cols, dst_s, scale):
        pad_s[8:8 + t, :] = src_ref[0, :, cols]
        off = 8 - (MLSTM_CONV - 1)
        tile = 128
        for r0 in range(0, t, tile):
            acc = cw_ref[0:1, cols] * pad_s[r0 + off:r0 + off + tile, :]
            for j in range(1, MLSTM_CONV):
                acc = acc + cw_ref[j:j + 1, cols] * pad_s[r0 + off + j:r0 + off + j + tile, :]
            dst_s[r0:r0 + tile, :] = acc * _sigmoid(acc) * scale

    pad_s[0:8, :] = jnp.zeros((8, HEAD_DIM), F32)
    zg = zg_ref[0]
    for hh in range(hps):
        h = hp * hps + hh
        conv_silu(q_ref, cwq_ref, _head_cols(hh), qs_s.at[hh], HEAD_DIM ** -0.5)
        conv_silu(k_ref, cwk_ref, _head_cols(hh), ks_s.at[hh], 1.0)
        ic_s[hh] = _lane_col(zg, GL_MI + h) + gb_ref[0, h]
        f_pre = _lane_col(zg, GL_MF + h) + gb_ref[1, h]
        fc_s[hh] = jnp.minimum(f_pre, 0.0) - jnp.log1p(jnp.exp(-jnp.abs(f_pre)))

    rr = lax.broadcasted_iota(jnp.int32, (cl, cl), 0)
    cc = lax.broadcasted_iota(jnp.int32, (cl, cl), 1)
    eye = rr == cc
    causal = cc <= rr
    og = og_ref[...]

    def head_step(hh, c, carry):
        c_st, n_st, m_st = carry
        sl = pl.ds(pl.multiple_of(c * cl, cl), cl)
        cols = _head_cols(hh)
        qc = qs_s[hh, sl, :]
        kc = ks_s[hh, sl, :]
        vc = v_ref[0, sl, cols]
        i_col = ic_s[hh, sl, :]
        f_col = fc_s[hh, sl, :]
        f_row = jnp.sum(jnp.where(eye, f_col, 0.0), axis=0, keepdims=True)
        i_row = jnp.sum(jnp.where(eye, i_col, 0.0), axis=0, keepdims=True)
        a_col = jnp.sum(jnp.where(causal, f_row, 0.0), axis=1, keepdims=True)
        a_row = jnp.sum(jnp.where(rr <= cc, f_col, 0.0), axis=0, keepdims=True)
        log_d = jnp.where(causal, a_col - a_row + i_row, NEG)
        m_inter = a_col + m_st
        m_row = jnp.maximum(m_inter, jnp.max(log_d, axis=1, keepdims=True))
        s = _dot_nt(qc, kc) * jnp.exp(log_d - m_row)
        w_inter = jnp.exp(m_inter - m_row)
        num = _dot(s, vc) + w_inter * _dot(qc, c_st)
        den = jnp.sum(s, axis=1, keepdims=True) + w_inter * jnp.sum(qc * n_st, axis=1, keepdims=True)
        h_out = num / jnp.maximum(jnp.abs(den), jnp.exp(-m_row))
        a_last = jnp.sum(f_col, axis=0, keepdims=True)
        w_log = a_last - a_col + i_col
        m_new = jnp.maximum(a_last + m_st, jnp.max(w_log, axis=0, keepdims=True))
        w = jnp.exp(w_log - m_new)
        decay = jnp.exp(a_last + m_st - m_new)
        kw = kc * w
        c_new = decay * c_st + _dot_tn(kw, vc)
        n_new = decay * n_st + jnp.sum(kw, axis=0, keepdims=True)
        y = _rms(h_out, og) * _sigmoid(o_ref[0, sl, cols])
        y_ref[0, sl, cols] = y.astype(y_ref.dtype)
        return c_new, n_new, m_new

    def body(c, carry):
        return tuple(head_step(hh, c, carry[hh]) for hh in range(hps))

    carry0 = (jnp.zeros((HEAD_DIM, HEAD_DIM), F32), jnp.zeros((1, HEAD_DIM), F32), jnp.zeros((1, 1), F32))
    lax.fori_loop(0, nc, body, (carry0,) * hps)


def _col_spec(seq, group, hps):
    return pl.BlockSpec((1, seq, hps * HEAD_DIM), lambda b, h: (b, 0, group * (HEADS // hps) + h))


def _mlstm(z3, zg3, gate_b, conv_w, out_g):
    bsz, seq, _ = z3.shape
    hps = REC_HEADS_PER_STEP
    wide = hps * HEAD_DIM
    per_group = HEADS // hps
    smem = pl.BlockSpec(memory_space=pltpu.SMEM)
    return pl.pallas_call(
        _mlstm_kernel,
        grid=(bsz, per_group),
        in_specs=[smem,
                  _col_spec(seq, G_MQ, hps), _col_spec(seq, G_MK, hps), _col_spec(seq, G_MV, hps),
                  _col_spec(seq, G_MO, hps),
                  pl.BlockSpec((1, seq, GATE_W), lambda b, h: (b, 0, 0)),
                  pl.BlockSpec((MLSTM_CONV, wide), lambda b, h: (0, h)),
                  pl.BlockSpec((MLSTM_CONV, wide), lambda b, h: (0, per_group + h)),
                  pl.BlockSpec((1, HEAD_DIM), lambda b, h: (0, 0))],
        out_specs=pl.BlockSpec((1, seq, wide), lambda b, h: (b, 0, h)),
        out_shape=jax.ShapeDtypeStruct((bsz, seq, GROUP_W), BF16),
        scratch_shapes=[pltpu.VMEM((seq + 8, HEAD_DIM), F32), pltpu.VMEM((hps, seq, HEAD_DIM), F32),
                        pltpu.VMEM((hps, seq, HEAD_DIM), F32), pltpu.VMEM((hps, seq, 1), F32),
                        pltpu.VMEM((hps, seq, 1), F32)],
        compiler_params=_cparams(("arbitrary", "arbitrary")),
        name="mlstm",
    )(gate_b, z3, z3, z3, z3, zg3, conv_w, conv_w, out_g.reshape(1, HEAD_DIM))


def _ret_kernel(lg_ref, q_ref, k_ref, v_ref, g_ref, og_ref, y_ref):
    t = q_ref.shape[1]
    cl = RET_CHUNK
    rr = lax.broadcasted_iota(jnp.int32, (cl, cl), 0)
    cc = lax.broadcasted_iota(jnp.int32, (cl, cl), 1)
    diff = (rr - cc).astype(F32)
    jcol = lax.broadcasted_iota(jnp.int32, (cl, 1), 0).astype(F32)
    og = og_ref[...]
    consts = []
    for h in range(HEADS):
        lg = lg_ref[h]
        consts.append(dict(decay_in=jnp.where(diff >= 0, jnp.exp(lg * jnp.maximum(diff, 0.0)), 0.0),
                           zeta=jnp.exp(lg * (cl - 1.0 - jcol)), xi=jnp.exp(lg * (jcol + 1.0)),
                           g_chunk=jnp.exp(jnp.full((1, 1), lg * cl, F32))))

    def head_step(h, c, r_st):
        sl = pl.ds(pl.multiple_of(c * cl, cl), cl)
        cols = _head_cols(h)
        qc = q_ref[0, sl, cols]
        kc = k_ref[0, sl, cols] * HEAD_DIM ** -0.5
        vc = v_ref[0, sl, cols]
        scores = _dot_nt(qc, kc) * consts[h]["decay_in"]
        o = _dot(scores, vc) + _dot(qc, r_st) * consts[h]["xi"]
        r_new = consts[h]["g_chunk"] * r_st + _dot_tn(kc * consts[h]["zeta"], vc)
        gg = g_ref[0, sl, cols]
        y = _rms(o, og) * (gg * _sigmoid(gg))
        y_ref[0, sl, cols] = y.astype(y_ref.dtype)
        return r_new

    def body(c, carry):
        return tuple(head_step(h, c, carry[h]) for h in range(HEADS))

    lax.fori_loop(0, t // cl, body, (jnp.zeros((HEAD_DIM, HEAD_DIM), F32),) * HEADS)


def _retention(z3, log_gamma, out_g):
    bsz, seq, _ = z3.shape
    grp = lambda group: pl.BlockSpec((1, seq, GROUP_W), lambda b: (b, 0, group))
    return pl.pallas_call(
        _ret_kernel,
        grid=(bsz,),
        in_specs=[pl.BlockSpec(memory_space=pltpu.SMEM), grp(G_RQ), grp(G_RK), grp(G_RV), grp(G_RG),
                  pl.BlockSpec((1, HEAD_DIM), lambda b: (0, 0))],
        out_specs=pl.BlockSpec((1, seq, GROUP_W), lambda b: (b, 0, 0)),
        out_shape=jax.ShapeDtypeStruct((bsz, seq, GROUP_W), BF16),
        compiler_params=_cparams(("arbitrary",)),
        name="retention",
    )(log_gamma, z3, z3, z3, z3, out_g.reshape(1, HEAD_DIM))


BF16_SUBLANES = 16
ATT_TQ = 256
STRIP_W = 512
STRIP_SHIFT = 9
HEADS_PER_STEP = 2


def _rank_rows(vals):
    n = vals.shape[0]
    rowb = lax.broadcasted_iota(jnp.int32, vals.shape, 0)
    rank = jnp.zeros(vals.shape, jnp.int32)
    for jp in range(n):
        rv = vals[jp:jp + 1, :]
        beats = (rv > vals) | ((rv == vals) & (rowb > jp))
        rank = rank + beats.astype(jnp.int32)
    return rank


def _top_rows(vals, k):
    rowb = lax.broadcasted_iota(jnp.int32, vals.shape, 0).astype(F32)
    picked = jnp.zeros(vals.shape, jnp.bool_)
    for _ in range(k):
        best = jnp.max(vals, axis=0, keepdims=True)
        first = jnp.min(jnp.where(vals == best, rowb, float(vals.shape[0])), axis=0, keepdims=True)
        hit = (rowb == first) & (best > NEG)
        picked = picked | hit
        vals = jnp.where(hit, NEG, vals)
    return picked


def _pad_rows(x, rows):
    return jnp.concatenate([x, jnp.zeros((rows - x.shape[0], x.shape[1]), x.dtype)], axis=0)


def _fold_lanes(op, acc, x):
    for b in range(x.shape[1] // HEAD_DIM):
        acc = op(acc, x[:, b * HEAD_DIM:(b + 1) * HEAD_DIM])
    return acc


def _block_masked_attention(heads, qi, strip_s):
    tq, w = ATT_TQ, STRIP_W
    scale = HEAD_DIM ** -0.5
    t0 = qi * tq
    last = jnp.right_shift(t0, STRIP_SHIFT)
    rr = lax.broadcasted_iota(jnp.int32, (tq, w), 0)
    cc = lax.broadcasted_iota(jnp.int32, (tq, w), 1)
    rel = rr - cc
    rel_f = rel.astype(F32)
    alibi = [(-hd["slope"]) * rel_f for hd in heads]
    n_blk = heads[0]["unsel"].shape[0]
    erow = lax.broadcasted_iota(jnp.int32, (n_blk, w), 0)
    ecol = lax.broadcasted_iota(jnp.int32, (n_blk, w), 1)

    def scores(hd, c, bias):
        expand = jnp.where(erow == jnp.right_shift(c * w + ecol, hd["blk_shift"]), NEG, 0.0).astype(BF16)
        mask_bias = lax.dot_general(hd["unsel"], expand, (((0,), (0,)), ((), ())), preferred_element_type=F32)
        kc = hd["k_s"][pl.ds(pl.multiple_of(c * w, w), w), :]
        return _dot_nt(hd["qb"], kc) * scale + bias + mask_bias

    def first_pass(c, ms):
        out = []
        for hi, hd in enumerate(heads):
            s = scores(hd, c, alibi[hi] + (-hd["slope"]) * (t0 - c * w).astype(F32))
            strip_s[hi, c] = s
            out.append(_fold_lanes(jnp.maximum, ms[hi], s))
        return tuple(out)

    ms = lax.fori_loop(0, last, first_pass, tuple(jnp.full((tq, HEAD_DIM), NEG, F32) for _ in heads))
    dist = rel + (t0 - last * w)
    row_max = []
    for hi, hd in enumerate(heads):
        bias = jnp.where(dist >= 0, (-hd["slope"]) * dist.astype(F32), NEG)
        s = scores(hd, last, bias)
        strip_s[hi, last] = s
        row_max.append(jnp.max(_fold_lanes(jnp.maximum, ms[hi], s), axis=1, keepdims=True))

    def second_pass(c, carry):
        out = []
        for hi, hd in enumerate(heads):
            l_run, acc = carry[hi]
            p = jnp.exp(strip_s[hi, c] - row_max[hi])
            vc = hd["v_s"][pl.ds(pl.multiple_of(c * w, w), w), :]
            out.append((_fold_lanes(jnp.add, l_run, p), acc + _dot(p, vc)))
        return tuple(out)

    zero = jnp.zeros((tq, HEAD_DIM), F32)
    res = lax.fori_loop(0, last + 1, second_pass, tuple((zero, zero) for _ in heads))
    return [acc / jnp.sum(l_run, axis=1, keepdims=True) for l_run, acc in res]


def _head_cols(hh):
    return slice(hh * HEAD_DIM, (hh + 1) * HEAD_DIM)


def _moba_kernel(slope_ref, q_ref, k_ref, v_ref, g_ref, y_ref, kn_s, vb_s, kmean_s, strip_s):
    hp = pl.program_id(1)
    qi = pl.program_id(2)
    t = k_ref.shape[1]
    blk = MOBA_BLOCK
    nb = t // blk

    @pl.when(qi == 0)
    def _():
        for hh in range(HEADS_PER_STEP):
            kn = _rms(k_ref[0, :, _head_cols(hh)], g_ref[1:2, :])
            kn_s[hh] = kn.astype(kn_s.dtype)
            vb_s[hh] = v_ref[0, :, _head_cols(hh)].astype(vb_s.dtype)
            kmean_s[hh] = jnp.zeros(kmean_s.shape[1:], F32)
            for j in range(nb):
                kmean_s[hh, j:j + 1, :] = jnp.mean(kn[j * blk:(j + 1) * blk, :], axis=0, keepdims=True)

    rowb = lax.broadcasted_iota(jnp.int32, (nb, ATT_TQ), 0)
    heads = []
    for hh in range(HEADS_PER_STEP):
        qn = _rms(q_ref[0, :, _head_cols(hh)], g_ref[0:1, :])
        gate = jnp.where(rowb < qi, _dot_nt(kmean_s[hh], qn)[0:nb, :], NEG)
        sel = (_rank_rows(gate) < MOBA_TOPK) & (rowb < qi)
        unsel = jnp.where(sel | (rowb == qi), 0.0, 1.0)
        heads.append(dict(qb=qn.astype(MXU_DTYPE), k_s=kn_s.at[hh], v_s=vb_s.at[hh],
                          slope=slope_ref[hp * HEADS_PER_STEP + hh], blk_shift=MOBA_SHIFT,
                          unsel=_pad_rows(unsel, BF16_SUBLANES).astype(BF16)))
    outs = _block_masked_attention(heads, qi, strip_s)
    for hh in range(HEADS_PER_STEP):
        y_ref[0, :, _head_cols(hh)] = outs[hh].astype(y_ref.dtype)


def _moba(z3, qk_g, slopes):
    bsz, seq, _ = z3.shape
    assert MOBA_BLOCK == ATT_TQ and seq % STRIP_W == 0
    hps = HEADS_PER_STEP
    wide = hps * HEAD_DIM
    per_group = HEADS // hps
    kv = lambda group: pl.BlockSpec((1, seq, wide), lambda b, h, i: (b, 0, group * per_group + h))
    return pl.pallas_call(
        _moba_kernel,
        grid=(bsz, per_group, seq // ATT_TQ),
        in_specs=[pl.BlockSpec(memory_space=pltpu.SMEM),
                  pl.BlockSpec((1, ATT_TQ, wide), lambda b, h, i: (b, i, G_BQ * per_group + h)),
                  kv(G_BK), kv(G_BV),
                  pl.BlockSpec((2, HEAD_DIM), lambda b, h, i: (0, 0))],
        out_specs=pl.BlockSpec((1, ATT_TQ, wide), lambda b, h, i: (b, i, h)),
        out_shape=jax.ShapeDtypeStruct((bsz, seq, GROUP_W), BF16),
        scratch_shapes=[pltpu.VMEM((hps, seq, HEAD_DIM), MXU_DTYPE), pltpu.VMEM((hps, seq, HEAD_DIM), MXU_DTYPE),
                        pltpu.VMEM((hps, HEAD_DIM, HEAD_DIM), F32),
                        pltpu.VMEM((hps, seq // STRIP_W, ATT_TQ, STRIP_W), F32)],
        compiler_params=_cparams(("arbitrary", "arbitrary", "arbitrary")),
        name="moba",
    )(slopes, z3, z3, z3, qk_g)


def _gelu_tanh(x):
    return 0.5 * x * (1.0 + jnp.tanh(0.7978845608028654 * (x + 0.044715 * (x * x * x))))


def _nsa_kernel(slope_ref, q_ref, kc0_ref, kc1_ref, vc0_ref, vc1_ref, ks_ref, vs_ref, kw_ref, vw_ref, zg_ref,
                qg_ref, kg_ref, pe_ref, w1_ref, w2_ref, y_ref,
                kcmp_s, vcmp_s, ksn_s, vsb_s, kwn_s, vwb_s, strip_s):
    kc_refs, vc_refs = (kc0_ref, kc1_ref), (vc0_ref, vc1_ref)
    hp = pl.program_id(1)
    qi = pl.program_id(2)
    t = ks_ref.shape[1]
    tq = ATT_TQ
    nsub = t // NSA_CMP_STRIDE
    n_cmp = nsub - 1
    n_sel = t // NSA_SEL_BLOCK
    scale = HEAD_DIM ** -0.5

    @pl.when(qi == 0)
    def _():
        for hh in range(HEADS_PER_STEP):
            cols = _head_cols(hh)
            for cv, (src, dst) in enumerate(((kc_refs[hh], kcmp_s), (vc_refs[hh], vcmp_s))):
                acc_a = jnp.zeros((nsub, HEAD_DIM), F32)
                acc_b = jnp.zeros((nsub, HEAD_DIM), F32)
                for r in range(NSA_CMP_STRIDE):
                    zr = src[0, pl.ds(r, nsub, stride=NSA_CMP_STRIDE), :]
                    acc_a = acc_a + _dot(zr + pe_ref[cv, r:r + 1, :], w1_ref[cv, r])
                    rb = NSA_CMP_STRIDE + r
                    acc_b = acc_b + _dot(zr + pe_ref[cv, rb:rb + 1, :], w1_ref[cv, rb])
                hid = _gelu_tanh(acc_a + pltpu.roll(acc_b, nsub - 1, axis=0))
                cmp = _dot(hid, w2_ref[cv])
                if cv == 0:
                    cmp = _rms(cmp, kg_ref[0:1, :])
                dst[hh] = cmp.astype(dst.dtype)
            ksn_s[hh] = _rms(ks_ref[0, :, cols], kg_ref[1:2, :]).astype(ksn_s.dtype)
            vsb_s[hh] = vs_ref[0, :, cols].astype(vsb_s.dtype)
            kwn_s[hh] = _rms(kw_ref[0, :, cols], kg_ref[2:3, :]).astype(kwn_s.dtype)
            vwb_s[hh] = vw_ref[0, :, cols].astype(vwb_s.dtype)

    t0 = qi * tq
    rowi = lax.broadcasted_iota(jnp.int32, (tq, HEAD_DIM), 0)
    lane = lax.broadcasted_iota(jnp.int32, (tq, HEAD_DIM), 1)
    dist_c = (t0 + rowi) - (lane * NSA_CMP_STRIDE + (NSA_CMP_LEN - 1))
    ok_c = (dist_c >= 0) & (lane < n_cmp)
    dist_cf = dist_c.astype(F32)

    ob = lax.broadcasted_iota(jnp.int32, (HEAD_DIM, nsub), 0)
    oc = lax.broadcasted_iota(jnp.int32, (HEAD_DIM, nsub), 1)
    overlap_t = ((oc * NSA_CMP_STRIDE <= ob * NSA_SEL_BLOCK + (NSA_SEL_BLOCK - 1))
                 & (oc * NSA_CMP_STRIDE + (NSA_CMP_LEN - 1) >= ob * NSA_SEL_BLOCK)
                 & (oc < n_cmp) & (ob < n_sel)).astype(F32)
    rowb = lax.broadcasted_iota(jnp.int32, (n_sel, tq), 0)
    cur = jnp.right_shift(t0 + lax.broadcasted_iota(jnp.int32, (n_sel, tq), 1), NSA_SEL_SHIFT)

    win_w = NSA_WINDOW + tq
    k0 = jnp.maximum(t0 - NSA_WINDOW, 0)
    wr = lax.broadcasted_iota(jnp.int32, (tq, win_w), 0)
    wc = lax.broadcasted_iota(jnp.int32, (tq, win_w), 1)
    dist_w = (t0 - k0) + wr - wc
    ok_w = (dist_w >= 0) & (dist_w < NSA_WINDOW)
    dist_wf = dist_w.astype(F32)
    win_rows = pl.ds(pl.multiple_of(k0, tq), win_w)

    heads, o_cmp, o_win = [], [], []
    for hh in range(HEADS_PER_STEP):
        slope = slope_ref[hp * HEADS_PER_STEP + hh]
        qn = _rms(q_ref[0, :, _head_cols(hh)], qg_ref[...])
        qb = qn.astype(MXU_DTYPE)

        s_c = jnp.where(ok_c, _dot_nt(qb, kcmp_s[hh]) * scale - slope * dist_cf, NEG)
        m_c = jnp.max(s_c, axis=1, keepdims=True)
        e_c = jnp.where(ok_c, jnp.exp(s_c - m_c), 0.0)
        p_c = e_c / jnp.maximum(jnp.sum(e_c, axis=1, keepdims=True), 1e-30)
        o_cmp.append(_dot(p_c, vcmp_s[hh]))

        imp = _dot_split_nt(overlap_t, p_c)[0:n_sel, :]
        sel = (rowb == cur) | _top_rows(jnp.where(rowb < cur, imp, NEG), NSA_SEL_TOPN - 1)
        unsel = jnp.where(sel, 0.0, 1.0)
        heads.append(dict(qb=qb, k_s=ksn_s.at[hh], v_s=vsb_s.at[hh], slope=slope, blk_shift=NSA_SEL_SHIFT,
                          unsel=unsel.astype(BF16)))

        s_w = _dot_nt(qb, kwn_s[hh, win_rows, :]) * scale + jnp.where(ok_w, (-slope) * dist_wf, NEG)
        p_w = jnp.exp(s_w - jnp.max(s_w, axis=1, keepdims=True))
        o_win.append(_dot(p_w, vwb_s[hh, win_rows, :]) / jnp.sum(p_w, axis=1, keepdims=True))

    o_slc = _block_masked_attention(heads, qi, strip_s)

    zg = zg_ref[0]
    for hh in range(HEADS_PER_STEP):
        h = hp * HEADS_PER_STEP + hh
        g_cmp = _sigmoid(_lane_col(zg, GL_NG + h))
        g_slc = _sigmoid(_lane_col(zg, GL_NG + HEADS + h))
        g_win = _sigmoid(_lane_col(zg, GL_NG + 2 * HEADS + h))
        y_ref[0, :, _head_cols(hh)] = (g_cmp * o_cmp[hh] + g_slc * o_slc[hh] + g_win * o_win[hh]).astype(y_ref.dtype)


def _nsa(z3, zg3, q_g, k_g, cmp_pos, cmp_w1, cmp_w2, slopes):
    bsz, seq, _ = z3.shape
    assert seq // NSA_CMP_STRIDE == HEAD_DIM, "compressed blocks are laid out on the 128 lanes"
    assert seq % STRIP_W == 0 and seq >= NSA_WINDOW + ATT_TQ
    tq = ATT_TQ
    hps = HEADS_PER_STEP
    wide = hps * HEAD_DIM
    per_group = HEADS // hps
    kv = lambda group: pl.BlockSpec((1, seq, wide), lambda b, h, i: (b, 0, group * per_group + h))
    one = lambda group, hh: pl.BlockSpec((1, seq, HEAD_DIM), lambda b, h, i: (b, 0, group * HEADS + h * hps + hh))
    full = lambda shape: pl.BlockSpec(shape, lambda b, h, i: (0,) * len(shape))
    w1 = cmp_w1.reshape(2, NSA_CMP_LEN, HEAD_DIM, HEAD_DIM).astype(MXU_DTYPE)
    w2 = cmp_w2.astype(MXU_DTYPE)
    seq_buf = lambda: pltpu.VMEM((hps, seq, HEAD_DIM), MXU_DTYPE)
    cmp_buf = lambda: pltpu.VMEM((hps, HEAD_DIM, HEAD_DIM), MXU_DTYPE)
    return pl.pallas_call(
        _nsa_kernel,
        grid=(bsz, per_group, seq // tq),
        in_specs=[pl.BlockSpec(memory_space=pltpu.SMEM),
                  pl.BlockSpec((1, tq, wide), lambda b, h, i: (b, i, G_NQ * per_group + h)),
                  one(G_NKC, 0), one(G_NKC, 1), one(G_NVC, 0), one(G_NVC, 1),
                  kv(G_NKS), kv(G_NVS), kv(G_NKW), kv(G_NVW),
                  pl.BlockSpec((1, tq, GATE_W), lambda b, h, i: (b, i, 0)),
                  full((1, HEAD_DIM)), full((3, HEAD_DIM)), full((2, NSA_CMP_LEN, HEAD_DIM)),
                  full((2, NSA_CMP_LEN, HEAD_DIM, HEAD_DIM)), full((2, HEAD_DIM, HEAD_DIM))],
        out_specs=pl.BlockSpec((1, tq, wide), lambda b, h, i: (b, i, h)),
        out_shape=jax.ShapeDtypeStruct((bsz, seq, GROUP_W), BF16),
        scratch_shapes=[cmp_buf(), cmp_buf(), seq_buf(), seq_buf(), seq_buf(), seq_buf(),
                        pltpu.VMEM((hps, seq // STRIP_W, tq, STRIP_W), F32)],
        compiler_params=_cparams(("arbitrary", "arbitrary", "arbitrary")),
        name="nsa",
    )(slopes, z3, z3, z3, z3, z3, z3, z3, z3, z3, zg3, q_g.reshape(1, HEAD_DIM), k_g, cmp_pos, w1, w2)


def _out_proj_kernel(ym_ref, yb_ref, yr_ref, yn_ref, w_ref, x_ref, g_ref, o_ref):
    acc = jnp.dot(ym_ref[...], w_ref[0, 0:GROUP_W, :], preferred_element_type=F32)
    acc = acc + jnp.dot(yb_ref[...], w_ref[0, GROUP_W:2 * GROUP_W, :], preferred_element_type=F32)
    acc = acc + jnp.dot(yr_ref[...], w_ref[0, 2 * GROUP_W:3 * GROUP_W, :], preferred_element_type=F32)
    acc = acc + jnp.dot(yn_ref[...], w_ref[0, 3 * GROUP_W:4 * GROUP_W, :], preferred_element_type=F32)
    o_ref[...] = x_ref[...] + g_ref[0] * acc


def _out_proj(ys, w_out, layer, x2d, mod3, seq):
    n, d = x2d.shape
    tm, tn = 1024, 1024
    per_b = seq // tm
    y_spec = pl.BlockSpec((tm, GROUP_W), lambda i, j: (i, 0))
    return pl.pallas_call(
        _out_proj_kernel,
        grid=(n // tm, d // tn),
        in_specs=[y_spec, y_spec, y_spec, y_spec,
                  pl.BlockSpec((1, 4 * GROUP_W, tn), lambda i, j: (layer, 0, j)),
                  pl.BlockSpec((tm, tn), lambda i, j: (i, j)),
                  pl.BlockSpec((1, 1, tn), lambda i, j: (i // per_b, 0, 2 * (d // tn) + j))],
        out_specs=pl.BlockSpec((tm, tn), lambda i, j: (i, j)),
        out_shape=jax.ShapeDtypeStruct((n, d), F32),
        compiler_params=_cparams(("arbitrary", "arbitrary")),
        name="out_proj",
    )(*[y.reshape(n, GROUP_W) for y in ys], w_out, x2d, mod3)


SLAB_E0, SLAB_E1, SLAB_R0, SLAB_R1, SLAB_G0, SLAB_G1 = 0, 1, 2, 3, 4, 5


def _route_kernel(x_ref, g_ref, sc_ref, sh_ref, wr_ref, br_ref, h_ref, slab_ref, cnt_ref, carry_s):
    tm = x_ref.shape[0]

    @pl.when(pl.program_id(0) == 0)
    def _():
        carry_s[...] = jnp.zeros(carry_s.shape, F32)

    hmod = _rms(x_ref[...], g_ref[...]) * (1.0 + sc_ref[0]) + sh_ref[0]
    _store_slabs(h_ref, hmod)
    logits = _dot(hmod, wr_ref[...]) + br_ref[...]
    lane = lax.broadcasted_iota(jnp.int32, logits.shape, 1).astype(F32)
    far = 4.0 * GATE_W

    in_g = lane < N_GROUPS
    lg = jnp.where(in_g, logits, NEG)
    g_max = jnp.max(lg, axis=1, keepdims=True)
    grp = jnp.min(jnp.where(in_g & (lg == g_max), lane, far), axis=1, keepdims=True)
    p_grp = 1.0 / jnp.sum(jnp.where(in_g, jnp.exp(lg - g_max), 0.0), axis=1, keepdims=True)

    lo = N_GROUPS + grp * EXPERTS_PER_GROUP
    in_e = (lane >= lo) & (lane < lo + EXPERTS_PER_GROUP)
    le = jnp.where(in_e, logits, NEG)
    e_max = jnp.max(le, axis=1, keepdims=True)
    ee = jnp.where(in_e, jnp.exp(le - e_max), 0.0)
    pe = jnp.where(in_e, ee / jnp.sum(ee, axis=1, keepdims=True), -1.0)
    p1 = jnp.max(pe, axis=1, keepdims=True)
    i1 = jnp.min(jnp.where(pe == p1, lane, far), axis=1, keepdims=True)
    pe2 = jnp.where(lane == i1, -1.0, pe)
    p2 = jnp.max(pe2, axis=1, keepdims=True)
    i2 = jnp.min(jnp.where(pe2 == p2, lane, far), axis=1, keepdims=True)
    e0 = i1 - N_GROUPS
    e1 = i2 - N_GROUPS
    g0 = p_grp * (p1 / (p1 + p2))
    g1 = p_grp * (p2 / (p1 + p2))

    onehot = ((lane == e0) | (lane == e1)).astype(BF16)
    rr = lax.broadcasted_iota(jnp.int32, (tm, tm), 0)
    cc = lax.broadcasted_iota(jnp.int32, (tm, tm), 1)
    before = jnp.dot((cc < rr).astype(BF16), onehot, preferred_element_type=F32) + carry_s[...]
    r0 = jnp.sum(jnp.where(lane == e0, before, 0.0), axis=1, keepdims=True)
    r1 = jnp.sum(jnp.where(lane == e1, before, 0.0), axis=1, keepdims=True)
    carry_s[...] = carry_s[...] + jnp.sum(onehot.astype(F32), axis=0, keepdims=True)
    cnt_ref[...] = carry_s[...]

    slab = jnp.where(lane == SLAB_E0, e0.astype(F32), 0.0)
    slab = jnp.where(lane == SLAB_E1, e1.astype(F32), slab)
    slab = jnp.where(lane == SLAB_R0, r0, slab)
    slab = jnp.where(lane == SLAB_R1, r1, slab)
    slab = jnp.where(lane == SLAB_G0, g0, slab)
    slab = jnp.where(lane == SLAB_G1, g1, slab)
    slab_ref[...] = slab


def _route(x2d, norm_g, mod3, w_router, b_router, seq):
    n, d = x2d.shape
    tm = 512
    per_b = seq // tm
    return pl.pallas_call(
        _route_kernel,
        grid=(n // tm,),
        in_specs=[pl.BlockSpec((tm, d), lambda i: (i, 0)),
                  pl.BlockSpec((1, d), lambda i: (0, 0)),
                  pl.BlockSpec((1, 1, d), lambda i: (i // per_b, 0, 4)),
                  pl.BlockSpec((1, 1, d), lambda i: (i // per_b, 0, 3)),
                  pl.BlockSpec((d, GATE_W), lambda i: (0, 0)),
                  pl.BlockSpec((1, GATE_W), lambda i: (0, 0))],
        out_specs=[pl.BlockSpec((tm * SLAB, HEAD_DIM), lambda i: (i, 0)),
                   pl.BlockSpec((tm, GATE_W), lambda i: (i, 0)),
                   pl.BlockSpec((1, GATE_W), lambda i: (0, 0))],
        out_shape=[jax.ShapeDtypeStruct((n * SLAB, HEAD_DIM), F32),
                   jax.ShapeDtypeStruct((n, GATE_W), F32),
                   jax.ShapeDtypeStruct((1, GATE_W), F32)],
        scratch_shapes=[pltpu.VMEM((1, GATE_W), F32)],
        compiler_params=_cparams(("arbitrary",)),
        name="route",
    )(x2d, norm_g.reshape(1, d), mod3, mod3, w_router, b_router)


def _invert_kernel(dest_ref, pad_lo_ref, pad_hi_ref, inv_ref):
    def clear(i, _):
        inv_ref[i] = 0
        return 0

    def clear_segment(g, _):
        lax.fori_loop(pad_lo_ref[g], pad_hi_ref[g], clear, 0)
        return 0

    def put(a, _):
        inv_ref[dest_ref[a]] = jnp.right_shift(a, 1)
        return 0

    lax.fori_loop(0, pad_lo_ref.shape[0], clear_segment, 0)
    lax.fori_loop(0, dest_ref.shape[0], put, 0, unroll=32)


def _invert(dest_flat, pad_lo, pad_hi, cap):
    smem = pl.BlockSpec(memory_space=pltpu.SMEM)
    return pl.pallas_call(
        _invert_kernel,
        in_specs=[smem, smem, smem],
        out_specs=smem,
        out_shape=jax.ShapeDtypeStruct((cap,), jnp.int32),
        name="invert",
    )(dest_flat, pad_lo, pad_hi)


def _start_row_gather(idx_ref, src_hbm, dst, sem, n_rows, first=0, span=1):
    for r in range(first, n_rows):
        start = idx_ref[r] if span == 1 else pl.multiple_of(idx_ref[r] * span, span)
        pltpu.make_async_copy(src_hbm.at[pl.ds(start, span)], dst.at[pl.ds(r * span, span)], sem).start(
            priority=r % 2)


def _wait_row_gather(src_hbm, dst, sem, n_rows, span=1):
    pltpu.make_async_copy(src_hbm.at[pl.ds(0, n_rows * span)], dst, sem).wait()


SLAB = D_MODEL // HEAD_DIM


def _store_slabs(ref, x):
    for c in range(SLAB):
        ref[pl.ds(c, x.shape[0], stride=SLAB), :] = x[:, c * HEAD_DIM:(c + 1) * HEAD_DIM]


def _load_slabs(ref, n_rows, lead=()):
    return jnp.concatenate([ref[lead + (pl.ds(c, n_rows, stride=SLAB), slice(None))] for c in range(SLAB)], axis=1)


def _expert_kernel(blk_e_ref, n_used_ref, run_slot_ref, next_e_ref, next_ok_ref, inv_ref, h_hbm,
                   w1_hbm, w3_hbm, w2_hbm, y_ref,
                   x_s, w1_f, w3_f, w2_f, w1_s, w3_s, w2_s, sem, wsem, *, layer):
    s = pl.program_id(0)
    rows = EXPERT_ROWS
    n_used = n_used_ref[0]
    ahead = EXPERT_GATHER_AHEAD
    slot = lax.rem(s, ahead + 1)
    cslot = lax.rem(s + 1, ahead + 1)

    blk = s - ahead
    prev = jnp.maximum(blk - 1, 0)
    gather = s < n_used
    compute = (s >= ahead) & (blk < n_used)

    def start_gather(first, stop):
        _start_row_gather(inv_ref, h_hbm, x_s.at[slot], sem.at[slot], stop, first, span=SLAB)

    def weight_copies(e, wslot):
        return (pltpu.make_async_copy(w1_hbm.at[layer, e], w1_f.at[wslot], wsem.at[wslot, 0]),
                pltpu.make_async_copy(w3_hbm.at[layer, e], w3_f.at[wslot], wsem.at[wslot, 1]),
                pltpu.make_async_copy(w2_hbm.at[layer, e], w2_f.at[wslot], wsem.at[wslot, 2]))

    def expert_block(gather_too):
        @pl.when((blk == 0) | (blk_e_ref[blk] != blk_e_ref[prev]))
        def _():
            wslot = run_slot_ref[blk]
            for cp in weight_copies(blk_e_ref[blk], wslot):
                cp.wait()

            @pl.when(next_ok_ref[blk] == 1)
            def _():
                for cp in weight_copies(next_e_ref[blk], 1 - wslot):
                    cp.start()

            w1_s[...] = w1_f[wslot].astype(w1_s.dtype)
            w3_s[...] = w3_f[wslot].astype(w3_s.dtype)
            w2_s[...] = w2_f[wslot].astype(w2_s.dtype)

        cuts = (0, rows // 4, rows // 2, rows) if gather_too else (0, 0, 0, 0)
        _wait_row_gather(h_hbm, x_s.at[cslot], sem.at[cslot], rows, span=SLAB)
        start_gather(cuts[0], cuts[1])
        x = _load_slabs(x_s, rows, (cslot,)).astype(MXU_DTYPE)
        a = jnp.dot(x, w1_s[...], preferred_element_type=F32)
        start_gather(cuts[1], cuts[2])
        b = jnp.dot(x, w3_s[...], preferred_element_type=F32)
        start_gather(cuts[2], cuts[3])
        y_ref[...] = _dot(a * _sigmoid(a) * b, w2_s[...])

    @pl.when(gather & compute)
    def _():
        expert_block(True)

    @pl.when(gather & jnp.logical_not(compute))
    def _():
        @pl.when(s == 0)
        def _():
            for cp in weight_copies(blk_e_ref[0], run_slot_ref[0]):
                cp.start()

        start_gather(0, rows)

    @pl.when(compute & jnp.logical_not(gather))
    def _():
        expert_block(False)

    @pl.when((s >= ahead) & (blk >= n_used))
    def _():
        y_ref[...] = jnp.zeros(y_ref.shape, y_ref.dtype)


def _experts(h2, inv, blk_e, n_used, w1, w3, w2, layer):
    d = D_MODEL
    rows = EXPERT_ROWS
    n_blocks = inv.shape[0] // rows
    idx = jnp.arange(n_blocks, dtype=jnp.int32)
    run_start = jnp.concatenate([jnp.ones((1,), jnp.int32), (blk_e[1:] != blk_e[:-1]).astype(jnp.int32)])
    run_slot = (jnp.cumsum(run_start) - 1) % 2
    later_run = (idx[None, :] > idx[:, None]) & (blk_e[None, :] != blk_e[:, None])
    next_start = jnp.min(jnp.where(later_run, idx[None, :], n_blocks), axis=1)
    next_e = blk_e[jnp.minimum(next_start, n_blocks - 1)]
    next_ok = (next_start < n_used[0]).astype(jnp.int32)

    ahead = EXPERT_GATHER_AHEAD
    done = lambda s: jnp.maximum(s - ahead, 0)
    hbm = pl.BlockSpec(memory_space=pl.ANY)
    return pl.pallas_call(
        functools.partial(_expert_kernel, layer=layer),
        grid_spec=pltpu.PrefetchScalarGridSpec(
            num_scalar_prefetch=5,
            grid=(n_blocks + ahead,),
            in_specs=[pl.BlockSpec((rows,), lambda s, *_: (jnp.minimum(s, n_blocks - 1),), memory_space=pltpu.SMEM),
                      hbm, hbm, hbm, hbm],
            out_specs=pl.BlockSpec((rows, d), lambda s, *_: (done(s), 0)),
            scratch_shapes=[pltpu.VMEM((ahead + 1, rows * SLAB, HEAD_DIM), F32),
                            pltpu.VMEM((2, d, D_EXPERT), F32), pltpu.VMEM((2, d, D_EXPERT), F32),
                            pltpu.VMEM((2, D_EXPERT, d), F32),
                            pltpu.VMEM((d, D_EXPERT), MXU_DTYPE), pltpu.VMEM((d, D_EXPERT), MXU_DTYPE),
                            pltpu.VMEM((D_EXPERT, d), MXU_DTYPE),
                            pltpu.SemaphoreType.DMA((ahead + 1,)), pltpu.SemaphoreType.DMA((2, 3))]),
        out_shape=jax.ShapeDtypeStruct((n_blocks * rows, d), F32),
        compiler_params=_cparams(("arbitrary",)),
        name="experts",
    )(blk_e, n_used, run_slot.astype(jnp.int32), next_e.astype(jnp.int32), next_ok, inv, h2, w1, w3, w2)


def _combine_kernel(d0_ref, d1_ref, x_ref, g_ref, slab_ref, yb_hbm, o_ref, rows_s, sem):
    s = pl.program_id(0)
    n_tiles = pl.num_programs(0) - 1
    tm = x_ref.shape[0]
    slot = lax.rem(s, 2)

    @pl.when(s < n_tiles)
    def _():
        _start_row_gather(d0_ref, yb_hbm, rows_s.at[slot, 0], sem.at[slot], tm)
        _start_row_gather(d1_ref, yb_hbm, rows_s.at[slot, 1], sem.at[slot], tm)

    @pl.when(s >= 1)
    def _():
        _wait_row_gather(yb_hbm, rows_s.at[1 - slot, 0], sem.at[1 - slot], tm)
        _wait_row_gather(yb_hbm, rows_s.at[1 - slot, 1], sem.at[1 - slot], tm)
        route = slab_ref[...]
        g0 = route[:, SLAB_G0:SLAB_G0 + 1]
        g1 = route[:, SLAB_G1:SLAB_G1 + 1]
        o_ref[...] = x_ref[...] + g_ref[0] * (g0 * rows_s[1 - slot, 0] + g1 * rows_s[1 - slot, 1])


def _combine(x2d, mod3, slab, dest0, dest1, yb, seq):
    n, d = x2d.shape
    tm = 256
    per_b = seq // tm
    n_tiles = n // tm
    done = lambda s: jnp.maximum(s - 1, 0)
    idx_spec = pl.BlockSpec((tm,), lambda s: (jnp.minimum(s, n_tiles - 1),), memory_space=pltpu.SMEM)
    return pl.pallas_call(
        _combine_kernel,
        grid=(n_tiles + 1,),
        in_specs=[idx_spec, idx_spec,
                  pl.BlockSpec((tm, d), lambda s: (done(s), 0)),
                  pl.BlockSpec((1, 1, d), lambda s: (done(s) // per_b, 0, 5)),
                  pl.BlockSpec((tm, GATE_W), lambda s: (done(s), 0)),
                  pl.BlockSpec(memory_space=pl.ANY)],
        out_specs=pl.BlockSpec((tm, d), lambda s: (done(s), 0)),
        out_shape=jax.ShapeDtypeStruct((n, d), F32),
        scratch_shapes=[pltpu.VMEM((2, 2, tm, d), F32), pltpu.SemaphoreType.DMA((2,))],
        compiler_params=_cparams(("arbitrary",)),
        name="combine",
    )(dest0, dest1, x2d, mod3, slab, yb)


_OFF_MI = 4 * GROUP_W
_OFF_BQ = _OFF_MI + 2 * HEADS
_OFF_NG = _OFF_BQ + 14 * GROUP_W


def _pack_w_kernel(w_ref, o_ref, og_ref):
    gap = _OFF_BQ - _OFF_MI
    o_ref[0, :, 0:_OFF_MI] = w_ref[0, :, 0:_OFF_MI].astype(o_ref.dtype)
    tail = w_ref[0, :, _OFF_MI:]
    width = tail.shape[1]
    o_ref[0, :, _OFF_MI:] = pltpu.roll(tail, width - gap, axis=1)[:, 0:D_WIDE - _OFF_MI].astype(o_ref.dtype)
    lane = lax.broadcasted_iota(jnp.int32, (w_ref.shape[1], GATE_W), 1)
    first = w_ref[0, :, _OFF_MI:_OFF_MI + GATE_W]
    ragged = w_ref[0, :, _OFF_NG - GL_NG:_OFF_NG - GL_NG + GATE_W]
    gate = jnp.where(lane < GL_NG, first, jnp.where(lane < GL_NG + 3 * HEADS, ragged, 0.0))
    og_ref[0] = gate.astype(og_ref.dtype)


def _pack_w_in(w_in):
    depth, d, d_in = w_in.shape
    tk = 128
    lanes_in = -(-d_in // GATE_W) * GATE_W
    assert (_OFF_NG - GL_NG) % GATE_W == 0 and _OFF_NG - GL_NG + GATE_W == lanes_in
    return pl.pallas_call(
        _pack_w_kernel,
        grid=(depth, d // tk),
        in_specs=[pl.BlockSpec((1, tk, lanes_in), lambda l, i: (l, i, 0))],
        out_specs=[pl.BlockSpec((1, tk, D_WIDE), lambda l, i: (l, i, 0)),
                   pl.BlockSpec((1, tk, GATE_W), lambda l, i: (l, i, 0))],
        out_shape=[jax.ShapeDtypeStruct((depth, d, D_WIDE), MXU_DTYPE),
                   jax.ShapeDtypeStruct((depth, d, GATE_W), MXU_DTYPE)],
        compiler_params=_cparams(("arbitrary", "arbitrary")),
        name="pack_w_in",
    )(w_in)


def _moe(x2d, norm_g, mod3, wg, bg, we, be, w1, w3, w2, layer, seq):
    n, d = x2d.shape
    n_route = N_GROUPS + N_EXPERTS
    w_router = jnp.concatenate([wg, we, jnp.zeros((d, GATE_W - n_route), wg.dtype)], axis=1).astype(MXU_DTYPE)
    b_router = jnp.concatenate([bg, be, jnp.zeros((GATE_W - n_route,), bg.dtype)]).reshape(1, GATE_W)
    h2, slab, cnt = _route(x2d, norm_g, mod3, w_router, b_router, seq)

    rows = EXPERT_ROWS
    counts = cnt[0, :N_EXPERTS].astype(jnp.int32)
    pcounts = (counts + rows - 1) // rows * rows
    pends = jnp.cumsum(pcounts)
    pstarts = pends - pcounts
    eid = slab[:, SLAB_E0:SLAB_E1 + 1].astype(jnp.int32)
    rank = slab[:, SLAB_R0:SLAB_R1 + 1].astype(jnp.int32)
    dest = pstarts[eid] + rank
    n_blocks = -(-2 * n // rows) + N_EXPERTS
    blk_row0 = jnp.arange(n_blocks, dtype=jnp.int32) * rows
    blk_e = jnp.minimum(jnp.sum((pends[None, :] <= blk_row0[:, None]).astype(jnp.int32), axis=1), N_EXPERTS - 1)
    n_used = (pends[-1:] // rows).astype(jnp.int32)

    cap = n_blocks * rows
    pad_lo = jnp.concatenate([pstarts + counts, pends[-1:]]).astype(jnp.int32)
    pad_hi = jnp.concatenate([pends, jnp.full((1,), cap, jnp.int32)]).astype(jnp.int32)
    inv = _invert(dest.reshape(-1), pad_lo, pad_hi, cap)
    yb = _experts(h2, inv, blk_e, n_used, w1, w3, w2, layer)
    return _combine(x2d, mod3, slab, dest[:, 0], dest[:, 1], yb, seq)


def _layer(x2d, mod, bsz, seq, layer, norm1_g, norm2_g, w_wide, w_gate, mlstm_gate_b, mlstm_conv_w, mlstm_out_g,
           moba_qk_g, ret_out_g, nsa_q_g, nsa_k_g, nsa_cmp_pos, nsa_cmp_w1, nsa_cmp_w2, w_out, router_g_w,
           router_g_b, router_e_w, router_e_b, exp_w1, exp_w3, exp_w2, slopes, log_gamma):
    n, d = x2d.shape
    mod3 = mod.reshape(bsz, 1, 6 * d)
    z, zg = _norm_in_proj(x2d, norm1_g, mod3, w_wide, layer, w_gate, seq)
    z3 = z.reshape(bsz, seq, D_WIDE)
    zg3 = zg.reshape(bsz, seq, GATE_W)
    y_m = _mlstm(z3, zg3, mlstm_gate_b, mlstm_conv_w, mlstm_out_g)
    y_b = _moba(z3, moba_qk_g, slopes[0::2])
    y_r = _retention(z3, log_gamma, ret_out_g)
    y_n = _nsa(z3, zg3, nsa_q_g, nsa_k_g, nsa_cmp_pos, nsa_cmp_w1, nsa_cmp_w2, slopes[1::2])
    x2d = _out_proj((y_m, y_b, y_r, y_n), w_out, layer, x2d, mod3, seq)
    return _moe(x2d, norm2_g, mod3, router_g_w, router_g_b, router_e_w, router_e_b, exp_w1, exp_w3, exp_w2,
                layer, seq)


def kernel(x, c, norm1_g, norm2_g, ada_w, ada_b, w_in, mlstm_gate_b, mlstm_conv_w, mlstm_out_g, moba_qk_g,
           ret_out_g, nsa_q_g, nsa_k_g, nsa_cmp_pos, nsa_cmp_w1, nsa_cmp_w2, w_out, router_g_w, router_g_b,
           router_e_w, router_e_b, exp_w1, exp_w3, exp_w2):
    bsz, seq, d = x.shape
    depth = ada_w.shape[0]
    n_softmax_heads = 2 * HEADS
    slopes = jnp.exp2(-8.0 * jnp.arange(1, n_softmax_heads + 1, dtype=F32) / n_softmax_heads)
    log_gamma = jnp.log(1.0 - jnp.exp2(-5.0 - jnp.arange(HEADS, dtype=F32)))
    mod = _ada_mod(c, ada_w, ada_b)
    w_wide, w_gate = _pack_w_in(w_in)
    w_out_b = w_out.astype(MXU_DTYPE)
    x2d = x.reshape(bsz * seq, d)
    for l in range(depth):
        x2d = _layer(x2d, mod[l], bsz, seq, l, norm1_g[l], norm2_g[l], w_wide, w_gate, mlstm_gate_b[l],
                     mlstm_conv_w[l], mlstm_out_g[l], moba_qk_g[l], ret_out_g[l], nsa_q_g[l], nsa_k_g[l],
                     nsa_cmp_pos[l], nsa_cmp_w1[l], nsa_cmp_w2[l], w_out_b, router_g_w[l], router_g_b[l],
                     router_e_w[l], router_e_b[l], exp_w1, exp_w3, exp_w2, slopes, log_gamma)
    return x2d.reshape(bsz, seq, d)
```

```python
import functools

import jax
import jax.numpy as jnp
from jax import lax
from jax.experimental import pallas as pl
from jax.experimental.pallas import tpu as pltpu

F32 = jnp.float32
BF16 = jnp.bfloat16
MXU_DTYPE = jnp.bfloat16

D_MODEL = 2048
HEAD_DIM = 128
HEADS = 4
GROUP_W = HEADS * HEAD_DIM
N_WIDE_GROUPS = 18
D_WIDE = N_WIDE_GROUPS * GROUP_W
GATE_W = 128

MLSTM_CHUNK = 64
MLSTM_CONV = 4
MOBA_BLOCK = 256
MOBA_SHIFT = 8
MOBA_TOPK = 3
RET_CHUNK = 128
NSA_CMP_LEN = 32
NSA_CMP_STRIDE = 16
NSA_SEL_BLOCK = 64
NSA_SEL_SHIFT = 6
NSA_SEL_TOPN = 4
NSA_WINDOW = 512
N_GROUPS = 4
EXPERTS_PER_GROUP = 8
N_EXPERTS = N_GROUPS * EXPERTS_PER_GROUP
D_EXPERT = 512
EXPERT_ROWS = 256
EXPERT_GATHER_AHEAD = 2

NORM_EPS = 1e-6
NEG = -1e30
BIG = 1e9

G_MQ, G_MK, G_MV, G_MO = 0, 1, 2, 3
G_BQ, G_BK, G_BV = 4, 5, 6
G_RQ, G_RK, G_RV, G_RG = 7, 8, 9, 10
G_NQ, G_NKC, G_NVC, G_NKS, G_NVS, G_NKW, G_NVW = 11, 12, 13, 14, 15, 16, 17
GL_MI, GL_MF, GL_NG = 0, 4, 8

VMEM_LIMIT = 56 * 1024 * 1024


def _cparams(sem):
    return pltpu.CompilerParams(dimension_semantics=sem, vmem_limit_bytes=VMEM_LIMIT)


def _dot(a, b):
    return jnp.dot(a.astype(MXU_DTYPE), b.astype(MXU_DTYPE), preferred_element_type=F32)


def _dot_nt(a, b):
    return lax.dot_general(a.astype(MXU_DTYPE), b.astype(MXU_DTYPE), (((1,), (1,)), ((), ())),
                           preferred_element_type=F32)


def _dot_tn(a, b):
    return lax.dot_general(a.astype(MXU_DTYPE), b.astype(MXU_DTYPE), (((0,), (0,)), ((), ())),
                           preferred_element_type=F32)


def _dot_split_nt(b01, a):
    a_hi = a.astype(MXU_DTYPE)
    r1 = a - a_hi.astype(F32)
    a_mid = r1.astype(MXU_DTYPE)
    a_lo = (r1 - a_mid.astype(F32)).astype(MXU_DTYPE)
    return _dot_nt(b01, a_hi) + _dot_nt(b01, a_mid) + _dot_nt(b01, a_lo)


def _rms(x, g):
    return x * lax.rsqrt(jnp.mean(x * x, axis=-1, keepdims=True) + NORM_EPS) * g


def _sigmoid(x):
    return jax.nn.sigmoid(x)


def _lane_col(x, idx):
    lane = lax.broadcasted_iota(jnp.int32, x.shape, 1)
    return jnp.sum(jnp.where(lane == idx, x, 0.0), axis=1, keepdims=True)


def _ada_kernel(c_ref, w_ref, b_ref, o_ref):
    c = c_ref[...]
    o_ref[0] = _dot(c * _sigmoid(c), w_ref[0]) + b_ref[0]


def _ada_mod(c, ada_w, ada_b):
    depth, d, n6 = ada_w.shape
    b = c.shape[0]
    tn = 1024
    return pl.pallas_call(
        _ada_kernel,
        grid=(depth, n6 // tn),
        in_specs=[pl.BlockSpec((b, d), lambda l, j: (0, 0)),
                  pl.BlockSpec((1, d, tn), lambda l, j: (l, 0, j)),
                  pl.BlockSpec((1, 1, tn), lambda l, j: (l, 0, j))],
        out_specs=pl.BlockSpec((1, b, tn), lambda l, j: (l, 0, j)),
        out_shape=jax.ShapeDtypeStruct((depth, b, n6), F32),
        compiler_params=_cparams(("arbitrary", "arbitrary")),
        name="ada_mod",
    )(c, ada_w, ada_b.reshape(depth, 1, n6))


def _norm_in_kernel(x_ref, g_ref, sc_ref, sh_ref, w_ref, ws_ref, z_ref, zg_ref, h_s):
    @pl.when(pl.program_id(1) == 0)
    def _():
        h = _rms(x_ref[...], g_ref[...]) * (1.0 + sc_ref[0]) + sh_ref[0]
        hb = h.astype(MXU_DTYPE)
        h_s[...] = hb
        zg_ref[...] = jnp.dot(hb, ws_ref[0], preferred_element_type=F32)

    z_ref[...] = jnp.dot(h_s[...], w_ref[0], preferred_element_type=F32)


def _norm_in_proj(x2d, norm_g, mod3, w_wide, layer, w_gate, seq):
    n, d = x2d.shape
    tm, tn = 1024, 1024
    per_b = seq // tm
    return pl.pallas_call(
        _norm_in_kernel,
        grid=(n // tm, D_WIDE // tn),
        in_specs=[pl.BlockSpec((tm, d), lambda i, j: (i, 0)),
                  pl.BlockSpec((1, d), lambda i, j: (0, 0)),
                  pl.BlockSpec((1, 1, d), lambda i, j: (i // per_b, 0, 1)),
                  pl.BlockSpec((1, 1, d), lambda i, j: (i // per_b, 0, 0)),
                  pl.BlockSpec((1, d, tn), lambda i, j: (layer, 0, j)),
                  pl.BlockSpec((1, d, GATE_W), lambda i, j: (layer, 0, 0))],
        out_specs=[pl.BlockSpec((tm, tn), lambda i, j: (i, j)),
                   pl.BlockSpec((tm, GATE_W), lambda i, j: (i, 0))],
        out_shape=[jax.ShapeDtypeStruct((n, D_WIDE), F32),
                   jax.ShapeDtypeStruct((n, GATE_W), F32)],
        scratch_shapes=[pltpu.VMEM((tm, d), MXU_DTYPE)],
        compiler_params=_cparams(("arbitrary", "arbitrary")),
        name="norm_in_proj",
    )(x2d, norm_g.reshape(1, d), mod3, mod3, w_wide, w_gate)


REC_HEADS_PER_STEP = 2


def _mlstm_kernel(gb_ref, q_ref, k_ref, v_ref, o_ref, zg_ref, cwq_ref, cwk_ref, og_ref, y_ref,
                  pad_s, qs_s, ks_s, ic_s, fc_s):
    hp = pl.program_id(1)
    t = q_ref.shape[1]
    cl = MLSTM_CHUNK
    nc = t // cl
    hps = REC_HEADS_PER_STEP

    def conv_silu(src_ref, cw_ref, cols, dst_s, scale):
        pad_s[8:8 + t, :] = src_ref[0, :, cols]
        off = 8 - (MLSTM_CONV - 1)
        tile = 128
        for r0 in range(0, t, tile):
            acc = cw_ref[0:1, cols] * pad_s[r0 + off:r0 + off + tile, :]
            for j in range(1, MLSTM_CONV):
                acc = acc + cw_ref[j:j + 1, cols] * pad_s[r0 + off + j:r0 + off + j + tile, :]
            dst_s[r0:r0 + tile, :] = acc * _sigmoid(acc) * scale

    pad_s[0:8, :] = jnp.zeros((8, HEAD_DIM), F32)
    zg = zg_ref[0]
    for hh in range(hps):
        h = hp * hps + hh
        conv_silu(q_ref, cwq_ref, _head_cols(hh), qs_s.at[hh], HEAD_DIM ** -0.5)
        conv_silu(k_ref, cwk_ref, _head_cols(hh), ks_s.at[hh], 1.0)
        ic_s[hh] = _lane_col(zg, GL_MI + h) + gb_ref[0, h]
        f_pre = _lane_col(zg, GL_MF + h) + gb_ref[1, h]
        fc_s[hh] = jnp.minimum(f_pre, 0.0) - jnp.log1p(jnp.exp(-jnp.abs(f_pre)))

    rr = lax.broadcasted_iota(jnp.int32, (cl, cl), 0)
    cc = lax.broadcasted_iota(jnp.int32, (cl, cl), 1)
    eye = rr == cc
    causal = cc <= rr
    og = og_ref[...]

    def head_step(hh, c, carry):
        c_st, n_st, m_st = carry
        sl = pl.ds(pl.multiple_of(c * cl, cl), cl)
        cols = _head_cols(hh)
        qc = qs_s[hh, sl, :]
        kc = ks_s[hh, sl, :]
        vc = v_ref[0, sl, cols]
        i_col = ic_s[hh, sl, :]
        f_col = fc_s[hh, sl, :]
        f_row = jnp.sum(jnp.where(eye, f_col, 0.0), axis=0, keepdims=True)
        i_row = jnp.sum(jnp.where(eye, i_col, 0.0), axis=0, keepdims=True)
        a_col = jnp.sum(jnp.where(causal, f_row, 0.0), axis=1, keepdims=True)
        a_row = jnp.sum(jnp.where(rr <= cc, f_col, 0.0), axis=0, keepdims=True)
        log_d = jnp.where(causal, a_col - a_row + i_row, NEG)
        m_inter = a_col + m_st
        m_row = jnp.maximum(m_inter, jnp.max(log_d, axis=1, keepdims=True))
        s = _dot_nt(qc, kc) * jnp.exp(log_d - m_row)
        w_inter = jnp.exp(m_inter - m_row)
        num = _dot(s, vc) + w_inter * _dot(qc, c_st)
        den = jnp.sum(s, axis=1, keepdims=True) + w_inter * jnp.sum(qc * n_st, axis=1, keepdims=True)
        h_out = num / jnp.maximum(jnp.abs(den), jnp.exp(-m_row))
        a_last = jnp.sum(f_col, axis=0, keepdims=True)
        w_log = a_last - a_col + i_col
        m_new = jnp.maximum(a_last + m_st, jnp.max(w_log, axis=0, keepdims=True))
        w = jnp.exp(w_log - m_new)
        decay = jnp.exp(a_last + m_st - m_new)
        kw = kc * w
        c_new = decay * c_st + _dot_tn(kw, vc)
        n_new = decay * n_st + jnp.sum(kw, axis=0, keepdims=True)
        y = _rms(h_out, og) * _sigmoid(o_ref[0, sl, cols])
        y_ref[0, sl, cols] = y.astype(y_ref.dtype)
        return c_new, n_new, m_new

    def body(c, carry):
        return tuple(head_step(hh, c, carry[hh]) for hh in range(hps))

    carry0 = (jnp.zeros((HEAD_DIM, HEAD_DIM), F32), jnp.zeros((1, HEAD_DIM), F32), jnp.zeros((1, 1), F32))
    lax.fori_loop(0, nc, body, (carry0,) * hps, unroll=2)


def _col_spec(seq, group, hps):
    return pl.BlockSpec((1, seq, hps * HEAD_DIM), lambda b, h: (b, 0, group * (HEADS // hps) + h))


def _mlstm(z3, zg3, gate_b, conv_w, out_g):
    bsz, seq, _ = z3.shape
    hps = REC_HEADS_PER_STEP
    wide = hps * HEAD_DIM
    per_group = HEADS // hps
    smem = pl.BlockSpec(memory_space=pltpu.SMEM)
    return pl.pallas_call(
        _mlstm_kernel,
        grid=(bsz, per_group),
        in_specs=[smem,
                  _col_spec(seq, G_MQ, hps), _col_spec(seq, G_MK, hps), _col_spec(seq, G_MV, hps),
                  _col_spec(seq, G_MO, hps),
                  pl.BlockSpec((1, seq, GATE_W), lambda b, h: (b, 0, 0)),
                  pl.BlockSpec((MLSTM_CONV, wide), lambda b, h: (0, h)),
                  pl.BlockSpec((MLSTM_CONV, wide), lambda b, h: (0, per_group + h)),
                  pl.BlockSpec((1, HEAD_DIM), lambda b, h: (0, 0))],
        out_specs=pl.BlockSpec((1, seq, wide), lambda b, h: (b, 0, h)),
        out_shape=jax.ShapeDtypeStruct((bsz, seq, GROUP_W), BF16),
        scratch_shapes=[pltpu.VMEM((seq + 8, HEAD_DIM), F32), pltpu.VMEM((hps, seq, HEAD_DIM), F32),
                        pltpu.VMEM((hps, seq, HEAD_DIM), F32), pltpu.VMEM((hps, seq, 1), F32),
                        pltpu.VMEM((hps, seq, 1), F32)],
        compiler_params=_cparams(("arbitrary", "arbitrary")),
        name="mlstm",
    )(gate_b, z3, z3, z3, z3, zg3, conv_w, conv_w, out_g.reshape(1, HEAD_DIM))


def _ret_kernel(lg_ref, q_ref, k_ref, v_ref, g_ref, og_ref, y_ref):
    t = q_ref.shape[1]
    cl = RET_CHUNK
    rr = lax.broadcasted_iota(jnp.int32, (cl, cl), 0)
    cc = lax.broadcasted_iota(jnp.int32, (cl, cl), 1)
    diff = (rr - cc).astype(F32)
    jcol = lax.broadcasted_iota(jnp.int32, (cl, 1), 0).astype(F32)
    og = og_ref[...]
    consts = []
    for h in range(HEADS):
        lg = lg_ref[h]
        consts.append(dict(decay_in=jnp.where(diff >= 0, jnp.exp(lg * jnp.maximum(diff, 0.0)), 0.0),
                           zeta=jnp.exp(lg * (cl - 1.0 - jcol)), xi=jnp.exp(lg * (jcol + 1.0)),
                           g_chunk=jnp.exp(jnp.full((1, 1), lg * cl, F32))))

    def head_step(h, c, r_st):
        sl = pl.ds(pl.multiple_of(c * cl, cl), cl)
        cols = _head_cols(h)
        qc = q_ref[0, sl, cols]
        kc = k_ref[0, sl, cols] * HEAD_DIM ** -0.5
        vc = v_ref[0, sl, cols]
        scores = _dot_nt(qc, kc) * consts[h]["decay_in"]
        o = _dot(scores, vc) + _dot(qc, r_st) * consts[h]["xi"]
        r_new = consts[h]["g_chunk"] * r_st + _dot_tn(kc * consts[h]["zeta"], vc)
        gg = g_ref[0, sl, cols]
        y = _rms(o, og) * (gg * _sigmoid(gg))
        y_ref[0, sl, cols] = y.astype(y_ref.dtype)
        return r_new

    def body(c, carry):
        return tuple(head_step(h, c, carry[h]) for h in range(HEADS))

    lax.fori_loop(0, t // cl, body, (jnp.zeros((HEAD_DIM, HEAD_DIM), F32),) * HEADS, unroll=2)


def _retention(z3, log_gamma, out_g):
    bsz, seq, _ = z3.shape
    grp = lambda group: pl.BlockSpec((1, seq, GROUP_W), lambda b: (b, 0, group))
    return pl.pallas_call(
        _ret_kernel,
        grid=(bsz,),
        in_specs=[pl.BlockSpec(memory_space=pltpu.SMEM), grp(G_RQ), grp(G_RK), grp(G_RV), grp(G_RG),
                  pl.BlockSpec((1, HEAD_DIM), lambda b: (0, 0))],
        out_specs=pl.BlockSpec((1, seq, GROUP_W), lambda b: (b, 0, 0)),
        out_shape=jax.ShapeDtypeStruct((bsz, seq, GROUP_W), BF16),
        compiler_params=_cparams(("arbitrary",)),
        name="retention",
    )(log_gamma, z3, z3, z3, z3, out_g.reshape(1, HEAD_DIM))


BF16_SUBLANES = 16
ATT_TQ = 256
STRIP_W = 512
STRIP_SHIFT = 9
HEADS_PER_STEP = 2


def _rank_rows(vals):
    n = vals.shape[0]
    rowb = lax.broadcasted_iota(jnp.int32, vals.shape, 0)
    rank = jnp.zeros(vals.shape, jnp.int32)
    for jp in range(n):
        rv = vals[jp:jp + 1, :]
        beats = (rv > vals) | ((rv == vals) & (rowb > jp))
        rank = rank + beats.astype(jnp.int32)
    return rank


def _top_rows(vals, k):
    rowb = lax.broadcasted_iota(jnp.int32, vals.shape, 0).astype(F32)
    picked = jnp.zeros(vals.shape, jnp.bool_)
    for _ in range(k):
        best = jnp.max(vals, axis=0, keepdims=True)
        first = jnp.min(jnp.where(vals == best, rowb, float(vals.shape[0])), axis=0, keepdims=True)
        hit = (rowb == first) & (best > NEG)
        picked = picked | hit
        vals = jnp.where(hit, NEG, vals)
    return picked


def _pad_rows(x, rows):
    return jnp.concatenate([x, jnp.zeros((rows - x.shape[0], x.shape[1]), x.dtype)], axis=0)


def _fold_lanes(op, acc, x):
    for b in range(x.shape[1] // HEAD_DIM):
        acc = op(acc, x[:, b * HEAD_DIM:(b + 1) * HEAD_DIM])
    return acc


def _block_masked_attention(heads, qi, strip_s):
    tq, w = ATT_TQ, STRIP_W
    scale = HEAD_DIM ** -0.5
    t0 = qi * tq
    last = jnp.right_shift(t0, STRIP_SHIFT)
    rr = lax.broadcasted_iota(jnp.int32, (tq, w), 0)
    cc = lax.broadcasted_iota(jnp.int32, (tq, w), 1)
    rel = rr - cc
    rel_f = rel.astype(F32)
    alibi = [(-hd["slope"]) * rel_f for hd in heads]
    n_blk = heads[0]["unsel"].shape[0]
    erow = lax.broadcasted_iota(jnp.int32, (n_blk, w), 0)
    ecol = lax.broadcasted_iota(jnp.int32, (n_blk, w), 1)

    def scores(hd, c, bias):
        expand = jnp.where(erow == jnp.right_shift(c * w + ecol, hd["blk_shift"]), NEG, 0.0).astype(BF16)
        mask_bias = lax.dot_general(hd["unsel"], expand, (((0,), (0,)), ((), ())), preferred_element_type=F32)
        kc = hd["k_s"][pl.ds(pl.multiple_of(c * w, w), w), :]
        return _dot_nt(hd["qb"], kc) * scale + bias + mask_bias

    def first_pass(c, ms):
        out = []
        for hi, hd in enumerate(heads):
            s = scores(hd, c, alibi[hi] + (-hd["slope"]) * (t0 - c * w).astype(F32))
            strip_s[hi, c] = s
            out.append(_fold_lanes(jnp.maximum, ms[hi], s))
        return tuple(out)

    ms = lax.fori_loop(0, last, first_pass, tuple(jnp.full((tq, HEAD_DIM), NEG, F32) for _ in heads))
    dist = rel + (t0 - last * w)
    row_max = []
    for hi, hd in enumerate(heads):
        bias = jnp.where(dist >= 0, (-hd["slope"]) * dist.astype(F32), NEG)
        s = scores(hd, last, bias)
        strip_s[hi, last] = s
        row_max.append(jnp.max(_fold_lanes(jnp.maximum, ms[hi], s), axis=1, keepdims=True))

    def second_pass(c, carry):
        out = []
        for hi, hd in enumerate(heads):
            l_run, acc = carry[hi]
            p = jnp.exp(strip_s[hi, c] - row_max[hi])
            vc = hd["v_s"][pl.ds(pl.multiple_of(c * w, w), w), :]
            out.append((_fold_lanes(jnp.add, l_run, p), acc + _dot(p, vc)))
        return tuple(out)

    zero = jnp.zeros((tq, HEAD_DIM), F32)
    res = lax.fori_loop(0, last + 1, second_pass, tuple((zero, zero) for _ in heads))
    return [acc / jnp.sum(l_run, axis=1, keepdims=True) for l_run, acc in res]


def _head_cols(hh):
    return slice(hh * HEAD_DIM, (hh + 1) * HEAD_DIM)


def _moba_kernel(slope_ref, q_ref, k_ref, v_ref, g_ref, y_ref, kn_s, vb_s, kmean_s, strip_s):
    hp = pl.program_id(1)
    qi = pl.program_id(2)
    t = k_ref.shape[1]
    blk = MOBA_BLOCK
    nb = t // blk

    @pl.when(qi == 0)
    def _():
        for hh in range(HEADS_PER_STEP):
            kn = _rms(k_ref[0, :, _head_cols(hh)], g_ref[1:2, :])
            kn_s[hh] = kn.astype(kn_s.dtype)
            vb_s[hh] = v_ref[0, :, _head_cols(hh)].astype(vb_s.dtype)
            kmean_s[hh] = jnp.zeros(kmean_s.shape[1:], F32)
            for j in range(nb):
                kmean_s[hh, j:j + 1, :] = jnp.mean(kn[j * blk:(j + 1) * blk, :], axis=0, keepdims=True)

    rowb = lax.broadcasted_iota(jnp.int32, (nb, ATT_TQ), 0)
    heads = []
    for hh in range(HEADS_PER_STEP):
        qn = _rms(q_ref[0, :, _head_cols(hh)], g_ref[0:1, :])
        gate = jnp.where(rowb < qi, _dot_nt(kmean_s[hh], qn)[0:nb, :], NEG)
        sel = (_rank_rows(gate) < MOBA_TOPK) & (rowb < qi)
        unsel = jnp.where(sel | (rowb == qi), 0.0, 1.0)
        heads.append(dict(qb=qn.astype(MXU_DTYPE), k_s=kn_s.at[hh], v_s=vb_s.at[hh],
                          slope=slope_ref[hp * HEADS_PER_STEP + hh], blk_shift=MOBA_SHIFT,
                          unsel=_pad_rows(unsel, BF16_SUBLANES).astype(BF16)))
    outs = _block_masked_attention(heads, qi, strip_s)
    for hh in range(HEADS_PER_STEP):
        y_ref[0, :, _head_cols(hh)] = outs[hh].astype(y_ref.dtype)


def _moba(z3, qk_g, slopes):
    bsz, seq, _ = z3.shape
    assert MOBA_BLOCK == ATT_TQ and seq % STRIP_W == 0
    hps = HEADS_PER_STEP
    wide = hps * HEAD_DIM
    per_group = HEADS // hps
    kv = lambda group: pl.BlockSpec((1, seq, wide), lambda b, h, i: (b, 0, group * per_group + h))
    return pl.pallas_call(
        _moba_kernel,
        grid=(bsz, per_group, seq // ATT_TQ),
        in_specs=[pl.BlockSpec(memory_space=pltpu.SMEM),
                  pl.BlockSpec((1, ATT_TQ, wide), lambda b, h, i: (b, i, G_BQ * per_group + h)),
                  kv(G_BK), kv(G_BV),
                  pl.BlockSpec((2, HEAD_DIM), lambda b, h, i: (0, 0))],
        out_specs=pl.BlockSpec((1, ATT_TQ, wide), lambda b, h, i: (b, i, h)),
        out_shape=jax.ShapeDtypeStruct((bsz, seq, GROUP_W), BF16),
        scratch_shapes=[pltpu.VMEM((hps, seq, HEAD_DIM), MXU_DTYPE), pltpu.VMEM((hps, seq, HEAD_DIM), MXU_DTYPE),
                        pltpu.VMEM((hps, HEAD_DIM, HEAD_DIM), F32),
                        pltpu.VMEM((hps, seq // STRIP_W, ATT_TQ, STRIP_W), F32)],
        compiler_params=_cparams(("arbitrary", "arbitrary", "arbitrary")),
        name="moba",
    )(slopes, z3, z3, z3, qk_g)


def _gelu_tanh(x):
    return 0.5 * x * (1.0 + jnp.tanh(0.7978845608028654 * (x + 0.044715 * (x * x * x))))


def _nsa_kernel(slope_ref, q_ref, kc0_ref, kc1_ref, vc0_ref, vc1_ref, ks_ref, vs_ref, kw_ref, vw_ref, zg_ref,
                qg_ref, kg_ref, pe_ref, w1_ref, w2_ref, y_ref,
                kcmp_s, vcmp_s, ksn_s, vsb_s, kwn_s, vwb_s, strip_s):
    kc_refs, vc_refs = (kc0_ref, kc1_ref), (vc0_ref, vc1_ref)
    hp = pl.program_id(1)
    qi = pl.program_id(2)
    t = ks_ref.shape[1]
    tq = ATT_TQ
    nsub = t // NSA_CMP_STRIDE
    n_cmp = nsub - 1
    n_sel = t // NSA_SEL_BLOCK
    scale = HEAD_DIM ** -0.5

    @pl.when(qi == 0)
    def _():
        for hh in range(HEADS_PER_STEP):
            cols = _head_cols(hh)
            for cv, (src, dst) in enumerate(((kc_refs[hh], kcmp_s), (vc_refs[hh], vcmp_s))):
                acc_a = jnp.zeros((nsub, HEAD_DIM), F32)
                acc_b = jnp.zeros((nsub, HEAD_DIM), F32)
                for r in range(NSA_CMP_STRIDE):
                    zr = src[0, pl.ds(r, nsub, stride=NSA_CMP_STRIDE), :]
                    acc_a = acc_a + _dot(zr + pe_ref[cv, r:r + 1, :], w1_ref[cv, r])
                    rb = NSA_CMP_STRIDE + r
                    acc_b = acc_b + _dot(zr + pe_ref[cv, rb:rb + 1, :], w1_ref[cv, rb])
                hid = _gelu_tanh(acc_a + pltpu.roll(acc_b, nsub - 1, axis=0))
                cmp = _dot(hid, w2_ref[cv])
                if cv == 0:
                    cmp = _rms(cmp, kg_ref[0:1, :])
                dst[hh] = cmp.astype(dst.dtype)
            ksn_s[hh] = _rms(ks_ref[0, :, cols], kg_ref[1:2, :]).astype(ksn_s.dtype)
            vsb_s[hh] = vs_ref[0, :, cols].astype(vsb_s.dtype)
            kwn_s[hh] = _rms(kw_ref[0, :, cols], kg_ref[2:3, :]).astype(kwn_s.dtype)
            vwb_s[hh] = vw_ref[0, :, cols].astype(vwb_s.dtype)

    t0 = qi * tq
    rowi = lax.broadcasted_iota(jnp.int32, (tq, HEAD_DIM), 0)
    lane = lax.broadcasted_iota(jnp.int32, (tq, HEAD_DIM), 1)
    dist_c = (t0 + rowi) - (lane * NSA_CMP_STRIDE + (NSA_CMP_LEN - 1))
    ok_c = (dist_c >= 0) & (lane < n_cmp)
    dist_cf = dist_c.astype(F32)

    ob = lax.broadcasted_iota(jnp.int32, (HEAD_DIM, nsub), 0)
    oc = lax.broadcasted_iota(jnp.int32, (HEAD_DIM, nsub), 1)
    overlap_t = ((oc * NSA_CMP_STRIDE <= ob * NSA_SEL_BLOCK + (NSA_SEL_BLOCK - 1))
                 & (oc * NSA_CMP_STRIDE + (NSA_CMP_LEN - 1) >= ob * NSA_SEL_BLOCK)
                 & (oc < n_cmp) & (ob < n_sel)).astype(F32)
    rowb = lax.broadcasted_iota(jnp.int32, (n_sel, tq), 0)
    cur = jnp.right_shift(t0 + lax.broadcasted_iota(jnp.int32, (n_sel, tq), 1), NSA_SEL_SHIFT)

    win_w = NSA_WINDOW + tq
    k0 = jnp.maximum(t0 - NSA_WINDOW, 0)
    wr = lax.broadcasted_iota(jnp.int32, (tq, win_w), 0)
    wc = lax.broadcasted_iota(jnp.int32, (tq, win_w), 1)
    dist_w = (t0 - k0) + wr - wc
    ok_w = (dist_w >= 0) & (dist_w < NSA_WINDOW)
    dist_wf = dist_w.astype(F32)
    win_rows = pl.ds(pl.multiple_of(k0, tq), win_w)

    heads, o_cmp, o_win = [], [], []
    for hh in range(HEADS_PER_STEP):
        slope = slope_ref[hp * HEADS_PER_STEP + hh]
        qn = _rms(q_ref[0, :, _head_cols(hh)], qg_ref[...])
        qb = qn.astype(MXU_DTYPE)

        s_c = jnp.where(ok_c, _dot_nt(qb, kcmp_s[hh]) * scale - slope * dist_cf, NEG)
        m_c = jnp.max(s_c, axis=1, keepdims=True)
        e_c = jnp.where(ok_c, jnp.exp(s_c - m_c), 0.0)
        p_c = e_c / jnp.maximum(jnp.sum(e_c, axis=1, keepdims=True), 1e-30)
        o_cmp.append(_dot(p_c, vcmp_s[hh]))

        imp = _dot_split_nt(overlap_t, p_c)[0:n_sel, :]
        sel = (rowb == cur) | _top_rows(jnp.where(rowb < cur, imp, NEG), NSA_SEL_TOPN - 1)
        unsel = jnp.where(sel, 0.0, 1.0)
        heads.append(dict(qb=qb, k_s=ksn_s.at[hh], v_s=vsb_s.at[hh], slope=slope, blk_shift=NSA_SEL_SHIFT,
                          unsel=unsel.astype(BF16)))

        s_w = _dot_nt(qb, kwn_s[hh, win_rows, :]) * scale + jnp.where(ok_w, (-slope) * dist_wf, NEG)
        p_w = jnp.exp(s_w - jnp.max(s_w, axis=1, keepdims=True))
        o_win.append(_dot(p_w, vwb_s[hh, win_rows, :]) / jnp.sum(p_w, axis=1, keepdims=True))

    o_slc = _block_masked_attention(heads, qi, strip_s)

    zg = zg_ref[0]
    for hh in range(HEADS_PER_STEP):
        h = hp * HEADS_PER_STEP + hh
        g_cmp = _sigmoid(_lane_col(zg, GL_NG + h))
        g_slc = _sigmoid(_lane_col(zg, GL_NG + HEADS + h))
        g_win = _sigmoid(_lane_col(zg, GL_NG + 2 * HEADS + h))
        y_ref[0, :, _head_cols(hh)] = (g_cmp * o_cmp[hh] + g_slc * o_slc[hh] + g_win * o_win[hh]).astype(y_ref.dtype)


def _nsa(z3, zg3, q_g, k_g, cmp_pos, cmp_w1, cmp_w2, slopes):
    bsz, seq, _ = z3.shape
    assert seq // NSA_CMP_STRIDE == HEAD_DIM, "compressed blocks are laid out on the 128 lanes"
    assert seq % STRIP_W == 0 and seq >= NSA_WINDOW + ATT_TQ
    tq = ATT_TQ
    hps = HEADS_PER_STEP
    wide = hps * HEAD_DIM
    per_group = HEADS // hps
    kv = lambda group: pl.BlockSpec((1, seq, wide), lambda b, h, i: (b, 0, group * per_group + h))
    one = lambda group, hh: pl.BlockSpec((1, seq, HEAD_DIM), lambda b, h, i: (b, 0, group * HEADS + h * hps + hh))
    full = lambda shape: pl.BlockSpec(shape, lambda b, h, i: (0,) * len(shape))
    w1 = cmp_w1.reshape(2, NSA_CMP_LEN, HEAD_DIM, HEAD_DIM).astype(MXU_DTYPE)
    w2 = cmp_w2.astype(MXU_DTYPE)
    seq_buf = lambda: pltpu.VMEM((hps, seq, HEAD_DIM), MXU_DTYPE)
    cmp_buf = lambda: pltpu.VMEM((hps, HEAD_DIM, HEAD_DIM), MXU_DTYPE)
    return pl.pallas_call(
        _nsa_kernel,
        grid=(bsz, per_group, seq // tq),
        in_specs=[pl.BlockSpec(memory_space=pltpu.SMEM),
                  pl.BlockSpec((1, tq, wide), lambda b, h, i: (b, i, G_NQ * per_group + h)),
                  one(G_NKC, 0), one(G_NKC, 1), one(G_NVC, 0), one(G_NVC, 1),
                  kv(G_NKS), kv(G_NVS), kv(G_NKW), kv(G_NVW),
                  pl.BlockSpec((1, tq, GATE_W), lambda b, h, i: (b, i, 0)),
                  full((1, HEAD_DIM)), full((3, HEAD_DIM)), full((2, NSA_CMP_LEN, HEAD_DIM)),
                  full((2, NSA_CMP_LEN, HEAD_DIM, HEAD_DIM)), full((2, HEAD_DIM, HEAD_DIM))],
        out_specs=pl.BlockSpec((1, tq, wide), lambda b, h, i: (b, i, h)),
        out_shape=jax.ShapeDtypeStruct((bsz, seq, GROUP_W), BF16),
        scratch_shapes=[cmp_buf(), cmp_buf(), seq_buf(), seq_buf(), seq_buf(), seq_buf(),
                        pltpu.VMEM((hps, seq // STRIP_W, tq, STRIP_W), F32)],
        compiler_params=_cparams(("arbitrary", "arbitrary", "arbitrary")),
        name="nsa",
    )(slopes, z3, z3, z3, z3, z3, z3, z3, z3, z3, zg3, q_g.reshape(1, HEAD_DIM), k_g, cmp_pos, w1, w2)


def _out_proj_kernel(ym_ref, yb_ref, yr_ref, yn_ref, w_ref, x_ref, g_ref, o_ref):
    acc = jnp.dot(ym_ref[...], w_ref[0, 0:GROUP_W, :], preferred_element_type=F32)
    acc = acc + jnp.dot(yb_ref[...], w_ref[0, GROUP_W:2 * GROUP_W, :], preferred_element_type=F32)
    acc = acc + jnp.dot(yr_ref[...], w_ref[0, 2 * GROUP_W:3 * GROUP_W, :], preferred_element_type=F32)
    acc = acc + jnp.dot(yn_ref[...], w_ref[0, 3 * GROUP_W:4 * GROUP_W, :], preferred_element_type=F32)
    o_ref[...] = x_ref[...] + g_ref[0] * acc


def _out_proj(ys, w_out, layer, x2d, mod3, seq):
    n, d = x2d.shape
    tm, tn = 1024, 1024
    per_b = seq // tm
    y_spec = pl.BlockSpec((tm, GROUP_W), lambda i, j: (i, 0))
    return pl.pallas_call(
        _out_proj_kernel,
        grid=(n // tm, d // tn),
        in_specs=[y_spec, y_spec, y_spec, y_spec,
                  pl.BlockSpec((1, 4 * GROUP_W, tn), lambda i, j: (layer, 0, j)),
                  pl.BlockSpec((tm, tn), lambda i, j: (i, j)),
                  pl.BlockSpec((1, 1, tn), lambda i, j: (i // per_b, 0, 2 * (d // tn) + j))],
        out_specs=pl.BlockSpec((tm, tn), lambda i, j: (i, j)),
        out_shape=jax.ShapeDtypeStruct((n, d), F32),
        compiler_params=_cparams(("arbitrary", "arbitrary")),
        name="out_proj",
    )(*[y.reshape(n, GROUP_W) for y in ys], w_out, x2d, mod3)


SLAB_E0, SLAB_E1, SLAB_R0, SLAB_R1, SLAB_G0, SLAB_G1 = 0, 1, 2, 3, 4, 5


def _route_kernel(x_ref, g_ref, sc_ref, sh_ref, wr_ref, br_ref, h_ref, slab_ref, cnt_ref, carry_s):
    tm = x_ref.shape[0]

    @pl.when(pl.program_id(0) == 0)
    def _():
        carry_s[...] = jnp.zeros(carry_s.shape, F32)

    hmod = _rms(x_ref[...], g_ref[...]) * (1.0 + sc_ref[0]) + sh_ref[0]
    _store_slabs(h_ref, hmod)
    logits = _dot(hmod, wr_ref[...]) + br_ref[...]
    lane = lax.broadcasted_iota(jnp.int32, logits.shape, 1).astype(F32)
    far = 4.0 * GATE_W

    in_g = lane < N_GROUPS
    lg = jnp.where(in_g, logits, NEG)
    g_max = jnp.max(lg, axis=1, keepdims=True)
    grp = jnp.min(jnp.where(in_g & (lg == g_max), lane, far), axis=1, keepdims=True)
    p_grp = 1.0 / jnp.sum(jnp.where(in_g, jnp.exp(lg - g_max), 0.0), axis=1, keepdims=True)

    lo = N_GROUPS + grp * EXPERTS_PER_GROUP
    in_e = (lane >= lo) & (lane < lo + EXPERTS_PER_GROUP)
    le = jnp.where(in_e, logits, NEG)
    e_max = jnp.max(le, axis=1, keepdims=True)
    ee = jnp.where(in_e, jnp.exp(le - e_max), 0.0)
    pe = jnp.where(in_e, ee / jnp.sum(ee, axis=1, keepdims=True), -1.0)
    p1 = jnp.max(pe, axis=1, keepdims=True)
    i1 = jnp.min(jnp.where(pe == p1, lane, far), axis=1, keepdims=True)
    pe2 = jnp.where(lane == i1, -1.0, pe)
    p2 = jnp.max(pe2, axis=1, keepdims=True)
    i2 = jnp.min(jnp.where(pe2 == p2, lane, far), axis=1, keepdims=True)
    e0 = i1 - N_GROUPS
    e1 = i2 - N_GROUPS
    g0 = p_grp * (p1 / (p1 + p2))
    g1 = p_grp * (p2 / (p1 + p2))

    onehot = ((lane == e0) | (lane == e1)).astype(BF16)
    rr = lax.broadcasted_iota(jnp.int32, (tm, tm), 0)
    cc = lax.broadcasted_iota(jnp.int32, (tm, tm), 1)
    before = jnp.dot((cc < rr).astype(BF16), onehot, preferred_element_type=F32) + carry_s[...]
    r0 = jnp.sum(jnp.where(lane == e0, before, 0.0), axis=1, keepdims=True)
    r1 = jnp.sum(jnp.where(lane == e1, before, 0.0), axis=1, keepdims=True)
    carry_s[...] = carry_s[...] + jnp.sum(onehot.astype(F32), axis=0, keepdims=True)
    cnt_ref[...] = carry_s[...]

    slab = jnp.where(lane == SLAB_E0, e0.astype(F32), 0.0)
    slab = jnp.where(lane == SLAB_E1, e1.astype(F32), slab)
    slab = jnp.where(lane == SLAB_R0, r0, slab)
    slab = jnp.where(lane == SLAB_R1, r1, slab)
    slab = jnp.where(lane == SLAB_G0, g0, slab)
    slab = jnp.where(lane == SLAB_G1, g1, slab)
    slab_ref[...] = slab


def _route(x2d, norm_g, mod3, w_router, b_router, seq):
    n, d = x2d.shape
    tm = 512
    per_b = seq // tm
    return pl.pallas_call(
        _route_kernel,
        grid=(n // tm,),
        in_specs=[pl.BlockSpec((tm, d), lambda i: (i, 0)),
                  pl.BlockSpec((1, d), lambda i: (0, 0)),
                  pl.BlockSpec((1, 1, d), lambda i: (i // per_b, 0, 4)),
                  pl.BlockSpec((1, 1, d), lambda i: (i // per_b, 0, 3)),
                  pl.BlockSpec((d, GATE_W), lambda i: (0, 0)),
                  pl.BlockSpec((1, GATE_W), lambda i: (0, 0))],
        out_specs=[pl.BlockSpec((tm * SLAB, HEAD_DIM), lambda i: (i, 0)),
                   pl.BlockSpec((tm, GATE_W), lambda i: (i, 0)),
                   pl.BlockSpec((1, GATE_W), lambda i: (0, 0))],
        out_shape=[jax.ShapeDtypeStruct((n * SLAB, HEAD_DIM), F32),
                   jax.ShapeDtypeStruct((n, GATE_W), F32),
                   jax.ShapeDtypeStruct((1, GATE_W), F32)],
        scratch_shapes=[pltpu.VMEM((1, GATE_W), F32)],
        compiler_params=_cparams(("arbitrary",)),
        name="route",
    )(x2d, norm_g.reshape(1, d), mod3, mod3, w_router, b_router)


def _invert_kernel(dest_ref, pad_lo_ref, pad_hi_ref, inv_ref):
    def clear(i, _):
        inv_ref[i] = 0
        return 0

    def clear_segment(g, _):
        lax.fori_loop(pad_lo_ref[g], pad_hi_ref[g], clear, 0)
        return 0

    def put(a, _):
        inv_ref[dest_ref[a]] = jnp.right_shift(a, 1)
        return 0

    lax.fori_loop(0, pad_lo_ref.shape[0], clear_segment, 0)
    lax.fori_loop(0, dest_ref.shape[0], put, 0, unroll=32)


def _invert(dest_flat, pad_lo, pad_hi, cap):
    smem = pl.BlockSpec(memory_space=pltpu.SMEM)
    return pl.pallas_call(
        _invert_kernel,
        in_specs=[smem, smem, smem],
        out_specs=smem,
        out_shape=jax.ShapeDtypeStruct((cap,), jnp.int32),
        name="invert",
    )(dest_flat, pad_lo, pad_hi)


def _start_row_gather(idx_ref, src_hbm, dst, sem, n_rows, first=0, span=1):
    for r in range(first, n_rows):
        start = idx_ref[r] if span == 1 else pl.multiple_of(idx_ref[r] * span, span)
        pltpu.make_async_copy(src_hbm.at[pl.ds(start, span)], dst.at[pl.ds(r * span, span)], sem).start(
            priority=r % 2)


def _wait_row_gather(src_hbm, dst, sem, n_rows, span=1):
    pltpu.make_async_copy(src_hbm.at[pl.ds(0, n_rows * span)], dst, sem).wait()


SLAB = D_MODEL // HEAD_DIM


def _store_slabs(ref, x):
    for c in range(SLAB):
        ref[pl.ds(c, x.shape[0], stride=SLAB), :] = x[:, c * HEAD_DIM:(c + 1) * HEAD_DIM]


def _load_slabs(ref, n_rows, lead=()):
    return jnp.concatenate([ref[lead + (pl.ds(c, n_rows, stride=SLAB), slice(None))] for c in range(SLAB)], axis=1)


def _expert_kernel(blk_e_ref, n_used_ref, run_slot_ref, next_e_ref, next_ok_ref, inv_ref, h_hbm,
                   w1_hbm, w3_hbm, w2_hbm, y_ref,
                   x_s, w1_f, w3_f, w2_f, w1_s, w3_s, w2_s, sem, wsem, *, layer):
    s = pl.program_id(0)
    rows = EXPERT_ROWS
    n_used = n_used_ref[0]
    ahead = EXPERT_GATHER_AHEAD
    slot = lax.rem(s, ahead + 1)
    cslot = lax.rem(s + 1, ahead + 1)

    blk = s - ahead
    prev = jnp.maximum(blk - 1, 0)
    gather = s < n_used
    compute = (s >= ahead) & (blk < n_used)

    def start_gather(first, stop):
        _start_row_gather(inv_ref, h_hbm, x_s.at[slot], sem.at[slot], stop, first, span=SLAB)

    def weight_copies(e, wslot):
        return (pltpu.make_async_copy(w1_hbm.at[layer, e], w1_f.at[wslot], wsem.at[wslot, 0]),
                pltpu.make_async_copy(w3_hbm.at[layer, e], w3_f.at[wslot], wsem.at[wslot, 1]),
                pltpu.make_async_copy(w2_hbm.at[layer, e], w2_f.at[wslot], wsem.at[wslot, 2]))

    def expert_block(gather_too):
        @pl.when((blk == 0) | (blk_e_ref[blk] != blk_e_ref[prev]))
        def _():
            wslot = run_slot_ref[blk]
            for cp in weight_copies(blk_e_ref[blk], wslot):
                cp.wait()

            @pl.when(next_ok_ref[blk] == 1)
            def _():
                for cp in weight_copies(next_e_ref[blk], 1 - wslot):
                    cp.start()

            w1_s[...] = w1_f[wslot].astype(w1_s.dtype)
            w3_s[...] = w3_f[wslot].astype(w3_s.dtype)
            w2_s[...] = w2_f[wslot].astype(w2_s.dtype)

        cuts = (0, rows // 4, rows // 2, rows) if gather_too else (0, 0, 0, 0)
        _wait_row_gather(h_hbm, x_s.at[cslot], sem.at[cslot], rows, span=SLAB)
        start_gather(cuts[0], cuts[1])
        x = _load_slabs(x_s, rows, (cslot,)).astype(MXU_DTYPE)
        a = jnp.dot(x, w1_s[...], preferred_element_type=F32)
        start_gather(cuts[1], cuts[2])
        b = jnp.dot(x, w3_s[...], preferred_element_type=F32)
        start_gather(cuts[2], cuts[3])
        y_ref[...] = _dot(a * _sigmoid(a) * b, w2_s[...])

    @pl.when(gather & compute)
    def _():
        expert_block(True)

    @pl.when(gather & jnp.logical_not(compute))
    def _():
        @pl.when(s == 0)
        def _():
            for cp in weight_copies(blk_e_ref[0], run_slot_ref[0]):
                cp.start()

        start_gather(0, rows)

    @pl.when(compute & jnp.logical_not(gather))
    def _():
        expert_block(False)

    @pl.when((s >= ahead) & (blk >= n_used))
    def _():
        y_ref[...] = jnp.zeros(y_ref.shape, y_ref.dtype)


def _experts(h2, inv, blk_e, n_used, w1, w3, w2, layer):
    d = D_MODEL
    rows = EXPERT_ROWS
    n_blocks = inv.shape[0] // rows
    idx = jnp.arange(n_blocks, dtype=jnp.int32)
    run_start = jnp.concatenate([jnp.ones((1,), jnp.int32), (blk_e[1:] != blk_e[:-1]).astype(jnp.int32)])
    run_slot = (jnp.cumsum(run_start) - 1) % 2
    later_run = (idx[None, :] > idx[:, None]) & (blk_e[None, :] != blk_e[:, None])
    next_start = jnp.min(jnp.where(later_run, idx[None, :], n_blocks), axis=1)
    next_e = blk_e[jnp.minimum(next_start, n_blocks - 1)]
    next_ok = (next_start < n_used[0]).astype(jnp.int32)

    ahead = EXPERT_GATHER_AHEAD
    done = lambda s: jnp.maximum(s - ahead, 0)
    hbm = pl.BlockSpec(memory_space=pl.ANY)
    return pl.pallas_call(
        functools.partial(_expert_kernel, layer=layer),
        grid_spec=pltpu.PrefetchScalarGridSpec(
            num_scalar_prefetch=5,
            grid=(n_blocks + ahead,),
            in_specs=[pl.BlockSpec((rows,), lambda s, *_: (jnp.minimum(s, n_blocks - 1),), memory_space=pltpu.SMEM),
                      hbm, hbm, hbm, hbm],
            out_specs=pl.BlockSpec((rows, d), lambda s, *_: (done(s), 0)),
            scratch_shapes=[pltpu.VMEM((ahead + 1, rows * SLAB, HEAD_DIM), F32),
                            pltpu.VMEM((2, d, D_EXPERT), F32), pltpu.VMEM((2, d, D_EXPERT), F32),
                            pltpu.VMEM((2, D_EXPERT, d), F32),
                            pltpu.VMEM((d, D_EXPERT), MXU_DTYPE), pltpu.VMEM((d, D_EXPERT), MXU_DTYPE),
                            pltpu.VMEM((D_EXPERT, d), MXU_DTYPE),
                            pltpu.SemaphoreType.DMA((ahead + 1,)), pltpu.SemaphoreType.DMA((2, 3))]),
        out_shape=jax.ShapeDtypeStruct((n_blocks * rows, d), F32),
        compiler_params=_cparams(("arbitrary",)),
        name="experts",
    )(blk_e, n_used, run_slot.astype(jnp.int32), next_e.astype(jnp.int32), next_ok, inv, h2, w1, w3, w2)


def _combine_kernel(d0_ref, d1_ref, x_ref, g_ref, slab_ref, yb_hbm, o_ref, rows_s, sem):
    s = pl.program_id(0)
    n_tiles = pl.num_programs(0) - 1
    tm = x_ref.shape[0]
    slot = lax.rem(s, 2)

    @pl.when(s < n_tiles)
    def _():
        _start_row_gather(d0_ref, yb_hbm, rows_s.at[slot, 0], sem.at[slot], tm)
        _start_row_gather(d1_ref, yb_hbm, rows_s.at[slot, 1], sem.at[slot], tm)

    @pl.when(s >= 1)
    def _():
        _wait_row_gather(yb_hbm, rows_s.at[1 - slot, 0], sem.at[1 - slot], tm)
        _wait_row_gather(yb_hbm, rows_s.at[1 - slot, 1], sem.at[1 - slot], tm)
        route = slab_ref[...]
        g0 = route[:, SLAB_G0:SLAB_G0 + 1]
        g1 = route[:, SLAB_G1:SLAB_G1 + 1]
        o_ref[...] = x_ref[...] + g_ref[0] * (g0 * rows_s[1 - slot, 0] + g1 * rows_s[1 - slot, 1])


def _combine(x2d, mod3, slab, dest0, dest1, yb, seq):
    n, d = x2d.shape
    tm = 256
    per_b = seq // tm
    n_tiles = n // tm
    done = lambda s: jnp.maximum(s - 1, 0)
    idx_spec = pl.BlockSpec((tm,), lambda s: (jnp.minimum(s, n_tiles - 1),), memory_space=pltpu.SMEM)
    return pl.pallas_call(
        _combine_kernel,
        grid=(n_tiles + 1,),
        in_specs=[idx_spec, idx_spec,
                  pl.BlockSpec((tm, d), lambda s: (done(s), 0)),
                  pl.BlockSpec((1, 1, d), lambda s: (done(s) // per_b, 0, 5)),
                  pl.BlockSpec((tm, GATE_W), lambda s: (done(s), 0)),
                  pl.BlockSpec(memory_space=pl.ANY)],
        out_specs=pl.BlockSpec((tm, d), lambda s: (done(s), 0)),
        out_shape=jax.ShapeDtypeStruct((n, d), F32),
        scratch_shapes=[pltpu.VMEM((2, 2, tm, d), F32), pltpu.SemaphoreType.DMA((2,))],
        compiler_params=_cparams(("arbitrary",)),
        name="combine",
    )(dest0, dest1, x2d, mod3, slab, yb)


_OFF_MI = 4 * GROUP_W
_OFF_BQ = _OFF_MI + 2 * HEADS
_OFF_NG = _OFF_BQ + 14 * GROUP_W


def _pack_w_kernel(w_ref, o_ref, og_ref):
    gap = _OFF_BQ - _OFF_MI
    o_ref[0, :, 0:_OFF_MI] = w_ref[0, :, 0:_OFF_MI].astype(o_ref.dtype)
    tail = w_ref[0, :, _OFF_MI:]
    width = tail.shape[1]
    o_ref[0, :, _OFF_MI:] = pltpu.roll(tail, width - gap, axis=1)[:, 0:D_WIDE - _OFF_MI].astype(o_ref.dtype)
    lane = lax.broadcasted_iota(jnp.int32, (w_ref.shape[1], GATE_W), 1)
    first = w_ref[0, :, _OFF_MI:_OFF_MI + GATE_W]
    ragged = w_ref[0, :, _OFF_NG - GL_NG:_OFF_NG - GL_NG + GATE_W]
    gate = jnp.where(lane < GL_NG, first, jnp.where(lane < GL_NG + 3 * HEADS, ragged, 0.0))
    og_ref[0] = gate.astype(og_ref.dtype)


def _pack_w_in(w_in):
    depth, d, d_in = w_in.shape
    tk = 128
    lanes_in = -(-d_in // GATE_W) * GATE_W
    assert (_OFF_NG - GL_NG) % GATE_W == 0 and _OFF_NG - GL_NG + GATE_W == lanes_in
    return pl.pallas_call(
        _pack_w_kernel,
        grid=(depth, d // tk),
        in_specs=[pl.BlockSpec((1, tk, lanes_in), lambda l, i: (l, i, 0))],
        out_specs=[pl.BlockSpec((1, tk, D_WIDE), lambda l, i: (l, i, 0)),
                   pl.BlockSpec((1, tk, GATE_W), lambda l, i: (l, i, 0))],
        out_shape=[jax.ShapeDtypeStruct((depth, d, D_WIDE), MXU_DTYPE),
                   jax.ShapeDtypeStruct((depth, d, GATE_W), MXU_DTYPE)],
        compiler_params=_cparams(("arbitrary", "arbitrary")),
        name="pack_w_in",
    )(w_in)


def _moe(x2d, norm_g, mod3, wg, bg, we, be, w1, w3, w2, layer, seq):
    n, d = x2d.shape
    n_route = N_GROUPS + N_EXPERTS
    w_router = jnp.concatenate([wg, we, jnp.zeros((d, GATE_W - n_route), wg.dtype)], axis=1).astype(MXU_DTYPE)
    b_router = jnp.concatenate([bg, be, jnp.zeros((GATE_W - n_route,), bg.dtype)]).reshape(1, GATE_W)
    h2, slab, cnt = _route(x2d, norm_g, mod3, w_router, b_router, seq)

    rows = EXPERT_ROWS
    counts = cnt[0, :N_EXPERTS].astype(jnp.int32)
    pcounts = (counts + rows - 1) // rows * rows
    pends = jnp.cumsum(pcounts)
    pstarts = pends - pcounts
    eid = slab[:, SLAB_E0:SLAB_E1 + 1].astype(jnp.int32)
    rank = slab[:, SLAB_R0:SLAB_R1 + 1].astype(jnp.int32)
    expert_ids = jnp.arange(N_EXPERTS, dtype=jnp.int32)
    dest = jnp.sum(jnp.where(eid[..., None] == expert_ids, pstarts, 0), axis=-1) + rank
    n_blocks = -(-2 * n // rows) + N_EXPERTS
    blk_row0 = jnp.arange(n_blocks, dtype=jnp.int32) * rows
    blk_e = jnp.minimum(jnp.sum((pends[None, :] <= blk_row0[:, None]).astype(jnp.int32), axis=1), N_EXPERTS - 1)
    n_used = (pends[-1:] // rows).astype(jnp.int32)

    cap = n_blocks * rows
    pad_lo = jnp.concatenate([pstarts + counts, pends[-1:]]).astype(jnp.int32)
    pad_hi = jnp.concatenate([pends, jnp.full((1,), cap, jnp.int32)]).astype(jnp.int32)
    inv = _invert(dest.reshape(-1), pad_lo, pad_hi, cap)
    yb = _experts(h2, inv, blk_e, n_used, w1, w3, w2, layer)
    return _combine(x2d, mod3, slab, dest[:, 0], dest[:, 1], yb, seq)


def _layer(x2d, mod, bsz, seq, layer, norm1_g, norm2_g, w_wide, w_gate, mlstm_gate_b, mlstm_conv_w, mlstm_out_g,
           moba_qk_g, ret_out_g, nsa_q_g, nsa_k_g, nsa_cmp_pos, nsa_cmp_w1, nsa_cmp_w2, w_out, router_g_w,
           router_g_b, router_e_w, router_e_b, exp_w1, exp_w3, exp_w2, slopes, log_gamma):
    n, d = x2d.shape
    mod3 = mod.reshape(bsz, 1, 6 * d)
    z, zg = _norm_in_proj(x2d, norm1_g, mod3, w_wide, layer, w_gate, seq)
    z3 = z.reshape(bsz, seq, D_WIDE)
    zg3 = zg.reshape(bsz, seq, GATE_W)
    y_m = _mlstm(z3, zg3, mlstm_gate_b, mlstm_conv_w, mlstm_out_g)
    y_b = _moba(z3, moba_qk_g, slopes[0::2])
    y_r = _retention(z3, log_gamma, ret_out_g)
    y_n = _nsa(z3, zg3, nsa_q_g, nsa_k_g, nsa_cmp_pos, nsa_cmp_w1, nsa_cmp_w2, slopes[1::2])
    x2d = _out_proj((y_m, y_b, y_r, y_n), w_out, layer, x2d, mod3, seq)
    return _moe(x2d, norm2_g, mod3, router_g_w, router_g_b, router_e_w, router_e_b, exp_w1, exp_w3, exp_w2,
                layer, seq)


def kernel(x, c, norm1_g, norm2_g, ada_w, ada_b, w_in, mlstm_gate_b, mlstm_conv_w, mlstm_out_g, moba_qk_g,
           ret_out_g, nsa_q_g, nsa_k_g, nsa_cmp_pos, nsa_cmp_w1, nsa_cmp_w2, w_out, router_g_w, router_g_b,
           router_e_w, router_e_b, exp_w1, exp_w3, exp_w2):
    bsz, seq, d = x.shape
    depth = ada_w.shape[0]
    n_softmax_heads = 2 * HEADS
    slopes = jnp.exp2(-8.0 * jnp.arange(1, n_softmax_heads + 1, dtype=F32) / n_softmax_heads)
    log_gamma = jnp.log(1.0 - jnp.exp2(-5.0 - jnp.arange(HEADS, dtype=F32)))
    mod = _ada_mod(c, ada_w, ada_b)
    w_wide, w_gate = _pack_w_in(w_in)
    w_out_b = w_out.astype(MXU_DTYPE)
    x2d = x.reshape(bsz * seq, d)
    for l in range(depth):
        x2d = _layer(x2d, mod[l], bsz, seq, l, norm1_g[l], norm2_g[l], w_wide, w_gate, mlstm_gate_b[l],
                     mlstm_conv_w[l], mlstm_out_g[l], moba_qk_g[l], ret_out_g[l], nsa_q_g[l], nsa_k_g[l],
                     nsa_cmp_pos[l], nsa_cmp_w1[l], nsa_cmp_w2[l], w_out_b, router_g_w[l], router_g_b[l],
                     router_e_w[l], router_e_b[l], exp_w1, exp_w3, exp_w2, slopes, log_gamma)
    return x2d.reshape(bsz, seq, d)
```

```python
import functools

import jax
import jax.numpy as jnp
from jax import lax
from jax.experimental import pallas as pl
from jax.experimental.pallas import tpu as pltpu

F32 = jnp.float32
BF16 = jnp.bfloat16
MXU_DTYPE = jnp.bfloat16

D_MODEL = 2048
HEAD_DIM = 128
HEADS = 4
GROUP_W = HEADS * HEAD_DIM
N_WIDE_GROUPS = 18
D_WIDE = N_WIDE_GROUPS * GROUP_W
GATE_W = 128

MLSTM_CHUNK = 64
MLSTM_CONV = 4
MOBA_BLOCK = 256
MOBA_SHIFT = 8
MOBA_TOPK = 3
RET_CHUNK = 128
NSA_CMP_LEN = 32
NSA_CMP_STRIDE = 16
NSA_SEL_BLOCK = 64
NSA_SEL_SHIFT = 6
NSA_SEL_TOPN = 4
NSA_WINDOW = 512
N_GROUPS = 4
EXPERTS_PER_GROUP = 8
N_EXPERTS = N_GROUPS * EXPERTS_PER_GROUP
D_EXPERT = 512
EXPERT_ROWS = 256
EXPERT_GATHER_AHEAD = 2

NORM_EPS = 1e-6
NEG = -1e30
BIG = 1e9

G_MQ, G_MK, G_MV, G_MO = 0, 1, 2, 3
G_BQ, G_BK, G_BV = 4, 5, 6
G_RQ, G_RK, G_RV, G_RG = 7, 8, 9, 10
G_NQ, G_NKC, G_NVC, G_NKS, G_NVS, G_NKW, G_NVW = 11, 12, 13, 14, 15, 16, 17
GL_MI, GL_MF, GL_NG = 0, 4, 8

VMEM_LIMIT = 56 * 1024 * 1024


def _cparams(sem):
    return pltpu.CompilerParams(dimension_semantics=sem, vmem_limit_bytes=VMEM_LIMIT)


def _dot(a, b):
    return jnp.dot(a.astype(MXU_DTYPE), b.astype(MXU_DTYPE), preferred_element_type=F32)


def _dot_nt(a, b):
    return lax.dot_general(a.astype(MXU_DTYPE), b.astype(MXU_DTYPE), (((1,), (1,)), ((), ())),
                           preferred_element_type=F32)


def _dot_tn(a, b):
    return lax.dot_general(a.astype(MXU_DTYPE), b.astype(MXU_DTYPE), (((0,), (0,)), ((), ())),
                           preferred_element_type=F32)


def _dot_split_nt(b01, a):
    a_hi = a.astype(MXU_DTYPE)
    r1 = a - a_hi.astype(F32)
    a_mid = r1.astype(MXU_DTYPE)
    a_lo = (r1 - a_mid.astype(F32)).astype(MXU_DTYPE)
    return _dot_nt(b01, a_hi) + _dot_nt(b01, a_mid) + _dot_nt(b01, a_lo)


def _rms(x, g):
    return x * lax.rsqrt(jnp.mean(x * x, axis=-1, keepdims=True) + NORM_EPS) * g


def _sigmoid(x):
    return jax.nn.sigmoid(x)


def _lane_col(x, idx):
    lane = lax.broadcasted_iota(jnp.int32, x.shape, 1)
    return jnp.sum(jnp.where(lane == idx, x, 0.0), axis=1, keepdims=True)


def _ada_kernel(c_ref, w_ref, b_ref, o_ref):
    c = c_ref[...]
    o_ref[0] = _dot(c * _sigmoid(c), w_ref[0]) + b_ref[0]


def _ada_mod(c, ada_w, ada_b):
    depth, d, n6 = ada_w.shape
    b = c.shape[0]
    tn = 1024
    return pl.pallas_call(
        _ada_kernel,
        grid=(depth, n6 // tn),
        in_specs=[pl.BlockSpec((b, d), lambda l, j: (0, 0)),
                  pl.BlockSpec((1, d, tn), lambda l, j: (l, 0, j)),
                  pl.BlockSpec((1, 1, tn), lambda l, j: (l, 0, j))],
        out_specs=pl.BlockSpec((1, b, tn), lambda l, j: (l, 0, j)),
        out_shape=jax.ShapeDtypeStruct((depth, b, n6), F32),
        compiler_params=_cparams(("arbitrary", "arbitrary")),
        name="ada_mod",
    )(c, ada_w, ada_b.reshape(depth, 1, n6))


def _norm_in_kernel(x_ref, g_ref, sc_ref, sh_ref, w_ref, ws_ref, z_ref, zg_ref, h_s):
    @pl.when(pl.program_id(1) == 0)
    def _():
        h = _rms(x_ref[...], g_ref[...]) * (1.0 + sc_ref[0]) + sh_ref[0]
        hb = h.astype(MXU_DTYPE)
        h_s[...] = hb
        zg_ref[...] = jnp.dot(hb, ws_ref[0], preferred_element_type=F32)

    z_ref[...] = jnp.dot(h_s[...], w_ref[0], preferred_element_type=F32)


def _norm_in_proj(x2d, norm_g, mod3, w_wide, layer, w_gate, seq):
    n, d = x2d.shape
    tm, tn = 1024, 1024
    per_b = seq // tm
    return pl.pallas_call(
        _norm_in_kernel,
        grid=(n // tm, D_WIDE // tn),
        in_specs=[pl.BlockSpec((tm, d), lambda i, j: (i, 0)),
                  pl.BlockSpec((1, d), lambda i, j: (0, 0)),
                  pl.BlockSpec((1, 1, d), lambda i, j: (i // per_b, 0, 1)),
                  pl.BlockSpec((1, 1, d), lambda i, j: (i // per_b, 0, 0)),
                  pl.BlockSpec((1, d, tn), lambda i, j: (layer, 0, j)),
                  pl.BlockSpec((1, d, GATE_W), lambda i, j: (layer, 0, 0))],
        out_specs=[pl.BlockSpec((tm, tn), lambda i, j: (i, j)),
                   pl.BlockSpec((tm, GATE_W), lambda i, j: (i, 0))],
        out_shape=[jax.ShapeDtypeStruct((n, D_WIDE), F32),
                   jax.ShapeDtypeStruct((n, GATE_W), F32)],
        scratch_shapes=[pltpu.VMEM((tm, d), MXU_DTYPE)],
        compiler_params=_cparams(("arbitrary", "arbitrary")),
        name="norm_in_proj",
    )(x2d, norm_g.reshape(1, d), mod3, mod3, w_wide, w_gate)


REC_HEADS_PER_STEP = 2


def _mlstm_kernel(gb_ref, q_ref, k_ref, v_ref, o_ref, zg_ref, cwq_ref, cwk_ref, og_ref, y_ref,
                  pad_s, qs_s, ks_s, ic_s, fc_s):
    hp = pl.program_id(1)
    t = q_ref.shape[1]
    cl = MLSTM_CHUNK
    nc = t // cl
    hps = REC_HEADS_PER_STEP

    def conv_silu(src_ref, cw_ref, cols, dst_s, scale):
        pad_s[8:8 + t, :] = src_ref[0, :, cols]
        off = 8 - (MLSTM_CONV - 1)
        tile = 128
        for r0 in range(0, t, tile):
            acc = cw_ref[0:1, cols] * pad_s[r0 + off:r0 + off + tile, :]
            for j in range(1, MLSTM_CONV):
                acc = acc + cw_ref[j:j + 1, cols] * pad_s[r0 + off + j:r0 + off + j + tile, :]
            dst_s[r0:r0 + tile, :] = acc * _sigmoid(acc) * scale

    pad_s[0:8, :] = jnp.zeros((8, HEAD_DIM), F32)
    zg = zg_ref[0]
    for hh in range(hps):
        h = hp * hps + hh
        conv_silu(q_ref, cwq_ref, _head_cols(hh), qs_s.at[hh], HEAD_DIM ** -0.5)
        conv_silu(k_ref, cwk_ref, _head_cols(hh), ks_s.at[hh], 1.0)
        ic_s[hh] = _lane_col(zg, GL_MI + h) + gb_ref[0, h]
        f_pre = _lane_col(zg, GL_MF + h) + gb_ref[1, h]
        fc_s[hh] = jnp.minimum(f_pre, 0.0) - jnp.log1p(jnp.exp(-jnp.abs(f_pre)))

    rr = lax.broadcasted_iota(jnp.int32, (cl, cl), 0)
    cc = lax.broadcasted_iota(jnp.int32, (cl, cl), 1)
    eye = rr == cc
    causal = cc <= rr
    og = og_ref[...]

    def head_step(hh, c, carry):
        c_st, n_st, m_st = carry
        sl = pl.ds(pl.multiple_of(c * cl, cl), cl)
        cols = _head_cols(hh)
        qc = qs_s[hh, sl, :]
        kc = ks_s[hh, sl, :]
        vc = v_ref[0, sl, cols]
        i_col = ic_s[hh, sl, :]
        f_col = fc_s[hh, sl, :]
        f_row = jnp.sum(jnp.where(eye, f_col, 0.0), axis=0, keepdims=True)
        i_row = jnp.sum(jnp.where(eye, i_col, 0.0), axis=0, keepdims=True)
        a_col = jnp.sum(jnp.where(causal, f_row, 0.0), axis=1, keepdims=True)
        a_row = jnp.sum(jnp.where(rr <= cc, f_col, 0.0), axis=0, keepdims=True)
        log_d = jnp.where(causal, a_col - a_row + i_row, NEG)
        m_inter = a_col + m_st
        m_row = jnp.maximum(m_inter, jnp.max(log_d, axis=1, keepdims=True))
        s = _dot_nt(qc, kc) * jnp.exp(log_d - m_row)
        w_inter = jnp.exp(m_inter - m_row)
        num = _dot(s, vc) + w_inter * _dot(qc, c_st)
        den = jnp.sum(s, axis=1, keepdims=True) + w_inter * jnp.sum(qc * n_st, axis=1, keepdims=True)
        h_out = num / jnp.maximum(jnp.abs(den), jnp.exp(-m_row))
        a_last = jnp.sum(f_col, axis=0, keepdims=True)
        w_log = a_last - a_col + i_col
        m_new = jnp.maximum(a_last + m_st, jnp.max(w_log, axis=0, keepdims=True))
        w = jnp.exp(w_log - m_new)
        decay = jnp.exp(a_last + m_st - m_new)
        kw = kc * w
        c_new = decay * c_st + _dot_tn(kw, vc)
        n_new = decay * n_st + jnp.sum(kw, axis=0, keepdims=True)
        y = _rms(h_out, og) * _sigmoid(o_ref[0, sl, cols])
        y_ref[0, sl, cols] = y.astype(y_ref.dtype)
        return c_new, n_new, m_new

    def body(c, carry):
        return tuple(head_step(hh, c, carry[hh]) for hh in range(hps))

    carry0 = (jnp.zeros((HEAD_DIM, HEAD_DIM), F32), jnp.zeros((1, HEAD_DIM), F32), jnp.zeros((1, 1), F32))
    lax.fori_loop(0, nc, body, (carry0,) * hps, unroll=4)


def _col_spec(seq, group, hps):
    return pl.BlockSpec((1, seq, hps * HEAD_DIM), lambda b, h: (b, 0, group * (HEADS // hps) + h))


def _mlstm(z3, zg3, gate_b, conv_w, out_g):
    bsz, seq, _ = z3.shape
    hps = REC_HEADS_PER_STEP
    wide = hps * HEAD_DIM
    per_group = HEADS // hps
    smem = pl.BlockSpec(memory_space=pltpu.SMEM)
    return pl.pallas_call(
        _mlstm_kernel,
        grid=(bsz, per_group),
        in_specs=[smem,
                  _col_spec(seq, G_MQ, hps), _col_spec(seq, G_MK, hps), _col_spec(seq, G_MV, hps),
                  _col_spec(seq, G_MO, hps),
                  pl.BlockSpec((1, seq, GATE_W), lambda b, h: (b, 0, 0)),
                  pl.BlockSpec((MLSTM_CONV, wide), lambda b, h: (0, h)),
                  pl.BlockSpec((MLSTM_CONV, wide), lambda b, h: (0, per_group + h)),
                  pl.BlockSpec((1, HEAD_DIM), lambda b, h: (0, 0))],
        out_specs=pl.BlockSpec((1, seq, wide), lambda b, h: (b, 0, h)),
        out_shape=jax.ShapeDtypeStruct((bsz, seq, GROUP_W), BF16),
        scratch_shapes=[pltpu.VMEM((seq + 8, HEAD_DIM), F32), pltpu.VMEM((hps, seq, HEAD_DIM), F32),
                        pltpu.VMEM((hps, seq, HEAD_DIM), F32), pltpu.VMEM((hps, seq, 1), F32),
                        pltpu.VMEM((hps, seq, 1), F32)],
        compiler_params=_cparams(("arbitrary", "arbitrary")),
        name="mlstm",
    )(gate_b, z3, z3, z3, z3, zg3, conv_w, conv_w, out_g.reshape(1, HEAD_DIM))


def _ret_kernel(lg_ref, q_ref, k_ref, v_ref, g_ref, og_ref, y_ref):
    t = q_ref.shape[1]
    cl = RET_CHUNK
    rr = lax.broadcasted_iota(jnp.int32, (cl, cl), 0)
    cc = lax.broadcasted_iota(jnp.int32, (cl, cl), 1)
    diff = (rr - cc).astype(F32)
    jcol = lax.broadcasted_iota(jnp.int32, (cl, 1), 0).astype(F32)
    og = og_ref[...]
    consts = []
    for h in range(HEADS):
        lg = lg_ref[h]
        consts.append(dict(decay_in=jnp.where(diff >= 0, jnp.exp(lg * jnp.maximum(diff, 0.0)), 0.0),
                           zeta=jnp.exp(lg * (cl - 1.0 - jcol)), xi=jnp.exp(lg * (jcol + 1.0)),
                           g_chunk=jnp.exp(jnp.full((1, 1), lg * cl, F32))))

    def head_step(h, c, r_st):
        sl = pl.ds(pl.multiple_of(c * cl, cl), cl)
        cols = _head_cols(h)
        qc = q_ref[0, sl, cols]
        kc = k_ref[0, sl, cols] * HEAD_DIM ** -0.5
        vc = v_ref[0, sl, cols]
        scores = _dot_nt(qc, kc) * consts[h]["decay_in"]
        o = _dot(scores, vc) + _dot(qc, r_st) * consts[h]["xi"]
        r_new = consts[h]["g_chunk"] * r_st + _dot_tn(kc * consts[h]["zeta"], vc)
        gg = g_ref[0, sl, cols]
        y = _rms(o, og) * (gg * _sigmoid(gg))
        y_ref[0, sl, cols] = y.astype(y_ref.dtype)
        return r_new

    def body(c, carry):
        return tuple(head_step(h, c, carry[h]) for h in range(HEADS))

    lax.fori_loop(0, t // cl, body, (jnp.zeros((HEAD_DIM, HEAD_DIM), F32),) * HEADS, unroll=4)


def _retention(z3, log_gamma, out_g):
    bsz, seq, _ = z3.shape
    grp = lambda group: pl.BlockSpec((1, seq, GROUP_W), lambda b: (b, 0, group))
    return pl.pallas_call(
        _ret_kernel,
        grid=(bsz,),
        in_specs=[pl.BlockSpec(memory_space=pltpu.SMEM), grp(G_RQ), grp(G_RK), grp(G_RV), grp(G_RG),
                  pl.BlockSpec((1, HEAD_DIM), lambda b: (0, 0))],
        out_specs=pl.BlockSpec((1, seq, GROUP_W), lambda b: (b, 0, 0)),
        out_shape=jax.ShapeDtypeStruct((bsz, seq, GROUP_W), BF16),
        compiler_params=_cparams(("arbitrary",)),
        name="retention",
    )(log_gamma, z3, z3, z3, z3, out_g.reshape(1, HEAD_DIM))


BF16_SUBLANES = 16
ATT_TQ = 256
STRIP_W = 512
STRIP_SHIFT = 9
HEADS_PER_STEP = 2
MOBA_HEADS_PER_STEP = 4


def _rank_rows(vals):
    n = vals.shape[0]
    rowb = lax.broadcasted_iota(jnp.int32, vals.shape, 0)
    rank = jnp.zeros(vals.shape, jnp.int32)
    for jp in range(n):
        rv = vals[jp:jp + 1, :]
        beats = (rv > vals) | ((rv == vals) & (rowb > jp))
        rank = rank + beats.astype(jnp.int32)
    return rank


def _top_rows(vals, k):
    rowb = lax.broadcasted_iota(jnp.int32, vals.shape, 0).astype(F32)
    picked = jnp.zeros(vals.shape, jnp.bool_)
    for _ in range(k):
        best = jnp.max(vals, axis=0, keepdims=True)
        first = jnp.min(jnp.where(vals == best, rowb, float(vals.shape[0])), axis=0, keepdims=True)
        hit = (rowb == first) & (best > NEG)
        picked = picked | hit
        vals = jnp.where(hit, NEG, vals)
    return picked


def _pad_rows(x, rows):
    return jnp.concatenate([x, jnp.zeros((rows - x.shape[0], x.shape[1]), x.dtype)], axis=0)


def _fold_lanes(op, acc, x):
    for b in range(x.shape[1] // HEAD_DIM):
        acc = op(acc, x[:, b * HEAD_DIM:(b + 1) * HEAD_DIM])
    return acc


def _block_masked_attention(heads, qi, strip_s):
    tq, w = ATT_TQ, STRIP_W
    scale = HEAD_DIM ** -0.5
    t0 = qi * tq
    last = jnp.right_shift(t0, STRIP_SHIFT)
    rr = lax.broadcasted_iota(jnp.int32, (tq, w), 0)
    cc = lax.broadcasted_iota(jnp.int32, (tq, w), 1)
    rel = rr - cc
    rel_f = rel.astype(F32)
    alibi = [(-hd["slope"]) * rel_f for hd in heads]
    n_blk = heads[0]["unsel"].shape[0]
    erow = lax.broadcasted_iota(jnp.int32, (n_blk, w), 0)
    ecol = lax.broadcasted_iota(jnp.int32, (n_blk, w), 1)

    def scores(hd, c, bias):
        expand = jnp.where(erow == jnp.right_shift(c * w + ecol, hd["blk_shift"]), NEG, 0.0).astype(BF16)
        mask_bias = lax.dot_general(hd["unsel"], expand, (((0,), (0,)), ((), ())), preferred_element_type=F32)
        kc = hd["k_s"][pl.ds(pl.multiple_of(c * w, w), w), :]
        return _dot_nt(hd["qb"], kc) * scale + bias + mask_bias

    def first_pass(c, ms):
        out = []
        for hi, hd in enumerate(heads):
            s = scores(hd, c, alibi[hi] + (-hd["slope"]) * (t0 - c * w).astype(F32))
            strip_s[hi, c] = s
            out.append(_fold_lanes(jnp.maximum, ms[hi], s))
        return tuple(out)

    ms = lax.fori_loop(0, last, first_pass, tuple(jnp.full((tq, HEAD_DIM), NEG, F32) for _ in heads))
    dist = rel + (t0 - last * w)
    row_max = []
    for hi, hd in enumerate(heads):
        bias = jnp.where(dist >= 0, (-hd["slope"]) * dist.astype(F32), NEG)
        s = scores(hd, last, bias)
        strip_s[hi, last] = s
        row_max.append(jnp.max(_fold_lanes(jnp.maximum, ms[hi], s), axis=1, keepdims=True))

    def second_pass(c, carry):
        out = []
        for hi, hd in enumerate(heads):
            l_run, acc = carry[hi]
            p = jnp.exp(strip_s[hi, c] - row_max[hi])
            vc = hd["v_s"][pl.ds(pl.multiple_of(c * w, w), w), :]
            out.append((_fold_lanes(jnp.add, l_run, p), acc + _dot(p, vc)))
        return tuple(out)

    zero = jnp.zeros((tq, HEAD_DIM), F32)
    res = lax.fori_loop(0, last + 1, second_pass, tuple((zero, zero) for _ in heads))
    return [acc / jnp.sum(l_run, axis=1, keepdims=True) for l_run, acc in res]


def _head_cols(hh):
    return slice(hh * HEAD_DIM, (hh + 1) * HEAD_DIM)


def _moba_kernel(slope_ref, q_ref, k_ref, v_ref, g_ref, y_ref, kn_s, vb_s, kmean_s, strip_s):
    hp = pl.program_id(1)
    qi = pl.program_id(2)
    t = k_ref.shape[1]
    blk = MOBA_BLOCK
    nb = t // blk

    @pl.when(qi == 0)
    def _():
        for hh in range(MOBA_HEADS_PER_STEP):
            kn = _rms(k_ref[0, :, _head_cols(hh)], g_ref[1:2, :])
            kn_s[hh] = kn.astype(kn_s.dtype)
            vb_s[hh] = v_ref[0, :, _head_cols(hh)].astype(vb_s.dtype)
            kmean_s[hh] = jnp.zeros(kmean_s.shape[1:], F32)
            for j in range(nb):
                kmean_s[hh, j:j + 1, :] = jnp.mean(kn[j * blk:(j + 1) * blk, :], axis=0, keepdims=True)

    rowb = lax.broadcasted_iota(jnp.int32, (nb, ATT_TQ), 0)
    heads = []
    for hh in range(MOBA_HEADS_PER_STEP):
        qn = _rms(q_ref[0, :, _head_cols(hh)], g_ref[0:1, :])
        gate = jnp.where(rowb < qi, _dot_nt(kmean_s[hh], qn)[0:nb, :], NEG)
        sel = (_rank_rows(gate) < MOBA_TOPK) & (rowb < qi)
        unsel = jnp.where(sel | (rowb == qi), 0.0, 1.0)
        heads.append(dict(qb=qn.astype(MXU_DTYPE), k_s=kn_s.at[hh], v_s=vb_s.at[hh],
                          slope=slope_ref[hp * MOBA_HEADS_PER_STEP + hh], blk_shift=MOBA_SHIFT,
                          unsel=_pad_rows(unsel, BF16_SUBLANES).astype(BF16)))
    outs = _block_masked_attention(heads, qi, strip_s)
    for hh in range(MOBA_HEADS_PER_STEP):
        y_ref[0, :, _head_cols(hh)] = outs[hh].astype(y_ref.dtype)


def _moba(z3, qk_g, slopes):
    bsz, seq, _ = z3.shape
    assert MOBA_BLOCK == ATT_TQ and seq % STRIP_W == 0
    hps = MOBA_HEADS_PER_STEP
    wide = hps * HEAD_DIM
    per_group = HEADS // hps
    kv = lambda group: pl.BlockSpec((1, seq, wide), lambda b, h, i: (b, 0, group * per_group + h))
    return pl.pallas_call(
        _moba_kernel,
        grid=(bsz, per_group, seq // ATT_TQ),
        in_specs=[pl.BlockSpec(memory_space=pltpu.SMEM),
                  pl.BlockSpec((1, ATT_TQ, wide), lambda b, h, i: (b, i, G_BQ * per_group + h)),
                  kv(G_BK), kv(G_BV),
                  pl.BlockSpec((2, HEAD_DIM), lambda b, h, i: (0, 0))],
        out_specs=pl.BlockSpec((1, ATT_TQ, wide), lambda b, h, i: (b, i, h)),
        out_shape=jax.ShapeDtypeStruct((bsz, seq, GROUP_W), BF16),
        scratch_shapes=[pltpu.VMEM((hps, seq, HEAD_DIM), MXU_DTYPE), pltpu.VMEM((hps, seq, HEAD_DIM), MXU_DTYPE),
                        pltpu.VMEM((hps, HEAD_DIM, HEAD_DIM), F32),
                        pltpu.VMEM((hps, seq // STRIP_W, ATT_TQ, STRIP_W), F32)],
        compiler_params=_cparams(("arbitrary", "arbitrary", "arbitrary")),
        name="moba",
    )(slopes, z3, z3, z3, qk_g)


def _gelu_tanh(x):
    return 0.5 * x * (1.0 + jnp.tanh(0.7978845608028654 * (x + 0.044715 * (x * x * x))))


def _nsa_kernel(slope_ref, q_ref, kc0_ref, kc1_ref, vc0_ref, vc1_ref, ks_ref, vs_ref, kw_ref, vw_ref, zg_ref,
                qg_ref, kg_ref, pe_ref, w1_ref, w2_ref, y_ref,
                kcmp_s, vcmp_s, ksn_s, vsb_s, kwn_s, vwb_s, strip_s):
    kc_refs, vc_refs = (kc0_ref, kc1_ref), (vc0_ref, vc1_ref)
    hp = pl.program_id(1)
    qi = pl.program_id(2)
    t = ks_ref.shape[1]
    tq = ATT_TQ
    nsub = t // NSA_CMP_STRIDE
    n_cmp = nsub - 1
    n_sel = t // NSA_SEL_BLOCK
    scale = HEAD_DIM ** -0.5

    @pl.when(qi == 0)
    def _():
        for hh in range(HEADS_PER_STEP):
            cols = _head_cols(hh)
            for cv, (src, dst) in enumerate(((kc_refs[hh], kcmp_s), (vc_refs[hh], vcmp_s))):
                acc_a = jnp.zeros((nsub, HEAD_DIM), F32)
                acc_b = jnp.zeros((nsub, HEAD_DIM), F32)
                for r in range(NSA_CMP_STRIDE):
                    zr = src[0, pl.ds(r, nsub, stride=NSA_CMP_STRIDE), :]
                    acc_a = acc_a + _dot(zr + pe_ref[cv, r:r + 1, :], w1_ref[cv, r])
                    rb = NSA_CMP_STRIDE + r
                    acc_b = acc_b + _dot(zr + pe_ref[cv, rb:rb + 1, :], w1_ref[cv, rb])
                hid = _gelu_tanh(acc_a + pltpu.roll(acc_b, nsub - 1, axis=0))
                cmp = _dot(hid, w2_ref[cv])
                if cv == 0:
                    cmp = _rms(cmp, kg_ref[0:1, :])
                dst[hh] = cmp.astype(dst.dtype)
            ksn_s[hh] = _rms(ks_ref[0, :, cols], kg_ref[1:2, :]).astype(ksn_s.dtype)
            vsb_s[hh] = vs_ref[0, :, cols].astype(vsb_s.dtype)
            kwn_s[hh] = _rms(kw_ref[0, :, cols], kg_ref[2:3, :]).astype(kwn_s.dtype)
            vwb_s[hh] = vw_ref[0, :, cols].astype(vwb_s.dtype)

    t0 = qi * tq
    rowi = lax.broadcasted_iota(jnp.int32, (tq, HEAD_DIM), 0)
    lane = lax.broadcasted_iota(jnp.int32, (tq, HEAD_DIM), 1)
    dist_c = (t0 + rowi) - (lane * NSA_CMP_STRIDE + (NSA_CMP_LEN - 1))
    ok_c = (dist_c >= 0) & (lane < n_cmp)
    dist_cf = dist_c.astype(F32)

    ob = lax.broadcasted_iota(jnp.int32, (HEAD_DIM, nsub), 0)
    oc = lax.broadcasted_iota(jnp.int32, (HEAD_DIM, nsub), 1)
    overlap_t = ((oc * NSA_CMP_STRIDE <= ob * NSA_SEL_BLOCK + (NSA_SEL_BLOCK - 1))
                 & (oc * NSA_CMP_STRIDE + (NSA_CMP_LEN - 1) >= ob * NSA_SEL_BLOCK)
                 & (oc < n_cmp) & (ob < n_sel)).astype(F32)
    rowb = lax.broadcasted_iota(jnp.int32, (n_sel, tq), 0)
    cur = jnp.right_shift(t0 + lax.broadcasted_iota(jnp.int32, (n_sel, tq), 1), NSA_SEL_SHIFT)

    win_w = NSA_WINDOW + tq
    k0 = jnp.maximum(t0 - NSA_WINDOW, 0)
    wr = lax.broadcasted_iota(jnp.int32, (tq, win_w), 0)
    wc = lax.broadcasted_iota(jnp.int32, (tq, win_w), 1)
    dist_w = (t0 - k0) + wr - wc
    ok_w = (dist_w >= 0) & (dist_w < NSA_WINDOW)
    dist_wf = dist_w.astype(F32)
    win_rows = pl.ds(pl.multiple_of(k0, tq), win_w)

    heads, o_cmp, o_win = [], [], []
    for hh in range(HEADS_PER_STEP):
        slope = slope_ref[hp * HEADS_PER_STEP + hh]
        qn = _rms(q_ref[0, :, _head_cols(hh)], qg_ref[...])
        qb = qn.astype(MXU_DTYPE)

        s_c = jnp.where(ok_c, _dot_nt(qb, kcmp_s[hh]) * scale - slope * dist_cf, NEG)
        m_c = jnp.max(s_c, axis=1, keepdims=True)
        e_c = jnp.where(ok_c, jnp.exp(s_c - m_c), 0.0)
        p_c = e_c / jnp.maximum(jnp.sum(e_c, axis=1, keepdims=True), 1e-30)
        o_cmp.append(_dot(p_c, vcmp_s[hh]))

        imp = _dot_split_nt(overlap_t, p_c)[0:n_sel, :]
        sel = (rowb == cur) | _top_rows(jnp.where(rowb < cur, imp, NEG), NSA_SEL_TOPN - 1)
        unsel = jnp.where(sel, 0.0, 1.0)
        heads.append(dict(qb=qb, k_s=ksn_s.at[hh], v_s=vsb_s.at[hh], slope=slope, blk_shift=NSA_SEL_SHIFT,
                          unsel=unsel.astype(BF16)))

        s_w = _dot_nt(qb, kwn_s[hh, win_rows, :]) * scale + jnp.where(ok_w, (-slope) * dist_wf, NEG)
        p_w = jnp.exp(s_w - jnp.max(s_w, axis=1, keepdims=True))
        o_win.append(_dot(p_w, vwb_s[hh, win_rows, :]) / jnp.sum(p_w, axis=1, keepdims=True))

    o_slc = _block_masked_attention(heads, qi, strip_s)

    zg = zg_ref[0]
    for hh in range(HEADS_PER_STEP):
        h = hp * HEADS_PER_STEP + hh
        g_cmp = _sigmoid(_lane_col(zg, GL_NG + h))
        g_slc = _sigmoid(_lane_col(zg, GL_NG + HEADS + h))
        g_win = _sigmoid(_lane_col(zg, GL_NG + 2 * HEADS + h))
        y_ref[0, :, _head_cols(hh)] = (g_cmp * o_cmp[hh] + g_slc * o_slc[hh] + g_win * o_win[hh]).astype(y_ref.dtype)


def _nsa(z3, zg3, q_g, k_g, cmp_pos, cmp_w1, cmp_w2, slopes):
    bsz, seq, _ = z3.shape
    assert seq // NSA_CMP_STRIDE == HEAD_DIM, "compressed blocks are laid out on the 128 lanes"
    assert seq % STRIP_W == 0 and seq >= NSA_WINDOW + ATT_TQ
    tq = ATT_TQ
    hps = HEADS_PER_STEP
    wide = hps * HEAD_DIM
    per_group = HEADS // hps
    kv = lambda group: pl.BlockSpec((1, seq, wide), lambda b, h, i: (b, 0, group * per_group + h))
    one = lambda group, hh: pl.BlockSpec((1, seq, HEAD_DIM), lambda b, h, i: (b, 0, group * HEADS + h * hps + hh))
    full = lambda shape: pl.BlockSpec(shape, lambda b, h, i: (0,) * len(shape))
    w1 = cmp_w1.reshape(2, NSA_CMP_LEN, HEAD_DIM, HEAD_DIM).astype(MXU_DTYPE)
    w2 = cmp_w2.astype(MXU_DTYPE)
    seq_buf = lambda: pltpu.VMEM((hps, seq, HEAD_DIM), MXU_DTYPE)
    cmp_buf = lambda: pltpu.VMEM((hps, HEAD_DIM, HEAD_DIM), MXU_DTYPE)
    return pl.pallas_call(
        _nsa_kernel,
        grid=(bsz, per_group, seq // tq),
        in_specs=[pl.BlockSpec(memory_space=pltpu.SMEM),
                  pl.BlockSpec((1, tq, wide), lambda b, h, i: (b, i, G_NQ * per_group + h)),
                  one(G_NKC, 0), one(G_NKC, 1), one(G_NVC, 0), one(G_NVC, 1),
                  kv(G_NKS), kv(G_NVS), kv(G_NKW), kv(G_NVW),
                  pl.BlockSpec((1, tq, GATE_W), lambda b, h, i: (b, i, 0)),
                  full((1, HEAD_DIM)), full((3, HEAD_DIM)), full((2, NSA_CMP_LEN, HEAD_DIM)),
                  full((2, NSA_CMP_LEN, HEAD_DIM, HEAD_DIM)), full((2, HEAD_DIM, HEAD_DIM))],
        out_specs=pl.BlockSpec((1, tq, wide), lambda b, h, i: (b, i, h)),
        out_shape=jax.ShapeDtypeStruct((bsz, seq, GROUP_W), BF16),
        scratch_shapes=[cmp_buf(), cmp_buf(), seq_buf(), seq_buf(), seq_buf(), seq_buf(),
                        pltpu.VMEM((hps, seq // STRIP_W, tq, STRIP_W), F32)],
        compiler_params=_cparams(("arbitrary", "arbitrary", "arbitrary")),
        name="nsa",
    )(slopes, z3, z3, z3, z3, z3, z3, z3, z3, z3, zg3, q_g.reshape(1, HEAD_DIM), k_g, cmp_pos, w1, w2)


def _out_proj_kernel(ym_ref, yb_ref, yr_ref, yn_ref, w_ref, x_ref, g_ref, o_ref):
    acc = jnp.dot(ym_ref[...], w_ref[0, 0:GROUP_W, :], preferred_element_type=F32)
    acc = acc + jnp.dot(yb_ref[...], w_ref[0, GROUP_W:2 * GROUP_W, :], preferred_element_type=F32)
    acc = acc + jnp.dot(yr_ref[...], w_ref[0, 2 * GROUP_W:3 * GROUP_W, :], preferred_element_type=F32)
    acc = acc + jnp.dot(yn_ref[...], w_ref[0, 3 * GROUP_W:4 * GROUP_W, :], preferred_element_type=F32)
    o_ref[...] = x_ref[...] + g_ref[0] * acc


def _out_proj(ys, w_out, layer, x2d, mod3, seq):
    n, d = x2d.shape
    tm, tn = 1024, 1024
    per_b = seq // tm
    y_spec = pl.BlockSpec((tm, GROUP_W), lambda i, j: (i, 0))
    return pl.pallas_call(
        _out_proj_kernel,
        grid=(n // tm, d // tn),
        in_specs=[y_spec, y_spec, y_spec, y_spec,
                  pl.BlockSpec((1, 4 * GROUP_W, tn), lambda i, j: (layer, 0, j)),
                  pl.BlockSpec((tm, tn), lambda i, j: (i, j)),
                  pl.BlockSpec((1, 1, tn), lambda i, j: (i // per_b, 0, 2 * (d // tn) + j))],
        out_specs=pl.BlockSpec((tm, tn), lambda i, j: (i, j)),
        out_shape=jax.ShapeDtypeStruct((n, d), F32),
        compiler_params=_cparams(("arbitrary", "arbitrary")),
        name="out_proj",
    )(*[y.reshape(n, GROUP_W) for y in ys], w_out, x2d, mod3)


SLAB_E0, SLAB_E1, SLAB_R0, SLAB_R1, SLAB_G0, SLAB_G1 = 0, 1, 2, 3, 4, 5


def _route_kernel(x_ref, g_ref, sc_ref, sh_ref, wr_ref, br_ref, h_ref, slab_ref, cnt_ref, carry_s):
    tm = x_ref.shape[0]

    @pl.when(pl.program_id(0) == 0)
    def _():
        carry_s[...] = jnp.zeros(carry_s.shape, F32)

    hmod = _rms(x_ref[...], g_ref[...]) * (1.0 + sc_ref[0]) + sh_ref[0]
    _store_slabs(h_ref, hmod)
    logits = _dot(hmod, wr_ref[...]) + br_ref[...]
    lane = lax.broadcasted_iota(jnp.int32, logits.shape, 1).astype(F32)
    far = 4.0 * GATE_W

    in_g = lane < N_GROUPS
    lg = jnp.where(in_g, logits, NEG)
    g_max = jnp.max(lg, axis=1, keepdims=True)
    grp = jnp.min(jnp.where(in_g & (lg == g_max), lane, far), axis=1, keepdims=True)
    p_grp = 1.0 / jnp.sum(jnp.where(in_g, jnp.exp(lg - g_max), 0.0), axis=1, keepdims=True)

    lo = N_GROUPS + grp * EXPERTS_PER_GROUP
    in_e = (lane >= lo) & (lane < lo + EXPERTS_PER_GROUP)
    le = jnp.where(in_e, logits, NEG)
    e_max = jnp.max(le, axis=1, keepdims=True)
    ee = jnp.where(in_e, jnp.exp(le - e_max), 0.0)
    pe = jnp.where(in_e, ee / jnp.sum(ee, axis=1, keepdims=True), -1.0)
    p1 = jnp.max(pe, axis=1, keepdims=True)
    i1 = jnp.min(jnp.where(pe == p1, lane, far), axis=1, keepdims=True)
    pe2 = jnp.where(lane == i1, -1.0, pe)
    p2 = jnp.max(pe2, axis=1, keepdims=True)
    i2 = jnp.min(jnp.where(pe2 == p2, lane, far), axis=1, keepdims=True)
    e0 = i1 - N_GROUPS
    e1 = i2 - N_GROUPS
    g0 = p_grp * (p1 / (p1 + p2))
    g1 = p_grp * (p2 / (p1 + p2))

    onehot = ((lane == e0) | (lane == e1)).astype(BF16)
    rr = lax.broadcasted_iota(jnp.int32, (tm, tm), 0)
    cc = lax.broadcasted_iota(jnp.int32, (tm, tm), 1)
    before = jnp.dot((cc < rr).astype(BF16), onehot, preferred_element_type=F32) + carry_s[...]
    r0 = jnp.sum(jnp.where(lane == e0, before, 0.0), axis=1, keepdims=True)
    r1 = jnp.sum(jnp.where(lane == e1, before, 0.0), axis=1, keepdims=True)
    carry_s[...] = carry_s[...] + jnp.sum(onehot.astype(F32), axis=0, keepdims=True)
    cnt_ref[...] = carry_s[...]

    slab = jnp.where(lane == SLAB_E0, e0.astype(F32), 0.0)
    slab = jnp.where(lane == SLAB_E1, e1.astype(F32), slab)
    slab = jnp.where(lane == SLAB_R0, r0, slab)
    slab = jnp.where(lane == SLAB_R1, r1, slab)
    slab = jnp.where(lane == SLAB_G0, g0, slab)
    slab = jnp.where(lane == SLAB_G1, g1, slab)
    slab_ref[...] = slab


def _route(x2d, norm_g, mod3, w_router, b_router, seq):
    n, d = x2d.shape
    tm = 512
    per_b = seq // tm
    return pl.pallas_call(
        _route_kernel,
        grid=(n // tm,),
        in_specs=[pl.BlockSpec((tm, d), lambda i: (i, 0)),
                  pl.BlockSpec((1, d), lambda i: (0, 0)),
                  pl.BlockSpec((1, 1, d), lambda i: (i // per_b, 0, 4)),
                  pl.BlockSpec((1, 1, d), lambda i: (i // per_b, 0, 3)),
                  pl.BlockSpec((d, GATE_W), lambda i: (0, 0)),
                  pl.BlockSpec((1, GATE_W), lambda i: (0, 0))],
        out_specs=[pl.BlockSpec((tm * SLAB, HEAD_DIM), lambda i: (i, 0)),
                   pl.BlockSpec((tm, GATE_W), lambda i: (i, 0)),
                   pl.BlockSpec((1, GATE_W), lambda i: (0, 0))],
        out_shape=[jax.ShapeDtypeStruct((n * SLAB, HEAD_DIM), F32),
                   jax.ShapeDtypeStruct((n, GATE_W), F32),
                   jax.ShapeDtypeStruct((1, GATE_W), F32)],
        scratch_shapes=[pltpu.VMEM((1, GATE_W), F32)],
        compiler_params=_cparams(("arbitrary",)),
        name="route",
    )(x2d, norm_g.reshape(1, d), mod3, mod3, w_router, b_router)


def _invert_kernel(dest_ref, pad_lo_ref, pad_hi_ref, inv_ref):
    def clear(i, _):
        inv_ref[i] = 0
        return 0

    def clear_segment(g, _):
        lax.fori_loop(pad_lo_ref[g], pad_hi_ref[g], clear, 0)
        return 0

    def put(a, _):
        inv_ref[dest_ref[a]] = jnp.right_shift(a, 1)
        return 0

    lax.fori_loop(0, pad_lo_ref.shape[0], clear_segment, 0)
    lax.fori_loop(0, dest_ref.shape[0], put, 0, unroll=32)


def _invert(dest_flat, pad_lo, pad_hi, cap):
    smem = pl.BlockSpec(memory_space=pltpu.SMEM)
    return pl.pallas_call(
        _invert_kernel,
        in_specs=[smem, smem, smem],
        out_specs=smem,
        out_shape=jax.ShapeDtypeStruct((cap,), jnp.int32),
        name="invert",
    )(dest_flat, pad_lo, pad_hi)


def _start_row_gather(idx_ref, src_hbm, dst, sem, n_rows, first=0, span=1):
    for r in range(first, n_rows):
        start = idx_ref[r] if span == 1 else pl.multiple_of(idx_ref[r] * span, span)
        pltpu.make_async_copy(src_hbm.at[pl.ds(start, span)], dst.at[pl.ds(r * span, span)], sem).start(
            priority=r % 2)


def _wait_row_gather(src_hbm, dst, sem, n_rows, span=1):
    pltpu.make_async_copy(src_hbm.at[pl.ds(0, n_rows * span)], dst, sem).wait()


SLAB = D_MODEL // HEAD_DIM


def _store_slabs(ref, x):
    for c in range(SLAB):
        ref[pl.ds(c, x.shape[0], stride=SLAB), :] = x[:, c * HEAD_DIM:(c + 1) * HEAD_DIM]


def _load_slabs(ref, n_rows, lead=()):
    return jnp.concatenate([ref[lead + (pl.ds(c, n_rows, stride=SLAB), slice(None))] for c in range(SLAB)], axis=1)


def _expert_kernel(blk_e_ref, n_used_ref, run_slot_ref, next_e_ref, next_ok_ref, inv_ref, h_hbm,
                   w1_hbm, w3_hbm, w2_hbm, y_ref,
                   x_s, w1_f, w3_f, w2_f, w1_s, w3_s, w2_s, sem, wsem, *, layer):
    s = pl.program_id(0)
    rows = EXPERT_ROWS
    n_used = n_used_ref[0]
    ahead = EXPERT_GATHER_AHEAD
    slot = lax.rem(s, ahead + 1)
    cslot = lax.rem(s + 1, ahead + 1)

    blk = s - ahead
    prev = jnp.maximum(blk - 1, 0)
    gather = s < n_used
    compute = (s >= ahead) & (blk < n_used)

    def start_gather(first, stop):
        _start_row_gather(inv_ref, h_hbm, x_s.at[slot], sem.at[slot], stop, first, span=SLAB)

    def weight_copies(e, wslot):
        return (pltpu.make_async_copy(w1_hbm.at[layer, e], w1_f.at[wslot], wsem.at[wslot, 0]),
                pltpu.make_async_copy(w3_hbm.at[layer, e], w3_f.at[wslot], wsem.at[wslot, 1]),
                pltpu.make_async_copy(w2_hbm.at[layer, e], w2_f.at[wslot], wsem.at[wslot, 2]))

    def expert_block(gather_too):
        @pl.when((blk == 0) | (blk_e_ref[blk] != blk_e_ref[prev]))
        def _():
            wslot = run_slot_ref[blk]
            for cp in weight_copies(blk_e_ref[blk], wslot):
                cp.wait()

            @pl.when(next_ok_ref[blk] == 1)
            def _():
                for cp in weight_copies(next_e_ref[blk], 1 - wslot):
                    cp.start()

            w1_s[...] = w1_f[wslot].astype(w1_s.dtype)
            w3_s[...] = w3_f[wslot].astype(w3_s.dtype)
            w2_s[...] = w2_f[wslot].astype(w2_s.dtype)

        cuts = (0, rows // 4, rows // 2, rows) if gather_too else (0, 0, 0, 0)
        _wait_row_gather(h_hbm, x_s.at[cslot], sem.at[cslot], rows, span=SLAB)
        start_gather(cuts[0], cuts[1])
        x = _load_slabs(x_s, rows, (cslot,)).astype(MXU_DTYPE)
        a = jnp.dot(x, w1_s[...], preferred_element_type=F32)
        start_gather(cuts[1], cuts[2])
        b = jnp.dot(x, w3_s[...], preferred_element_type=F32)
        start_gather(cuts[2], cuts[3])
        y_ref[...] = _dot(a * _sigmoid(a) * b, w2_s[...])

    @pl.when(gather & compute)
    def _():
        expert_block(True)

    @pl.when(gather & jnp.logical_not(compute))
    def _():
        @pl.when(s == 0)
        def _():
            for cp in weight_copies(blk_e_ref[0], run_slot_ref[0]):
                cp.start()

        start_gather(0, rows)

    @pl.when(compute & jnp.logical_not(gather))
    def _():
        expert_block(False)

    @pl.when((s >= ahead) & (blk >= n_used))
    def _():
        y_ref[...] = jnp.zeros(y_ref.shape, y_ref.dtype)


def _experts(h2, inv, blk_e, n_used, w1, w3, w2, layer):
    d = D_MODEL
    rows = EXPERT_ROWS
    n_blocks = inv.shape[0] // rows
    idx = jnp.arange(n_blocks, dtype=jnp.int32)
    run_start = jnp.concatenate([jnp.ones((1,), jnp.int32), (blk_e[1:] != blk_e[:-1]).astype(jnp.int32)])
    run_slot = (jnp.cumsum(run_start) - 1) % 2
    later_run = (idx[None, :] > idx[:, None]) & (blk_e[None, :] != blk_e[:, None])
    next_start = jnp.min(jnp.where(later_run, idx[None, :], n_blocks), axis=1)
    next_e = blk_e[jnp.minimum(next_start, n_blocks - 1)]
    next_ok = (next_start < n_used[0]).astype(jnp.int32)

    ahead = EXPERT_GATHER_AHEAD
    done = lambda s: jnp.maximum(s - ahead, 0)
    hbm = pl.BlockSpec(memory_space=pl.ANY)
    return pl.pallas_call(
        functools.partial(_expert_kernel, layer=layer),
        grid_spec=pltpu.PrefetchScalarGridSpec(
            num_scalar_prefetch=5,
            grid=(n_blocks + ahead,),
            in_specs=[pl.BlockSpec((rows,), lambda s, *_: (jnp.minimum(s, n_blocks - 1),), memory_space=pltpu.SMEM),
                      hbm, hbm, hbm, hbm],
            out_specs=pl.BlockSpec((rows, d), lambda s, *_: (done(s), 0)),
            scratch_shapes=[pltpu.VMEM((ahead + 1, rows * SLAB, HEAD_DIM), F32),
                            pltpu.VMEM((2, d, D_EXPERT), F32), pltpu.VMEM((2, d, D_EXPERT), F32),
                            pltpu.VMEM((2, D_EXPERT, d), F32),
                            pltpu.VMEM((d, D_EXPERT), MXU_DTYPE), pltpu.VMEM((d, D_EXPERT), MXU_DTYPE),
                            pltpu.VMEM((D_EXPERT, d), MXU_DTYPE),
                            pltpu.SemaphoreType.DMA((ahead + 1,)), pltpu.SemaphoreType.DMA((2, 3))]),
        out_shape=jax.ShapeDtypeStruct((n_blocks * rows, d), F32),
        compiler_params=_cparams(("arbitrary",)),
        name="experts",
    )(blk_e, n_used, run_slot.astype(jnp.int32), next_e.astype(jnp.int32), next_ok, inv, h2, w1, w3, w2)


def _combine_kernel(d0_ref, d1_ref, x_ref, g_ref, slab_ref, yb_hbm, o_ref, rows_s, sem):
    s = pl.program_id(0)
    n_tiles = pl.num_programs(0) - 1
    tm = x_ref.shape[0]
    slot = lax.rem(s, 2)

    @pl.when(s < n_tiles)
    def _():
        _start_row_gather(d0_ref, yb_hbm, rows_s.at[slot, 0], sem.at[slot], tm)
        _start_row_gather(d1_ref, yb_hbm, rows_s.at[slot, 1], sem.at[slot], tm)

    @pl.when(s >= 1)
    def _():
        _wait_row_gather(yb_hbm, rows_s.at[1 - slot, 0], sem.at[1 - slot], tm)
        _wait_row_gather(yb_hbm, rows_s.at[1 - slot, 1], sem.at[1 - slot], tm)
        route = slab_ref[...]
        g0 = route[:, SLAB_G0:SLAB_G0 + 1]
        g1 = route[:, SLAB_G1:SLAB_G1 + 1]
        o_ref[...] = x_ref[...] + g_ref[0] * (g0 * rows_s[1 - slot, 0] + g1 * rows_s[1 - slot, 1])


def _combine(x2d, mod3, slab, dest0, dest1, yb, seq):
    n, d = x2d.shape
    tm = 256
    per_b = seq // tm
    n_tiles = n // tm
    done = lambda s: jnp.maximum(s - 1, 0)
    idx_spec = pl.BlockSpec((tm,), lambda s: (jnp.minimum(s, n_tiles - 1),), memory_space=pltpu.SMEM)
    return pl.pallas_call(
        _combine_kernel,
        grid=(n_tiles + 1,),
        in_specs=[idx_spec, idx_spec,
                  pl.BlockSpec((tm, d), lambda s: (done(s), 0)),
                  pl.BlockSpec((1, 1, d), lambda s: (done(s) // per_b, 0, 5)),
                  pl.BlockSpec((tm, GATE_W), lambda s: (done(s), 0)),
                  pl.BlockSpec(memory_space=pl.ANY)],
        out_specs=pl.BlockSpec((tm, d), lambda s: (done(s), 0)),
        out_shape=jax.ShapeDtypeStruct((n, d), F32),
        scratch_shapes=[pltpu.VMEM((2, 2, tm, d), F32), pltpu.SemaphoreType.DMA((2,))],
        compiler_params=_cparams(("arbitrary",)),
        name="combine",
    )(dest0, dest1, x2d, mod3, slab, yb)


_OFF_MI = 4 * GROUP_W
_OFF_BQ = _OFF_MI + 2 * HEADS
_OFF_NG = _OFF_BQ + 14 * GROUP_W


def _pack_w_kernel(w_ref, o_ref, og_ref):
    gap = _OFF_BQ - _OFF_MI
    o_ref[0, :, 0:_OFF_MI] = w_ref[0, :, 0:_OFF_MI].astype(o_ref.dtype)
    tail = w_ref[0, :, _OFF_MI:]
    width = tail.shape[1]
    o_ref[0, :, _OFF_MI:] = pltpu.roll(tail, width - gap, axis=1)[:, 0:D_WIDE - _OFF_MI].astype(o_ref.dtype)
    lane = lax.broadcasted_iota(jnp.int32, (w_ref.shape[1], GATE_W), 1)
    first = w_ref[0, :, _OFF_MI:_OFF_MI + GATE_W]
    ragged = w_ref[0, :, _OFF_NG - GL_NG:_OFF_NG - GL_NG + GATE_W]
    gate = jnp.where(lane < GL_NG, first, jnp.where(lane < GL_NG + 3 * HEADS, ragged, 0.0))
    og_ref[0] = gate.astype(og_ref.dtype)


def _pack_w_in(w_in):
    depth, d, d_in = w_in.shape
    tk = 128
    lanes_in = -(-d_in // GATE_W) * GATE_W
    assert (_OFF_NG - GL_NG) % GATE_W == 0 and _OFF_NG - GL_NG + GATE_W == lanes_in
    return pl.pallas_call(
        _pack_w_kernel,
        grid=(depth, d // tk),
        in_specs=[pl.BlockSpec((1, tk, lanes_in), lambda l, i: (l, i, 0))],
        out_specs=[pl.BlockSpec((1, tk, D_WIDE), lambda l, i: (l, i, 0)),
                   pl.BlockSpec((1, tk, GATE_W), lambda l, i: (l, i, 0))],
        out_shape=[jax.ShapeDtypeStruct((depth, d, D_WIDE), MXU_DTYPE),
                   jax.ShapeDtypeStruct((depth, d, GATE_W), MXU_DTYPE)],
        compiler_params=_cparams(("arbitrary", "arbitrary")),
        name="pack_w_in",
    )(w_in)


def _moe(x2d, norm_g, mod3, wg, bg, we, be, w1, w3, w2, layer, seq):
    n, d = x2d.shape
    n_route = N_GROUPS + N_EXPERTS
    w_router = jnp.concatenate([wg, we, jnp.zeros((d, GATE_W - n_route), wg.dtype)], axis=1).astype(MXU_DTYPE)
    b_router = jnp.concatenate([bg, be, jnp.zeros((GATE_W - n_route,), bg.dtype)]).reshape(1, GATE_W)
    h2, slab, cnt = _route(x2d, norm_g, mod3, w_router, b_router, seq)

    rows = EXPERT_ROWS
    counts = cnt[0, :N_EXPERTS].astype(jnp.int32)
    pcounts = (counts + rows - 1) // rows * rows
    pends = jnp.cumsum(pcounts)
    pstarts = pends - pcounts
    eid = slab[:, SLAB_E0:SLAB_E1 + 1].astype(jnp.int32)
    rank = slab[:, SLAB_R0:SLAB_R1 + 1].astype(jnp.int32)
    expert_ids = jnp.arange(N_EXPERTS, dtype=jnp.int32)
    dest = jnp.sum(jnp.where(eid[..., None] == expert_ids, pstarts, 0), axis=-1) + rank
    n_blocks = -(-2 * n // rows) + N_EXPERTS
    blk_row0 = jnp.arange(n_blocks, dtype=jnp.int32) * rows
    blk_e = jnp.minimum(jnp.sum((pends[None, :] <= blk_row0[:, None]).astype(jnp.int32), axis=1), N_EXPERTS - 1)
    n_used = (pends[-1:] // rows).astype(jnp.int32)

    cap = n_blocks * rows
    pad_lo = jnp.concatenate([pstarts + counts, pends[-1:]]).astype(jnp.int32)
    pad_hi = jnp.concatenate([pends, jnp.full((1,), cap, jnp.int32)]).astype(jnp.int32)
    inv = _invert(dest.reshape(-1), pad_lo, pad_hi, cap)
    yb = _experts(h2, inv, blk_e, n_used, w1, w3, w2, layer)
    return _combine(x2d, mod3, slab, dest[:, 0], dest[:, 1], yb, seq)


def _layer(x2d, mod, bsz, seq, layer, norm1_g, norm2_g, w_wide, w_gate, mlstm_gate_b, mlstm_conv_w, mlstm_out_g,
           moba_qk_g, ret_out_g, nsa_q_g, nsa_k_g, nsa_cmp_pos, nsa_cmp_w1, nsa_cmp_w2, w_out, router_g_w,
           router_g_b, router_e_w, router_e_b, exp_w1, exp_w3, exp_w2, slopes, log_gamma):
    n, d = x2d.shape
    mod3 = mod.reshape(bsz, 1, 6 * d)
    z, zg = _norm_in_proj(x2d, norm1_g, mod3, w_wide, layer, w_gate, seq)
    z3 = z.reshape(bsz, seq, D_WIDE)
    zg3 = zg.reshape(bsz, seq, GATE_W)
    y_m = _mlstm(z3, zg3, mlstm_gate_b, mlstm_conv_w, mlstm_out_g)
    y_b = _moba(z3, moba_qk_g, slopes[0::2])
    y_r = _retention(z3, log_gamma, ret_out_g)
    y_n = _nsa(z3, zg3, nsa_q_g, nsa_k_g, nsa_cmp_pos, nsa_cmp_w1, nsa_cmp_w2, slopes[1::2])
    x2d = _out_proj((y_m, y_b, y_r, y_n), w_out, layer, x2d, mod3, seq)
    return _moe(x2d, norm2_g, mod3, router_g_w, router_g_b, router_e_w, router_e_b, exp_w1, exp_w3, exp_w2,
                layer, seq)


def kernel(x, c, norm1_g, norm2_g, ada_w, ada_b, w_in, mlstm_gate_b, mlstm_conv_w, mlstm_out_g, moba_qk_g,
           ret_out_g, nsa_q_g, nsa_k_g, nsa_cmp_pos, nsa_cmp_w1, nsa_cmp_w2, w_out, router_g_w, router_g_b,
           router_e_w, router_e_b, exp_w1, exp_w3, exp_w2):
    bsz, seq, d = x.shape
    depth = ada_w.shape[0]
    n_softmax_heads = 2 * HEADS
    slopes = jnp.exp2(-8.0 * jnp.arange(1, n_softmax_heads + 1, dtype=F32) / n_softmax_heads)
    log_gamma = jnp.log(1.0 - jnp.exp2(-5.0 - jnp.arange(HEADS, dtype=F32)))
    mod = _ada_mod(c, ada_w, ada_b)
    w_wide, w_gate = _pack_w_in(w_in)
    w_out_b = w_out.astype(MXU_DTYPE)
    x2d = x.reshape(bsz * seq, d)
    for l in range(depth):
        x2d = _layer(x2d, mod[l], bsz, seq, l, norm1_g[l], norm2_g[l], w_wide, w_gate, mlstm_gate_b[l],
                     mlstm_conv_w[l], mlstm_out_g[l], moba_qk_g[l], ret_out_g[l], nsa_q_g[l], nsa_k_g[l],
                     nsa_cmp_pos[l], nsa_cmp_w1[l], nsa_cmp_w2[l], w_out_b, router_g_w[l], router_g_b[l],
                     router_e_w[l], router_e_b[l], exp_w1, exp_w3, exp_w2, slopes, log_gamma)
    return x2d.reshape(bsz, seq, d)
```

```python
import functools

import jax
import jax.numpy as jnp
from jax import lax
from jax.experimental import pallas as pl
from jax.experimental.pallas import tpu as pltpu

F32 = jnp.float32
BF16 = jnp.bfloat16
MXU_DTYPE = jnp.bfloat16

D_MODEL = 2048
HEAD_DIM = 128
HEADS = 4
GROUP_W = HEADS * HEAD_DIM
N_WIDE_GROUPS = 18
D_WIDE = N_WIDE_GROUPS * GROUP_W
GATE_W = 128

MLSTM_CHUNK = 64
MLSTM_CONV = 4
MOBA_BLOCK = 256
MOBA_SHIFT = 8
MOBA_TOPK = 3
RET_CHUNK = 128
NSA_CMP_LEN = 32
NSA_CMP_STRIDE = 16
NSA_SEL_BLOCK = 64
NSA_SEL_SHIFT = 6
NSA_SEL_TOPN = 4
NSA_WINDOW = 512
N_GROUPS = 4
EXPERTS_PER_GROUP = 8
N_EXPERTS = N_GROUPS * EXPERTS_PER_GROUP
D_EXPERT = 512
EXPERT_ROWS = 256
EXPERT_GATHER_AHEAD = 2

NORM_EPS = 1e-6
NEG = -1e30
BIG = 1e9

G_MQ, G_MK, G_MV, G_MO = 0, 1, 2, 3
G_BQ, G_BK, G_BV = 4, 5, 6
G_RQ, G_RK, G_RV, G_RG = 7, 8, 9, 10
G_NQ, G_NKC, G_NVC, G_NKS, G_NVS, G_NKW, G_NVW = 11, 12, 13, 14, 15, 16, 17
GL_MI, GL_MF, GL_NG = 0, 4, 8

VMEM_LIMIT = 56 * 1024 * 1024


def _cparams(sem):
    return pltpu.CompilerParams(dimension_semantics=sem, vmem_limit_bytes=VMEM_LIMIT)


def _dot(a, b):
    return jnp.dot(a.astype(MXU_DTYPE), b.astype(MXU_DTYPE), preferred_element_type=F32)


def _dot_nt(a, b):
    return lax.dot_general(a.astype(MXU_DTYPE), b.astype(MXU_DTYPE), (((1,), (1,)), ((), ())),
                           preferred_element_type=F32)


def _dot_tn(a, b):
    return lax.dot_general(a.astype(MXU_DTYPE), b.astype(MXU_DTYPE), (((0,), (0,)), ((), ())),
                           preferred_element_type=F32)


def _dot_split_nt(b01, a):
    a_hi = a.astype(MXU_DTYPE)
    r1 = a - a_hi.astype(F32)
    a_mid = r1.astype(MXU_DTYPE)
    a_lo = (r1 - a_mid.astype(F32)).astype(MXU_DTYPE)
    return _dot_nt(b01, a_hi) + _dot_nt(b01, a_mid) + _dot_nt(b01, a_lo)


def _rms(x, g):
    return x * lax.rsqrt(jnp.mean(x * x, axis=-1, keepdims=True) + NORM_EPS) * g


def _sigmoid(x):
    return jax.nn.sigmoid(x)


def _lane_col(x, idx):
    lane = lax.broadcasted_iota(jnp.int32, x.shape, 1)
    return jnp.sum(jnp.where(lane == idx, x, 0.0), axis=1, keepdims=True)


def _ada_kernel(c_ref, w_ref, b_ref, o_ref):
    c = c_ref[...]
    o_ref[0] = _dot(c * _sigmoid(c), w_ref[0]) + b_ref[0]


def _ada_mod(c, ada_w, ada_b):
    depth, d, n6 = ada_w.shape
    b = c.shape[0]
    tn = 1024
    return pl.pallas_call(
        _ada_kernel,
        grid=(depth, n6 // tn),
        in_specs=[pl.BlockSpec((b, d), lambda l, j: (0, 0)),
                  pl.BlockSpec((1, d, tn), lambda l, j: (l, 0, j)),
                  pl.BlockSpec((1, 1, tn), lambda l, j: (l, 0, j))],
        out_specs=pl.BlockSpec((1, b, tn), lambda l, j: (l, 0, j)),
        out_shape=jax.ShapeDtypeStruct((depth, b, n6), F32),
        compiler_params=_cparams(("arbitrary", "arbitrary")),
        name="ada_mod",
    )(c, ada_w, ada_b.reshape(depth, 1, n6))


def _norm_in_kernel(x_ref, g_ref, sc_ref, sh_ref, w_ref, ws_ref, z_ref, zg_ref, h_s):
    @pl.when(pl.program_id(1) == 0)
    def _():
        h = _rms(x_ref[...], g_ref[...]) * (1.0 + sc_ref[0]) + sh_ref[0]
        hb = h.astype(MXU_DTYPE)
        h_s[...] = hb
        zg_ref[...] = jnp.dot(hb, ws_ref[0], preferred_element_type=F32)

    z_ref[...] = jnp.dot(h_s[...], w_ref[0], preferred_element_type=F32)


def _norm_in_proj(x2d, norm_g, mod3, w_wide, layer, w_gate, seq):
    n, d = x2d.shape
    tm, tn = 1024, 1024
    per_b = seq // tm
    return pl.pallas_call(
        _norm_in_kernel,
        grid=(n // tm, D_WIDE // tn),
        in_specs=[pl.BlockSpec((tm, d), lambda i, j: (i, 0)),
                  pl.BlockSpec((1, d), lambda i, j: (0, 0)),
                  pl.BlockSpec((1, 1, d), lambda i, j: (i // per_b, 0, 1)),
                  pl.BlockSpec((1, 1, d), lambda i, j: (i // per_b, 0, 0)),
                  pl.BlockSpec((1, d, tn), lambda i, j: (layer, 0, j)),
                  pl.BlockSpec((1, d, GATE_W), lambda i, j: (layer, 0, 0))],
        out_specs=[pl.BlockSpec((tm, tn), lambda i, j: (i, j)),
                   pl.BlockSpec((tm, GATE_W), lambda i, j: (i, 0))],
        out_shape=[jax.ShapeDtypeStruct((n, D_WIDE), F32),
                   jax.ShapeDtypeStruct((n, GATE_W), F32)],
        scratch_shapes=[pltpu.VMEM((tm, d), MXU_DTYPE)],
        compiler_params=_cparams(("arbitrary", "arbitrary")),
        name="norm_in_proj",
    )(x2d, norm_g.reshape(1, d), mod3, mod3, w_wide, w_gate)


REC_HEADS_PER_STEP = 2


def _mlstm_kernel(gb_ref, q_ref, k_ref, v_ref, o_ref, zg_ref, cwq_ref, cwk_ref, og_ref, y_ref,
                  pad_s, qs_s, ks_s, ic_s, fc_s):
    hp = pl.program_id(1)
    t = q_ref.shape[1]
    cl = MLSTM_CHUNK
    nc = t // cl
    hps = REC_HEADS_PER_STEP

    def conv_silu(src_ref, cw_ref, cols, dst_s, scale):
        pad_s[8:8 + t, :] = src_ref[0, :, cols]
        off = 8 - (MLSTM_CONV - 1)
        tile = 128
        for r0 in range(0, t, tile):
            acc = cw_ref[0:1, cols] * pad_s[r0 + off:r0 + off + tile, :]
            for j in range(1, MLSTM_CONV):
                acc = acc + cw_ref[j:j + 1, cols] * pad_s[r0 + off + j:r0 + off + j + tile, :]
            dst_s[r0:r0 + tile, :] = acc * _sigmoid(acc) * scale

    pad_s[0:8, :] = jnp.zeros((8, HEAD_DIM), F32)
    zg = zg_ref[0]
    for hh in range(hps):
        h = hp * hps + hh
        conv_silu(q_ref, cwq_ref, _head_cols(hh), qs_s.at[hh], HEAD_DIM ** -0.5)
        conv_silu(k_ref, cwk_ref, _head_cols(hh), ks_s.at[hh], 1.0)
        ic_s[hh] = _lane_col(zg, GL_MI + h) + gb_ref[0, h]
        f_pre = _lane_col(zg, GL_MF + h) + gb_ref[1, h]
        fc_s[hh] = jnp.minimum(f_pre, 0.0) - jnp.log1p(jnp.exp(-jnp.abs(f_pre)))

    rr = lax.broadcasted_iota(jnp.int32, (cl, cl), 0)
    cc = lax.broadcasted_iota(jnp.int32, (cl, cl), 1)
    eye = rr == cc
    causal = cc <= rr
    og = og_ref[...]

    def head_step(hh, c, carry):
        c_st, n_st, m_st = carry
        sl = pl.ds(pl.multiple_of(c * cl, cl), cl)
        cols = _head_cols(hh)
        qc = qs_s[hh, sl, :]
        kc = ks_s[hh, sl, :]
        vc = v_ref[0, sl, cols]
        i_col = ic_s[hh, sl, :]
        f_col = fc_s[hh, sl, :]
        f_row = jnp.sum(jnp.where(eye, f_col, 0.0), axis=0, keepdims=True)
        i_row = jnp.sum(jnp.where(eye, i_col, 0.0), axis=0, keepdims=True)
        a_col = jnp.sum(jnp.where(causal, f_row, 0.0), axis=1, keepdims=True)
        a_row = jnp.sum(jnp.where(rr <= cc, f_col, 0.0), axis=0, keepdims=True)
        log_d = jnp.where(causal, a_col - a_row + i_row, NEG)
        m_inter = a_col + m_st
        m_row = jnp.maximum(m_inter, jnp.max(log_d, axis=1, keepdims=True))
        s = _dot_nt(qc, kc) * jnp.exp(log_d - m_row)
        w_inter = jnp.exp(m_inter - m_row)
        num = _dot(s, vc) + w_inter * _dot(qc, c_st)
        den = jnp.sum(s, axis=1, keepdims=True) + w_inter * jnp.sum(qc * n_st, axis=1, keepdims=True)
        h_out = num / jnp.maximum(jnp.abs(den), jnp.exp(-m_row))
        a_last = jnp.sum(f_col, axis=0, keepdims=True)
        w_log = a_last - a_col + i_col
        m_new = jnp.maximum(a_last + m_st, jnp.max(w_log, axis=0, keepdims=True))
        w = jnp.exp(w_log - m_new)
        decay = jnp.exp(a_last + m_st - m_new)
        kw = kc * w
        c_new = decay * c_st + _dot_tn(kw, vc)
        n_new = decay * n_st + jnp.sum(kw, axis=0, keepdims=True)
        y = _rms(h_out, og) * _sigmoid(o_ref[0, sl, cols])
        y_ref[0, sl, cols] = y.astype(y_ref.dtype)
        return c_new, n_new, m_new

    def body(c, carry):
        return tuple(head_step(hh, c, carry[hh]) for hh in range(hps))

    carry0 = (jnp.zeros((HEAD_DIM, HEAD_DIM), F32), jnp.zeros((1, HEAD_DIM), F32), jnp.zeros((1, 1), F32))
    lax.fori_loop(0, nc, body, (carry0,) * hps, unroll=4)


def _col_spec(seq, group, hps):
    return pl.BlockSpec((1, seq, hps * HEAD_DIM), lambda b, h: (b, 0, group * (HEADS // hps) + h))


def _mlstm(z3, zg3, gate_b, conv_w, out_g):
    bsz, seq, _ = z3.shape
    hps = REC_HEADS_PER_STEP
    wide = hps * HEAD_DIM
    per_group = HEADS // hps
    smem = pl.BlockSpec(memory_space=pltpu.SMEM)
    return pl.pallas_call(
        _mlstm_kernel,
        grid=(bsz, per_group),
        in_specs=[smem,
                  _col_spec(seq, G_MQ, hps), _col_spec(seq, G_MK, hps), _col_spec(seq, G_MV, hps),
                  _col_spec(seq, G_MO, hps),
                  pl.BlockSpec((1, seq, GATE_W), lambda b, h: (b, 0, 0)),
                  pl.BlockSpec((MLSTM_CONV, wide), lambda b, h: (0, h)),
                  pl.BlockSpec((MLSTM_CONV, wide), lambda b, h: (0, per_group + h)),
                  pl.BlockSpec((1, HEAD_DIM), lambda b, h: (0, 0))],
        out_specs=pl.BlockSpec((1, seq, wide), lambda b, h: (b, 0, h)),
        out_shape=jax.ShapeDtypeStruct((bsz, seq, GROUP_W), BF16),
        scratch_shapes=[pltpu.VMEM((seq + 8, HEAD_DIM), F32), pltpu.VMEM((hps, seq, HEAD_DIM), F32),
                        pltpu.VMEM((hps, seq, HEAD_DIM), F32), pltpu.VMEM((hps, seq, 1), F32),
                        pltpu.VMEM((hps, seq, 1), F32)],
        compiler_params=_cparams(("arbitrary", "arbitrary")),
        name="mlstm",
    )(gate_b, z3, z3, z3, z3, zg3, conv_w, conv_w, out_g.reshape(1, HEAD_DIM))


def _ret_kernel(lg_ref, q_ref, k_ref, v_ref, g_ref, og_ref, y_ref):
    t = q_ref.shape[1]
    cl = RET_CHUNK
    rr = lax.broadcasted_iota(jnp.int32, (cl, cl), 0)
    cc = lax.broadcasted_iota(jnp.int32, (cl, cl), 1)
    diff = (rr - cc).astype(F32)
    jcol = lax.broadcasted_iota(jnp.int32, (cl, 1), 0).astype(F32)
    og = og_ref[...]
    consts = []
    for h in range(HEADS):
        lg = lg_ref[h]
        consts.append(dict(decay_in=jnp.where(diff >= 0, jnp.exp(lg * jnp.maximum(diff, 0.0)), 0.0),
                           zeta=jnp.exp(lg * (cl - 1.0 - jcol)), xi=jnp.exp(lg * (jcol + 1.0)),
                           g_chunk=jnp.exp(jnp.full((1, 1), lg * cl, F32))))

    def head_step(h, c, r_st):
        sl = pl.ds(pl.multiple_of(c * cl, cl), cl)
        cols = _head_cols(h)
        qc = q_ref[0, sl, cols]
        kc = k_ref[0, sl, cols] * HEAD_DIM ** -0.5
        vc = v_ref[0, sl, cols]
        scores = _dot_nt(qc, kc) * consts[h]["decay_in"]
        o = _dot(scores, vc) + _dot(qc, r_st) * consts[h]["xi"]
        r_new = consts[h]["g_chunk"] * r_st + _dot_tn(kc * consts[h]["zeta"], vc)
        gg = g_ref[0, sl, cols]
        y = _rms(o, og) * (gg * _sigmoid(gg))
        y_ref[0, sl, cols] = y.astype(y_ref.dtype)
        return r_new

    def body(c, carry):
        return tuple(head_step(h, c, carry[h]) for h in range(HEADS))

    lax.fori_loop(0, t // cl, body, (jnp.zeros((HEAD_DIM, HEAD_DIM), F32),) * HEADS, unroll=4)


def _retention(z3, log_gamma, out_g):
    bsz, seq, _ = z3.shape
    grp = lambda group: pl.BlockSpec((1, seq, GROUP_W), lambda b: (b, 0, group))
    return pl.pallas_call(
        _ret_kernel,
        grid=(bsz,),
        in_specs=[pl.BlockSpec(memory_space=pltpu.SMEM), grp(G_RQ), grp(G_RK), grp(G_RV), grp(G_RG),
                  pl.BlockSpec((1, HEAD_DIM), lambda b: (0, 0))],
        out_specs=pl.BlockSpec((1, seq, GROUP_W), lambda b: (b, 0, 0)),
        out_shape=jax.ShapeDtypeStruct((bsz, seq, GROUP_W), BF16),
        compiler_params=_cparams(("arbitrary",)),
        name="retention",
    )(log_gamma, z3, z3, z3, z3, out_g.reshape(1, HEAD_DIM))


BF16_SUBLANES = 16
ATT_TQ = 256
STRIP_W = 512
STRIP_SHIFT = 9
HEADS_PER_STEP = 2
MOBA_HEADS_PER_STEP = 4


def _rank_rows(vals):
    n = vals.shape[0]
    rowb = lax.broadcasted_iota(jnp.int32, vals.shape, 0)
    rank = jnp.zeros(vals.shape, jnp.int32)
    for jp in range(n):
        rv = vals[jp:jp + 1, :]
        beats = (rv > vals) | ((rv == vals) & (rowb > jp))
        rank = rank + beats.astype(jnp.int32)
    return rank


def _top_rows(vals, k):
    rowb = lax.broadcasted_iota(jnp.int32, vals.shape, 0).astype(F32)
    picked = jnp.zeros(vals.shape, jnp.bool_)
    for _ in range(k):
        best = jnp.max(vals, axis=0, keepdims=True)
        first = jnp.min(jnp.where(vals == best, rowb, float(vals.shape[0])), axis=0, keepdims=True)
        hit = (rowb == first) & (best > NEG)
        picked = picked | hit
        vals = jnp.where(hit, NEG, vals)
    return picked


def _pad_rows(x, rows):
    return jnp.concatenate([x, jnp.zeros((rows - x.shape[0], x.shape[1]), x.dtype)], axis=0)


def _fold_lanes(op, acc, x):
    for b in range(x.shape[1] // HEAD_DIM):
        acc = op(acc, x[:, b * HEAD_DIM:(b + 1) * HEAD_DIM])
    return acc


def _block_masked_attention(heads, qi, strip_s):
    tq, w = ATT_TQ, STRIP_W
    scale = HEAD_DIM ** -0.5
    t0 = qi * tq
    last = jnp.right_shift(t0, STRIP_SHIFT)
    rr = lax.broadcasted_iota(jnp.int32, (tq, w), 0)
    cc = lax.broadcasted_iota(jnp.int32, (tq, w), 1)
    rel = rr - cc
    rel_f = rel.astype(F32)
    alibi = [(-hd["slope"]) * rel_f for hd in heads]
    shift = heads[0]["blk_shift"]
    krow = lax.broadcasted_iota(jnp.int32, (w, HEAD_DIM), 0)
    klane = lax.broadcasted_iota(jnp.int32, (w, HEAD_DIM), 1)
    q_aug = []
    for hd in heads:
        unsel_t = _pad_rows(hd["unsel"].astype(F32), HEAD_DIM).T
        q_aug.append(jnp.concatenate([hd["qb"], unsel_t.astype(hd["qb"].dtype)], axis=1))

    def key_mask_cols(c):
        return jnp.where(klane == jnp.right_shift(c * w + krow, shift), NEG, 0.0)

    def scores(hi, hd, c, bias, mask_cols):
        kc = hd["k_s"][pl.ds(pl.multiple_of(c * w, w), w), :]
        k_aug = jnp.concatenate([kc, mask_cols.astype(kc.dtype)], axis=1)
        return _dot_nt(q_aug[hi], k_aug) * scale + bias

    def first_pass(c, ms):
        out = []
        mask_cols = key_mask_cols(c)
        for hi, hd in enumerate(heads):
            s = scores(hi, hd, c, alibi[hi] + (-hd["slope"]) * (t0 - c * w).astype(F32), mask_cols)
            strip_s[hi, c] = s
            out.append(_fold_lanes(jnp.maximum, ms[hi], s))
        return tuple(out)

    ms = lax.fori_loop(0, last, first_pass, tuple(jnp.full((tq, HEAD_DIM), NEG, F32) for _ in heads))
    dist = rel + (t0 - last * w)
    row_max = []
    last_mask_cols = key_mask_cols(last)
    for hi, hd in enumerate(heads):
        bias = jnp.where(dist >= 0, (-hd["slope"]) * dist.astype(F32), NEG)
        s = scores(hi, hd, last, bias, last_mask_cols)
        strip_s[hi, last] = s
        row_max.append(jnp.max(_fold_lanes(jnp.maximum, ms[hi], s), axis=1, keepdims=True))

    def second_pass(c, carry):
        out = []
        for hi, hd in enumerate(heads):
            l_run, acc = carry[hi]
            p = jnp.exp(strip_s[hi, c] - row_max[hi])
            vc = hd["v_s"][pl.ds(pl.multiple_of(c * w, w), w), :]
            out.append((_fold_lanes(jnp.add, l_run, p), acc + _dot(p, vc)))
        return tuple(out)

    zero = jnp.zeros((tq, HEAD_DIM), F32)
    res = lax.fori_loop(0, last + 1, second_pass, tuple((zero, zero) for _ in heads))
    return [acc / jnp.sum(l_run, axis=1, keepdims=True) for l_run, acc in res]


def _head_cols(hh):
    return slice(hh * HEAD_DIM, (hh + 1) * HEAD_DIM)


def _moba_kernel(slope_ref, q_ref, k_ref, v_ref, g_ref, y_ref, kn_s, vb_s, kmean_s, strip_s):
    hp = pl.program_id(1)
    qi = pl.program_id(2)
    t = k_ref.shape[1]
    blk = MOBA_BLOCK
    nb = t // blk

    @pl.when(qi == 0)
    def _():
        for hh in range(MOBA_HEADS_PER_STEP):
            kn = _rms(k_ref[0, :, _head_cols(hh)], g_ref[1:2, :])
            kn_s[hh] = kn.astype(kn_s.dtype)
            vb_s[hh] = v_ref[0, :, _head_cols(hh)].astype(vb_s.dtype)
            kmean_s[hh] = jnp.zeros(kmean_s.shape[1:], F32)
            for j in range(nb):
                kmean_s[hh, j:j + 1, :] = jnp.mean(kn[j * blk:(j + 1) * blk, :], axis=0, keepdims=True)

    rowb = lax.broadcasted_iota(jnp.int32, (nb, ATT_TQ), 0)
    heads = []
    for hh in range(MOBA_HEADS_PER_STEP):
        qn = _rms(q_ref[0, :, _head_cols(hh)], g_ref[0:1, :])
        gate = jnp.where(rowb < qi, _dot_nt(kmean_s[hh], qn)[0:nb, :], NEG)
        sel = (_rank_rows(gate) < MOBA_TOPK) & (rowb < qi)
        unsel = jnp.where(sel | (rowb == qi), 0.0, 1.0)
        heads.append(dict(qb=qn.astype(MXU_DTYPE), k_s=kn_s.at[hh], v_s=vb_s.at[hh],
                          slope=slope_ref[hp * MOBA_HEADS_PER_STEP + hh], blk_shift=MOBA_SHIFT,
                          unsel=_pad_rows(unsel, BF16_SUBLANES).astype(BF16)))
    outs = _block_masked_attention(heads, qi, strip_s)
    for hh in range(MOBA_HEADS_PER_STEP):
        y_ref[0, :, _head_cols(hh)] = outs[hh].astype(y_ref.dtype)


def _moba(z3, qk_g, slopes):
    bsz, seq, _ = z3.shape
    assert MOBA_BLOCK == ATT_TQ and seq % STRIP_W == 0
    hps = MOBA_HEADS_PER_STEP
    wide = hps * HEAD_DIM
    per_group = HEADS // hps
    kv = lambda group: pl.BlockSpec((1, seq, wide), lambda b, h, i: (b, 0, group * per_group + h))
    return pl.pallas_call(
        _moba_kernel,
        grid=(bsz, per_group, seq // ATT_TQ),
        in_specs=[pl.BlockSpec(memory_space=pltpu.SMEM),
                  pl.BlockSpec((1, ATT_TQ, wide), lambda b, h, i: (b, i, G_BQ * per_group + h)),
                  kv(G_BK), kv(G_BV),
                  pl.BlockSpec((2, HEAD_DIM), lambda b, h, i: (0, 0))],
        out_specs=pl.BlockSpec((1, ATT_TQ, wide), lambda b, h, i: (b, i, h)),
        out_shape=jax.ShapeDtypeStruct((bsz, seq, GROUP_W), BF16),
        scratch_shapes=[pltpu.VMEM((hps, seq, HEAD_DIM), MXU_DTYPE), pltpu.VMEM((hps, seq, HEAD_DIM), MXU_DTYPE),
                        pltpu.VMEM((hps, HEAD_DIM, HEAD_DIM), F32),
                        pltpu.VMEM((hps, seq // STRIP_W, ATT_TQ, STRIP_W), F32)],
        compiler_params=_cparams(("arbitrary", "arbitrary", "arbitrary")),
        name="moba",
    )(slopes, z3, z3, z3, qk_g)


def _gelu_tanh(x):
    return 0.5 * x * (1.0 + jnp.tanh(0.7978845608028654 * (x + 0.044715 * (x * x * x))))


def _nsa_kernel(slope_ref, q_ref, kc0_ref, kc1_ref, vc0_ref, vc1_ref, ks_ref, vs_ref, kw_ref, vw_ref, zg_ref,
                qg_ref, kg_ref, pe_ref, w1_ref, w2_ref, y_ref,
                kcmp_s, vcmp_s, ksn_s, vsb_s, kwn_s, vwb_s, strip_s):
    kc_refs, vc_refs = (kc0_ref, kc1_ref), (vc0_ref, vc1_ref)
    hp = pl.program_id(1)
    qi = pl.program_id(2)
    t = ks_ref.shape[1]
    tq = ATT_TQ
    nsub = t // NSA_CMP_STRIDE
    n_cmp = nsub - 1
    n_sel = t // NSA_SEL_BLOCK
    scale = HEAD_DIM ** -0.5

    @pl.when(qi == 0)
    def _():
        for hh in range(HEADS_PER_STEP):
            cols = _head_cols(hh)
            for cv, (src, dst) in enumerate(((kc_refs[hh], kcmp_s), (vc_refs[hh], vcmp_s))):
                acc_a = jnp.zeros((nsub, HEAD_DIM), F32)
                acc_b = jnp.zeros((nsub, HEAD_DIM), F32)
                for r in range(NSA_CMP_STRIDE):
                    zr = src[0, pl.ds(r, nsub, stride=NSA_CMP_STRIDE), :]
                    acc_a = acc_a + _dot(zr + pe_ref[cv, r:r + 1, :], w1_ref[cv, r])
                    rb = NSA_CMP_STRIDE + r
                    acc_b = acc_b + _dot(zr + pe_ref[cv, rb:rb + 1, :], w1_ref[cv, rb])
                hid = _gelu_tanh(acc_a + pltpu.roll(acc_b, nsub - 1, axis=0))
                cmp = _dot(hid, w2_ref[cv])
                if cv == 0:
                    cmp = _rms(cmp, kg_ref[0:1, :])
                dst[hh] = cmp.astype(dst.dtype)
            ksn_s[hh] = _rms(ks_ref[0, :, cols], kg_ref[1:2, :]).astype(ksn_s.dtype)
            vsb_s[hh] = vs_ref[0, :, cols].astype(vsb_s.dtype)
            kwn_s[hh] = _rms(kw_ref[0, :, cols], kg_ref[2:3, :]).astype(kwn_s.dtype)
            vwb_s[hh] = vw_ref[0, :, cols].astype(vwb_s.dtype)

    t0 = qi * tq
    rowi = lax.broadcasted_iota(jnp.int32, (tq, HEAD_DIM), 0)
    lane = lax.broadcasted_iota(jnp.int32, (tq, HEAD_DIM), 1)
    dist_c = (t0 + rowi) - (lane * NSA_CMP_STRIDE + (NSA_CMP_LEN - 1))
    ok_c = (dist_c >= 0) & (lane < n_cmp)
    dist_cf = dist_c.astype(F32)

    ob = lax.broadcasted_iota(jnp.int32, (HEAD_DIM, nsub), 0)
    oc = lax.broadcasted_iota(jnp.int32, (HEAD_DIM, nsub), 1)
    overlap_t = ((oc * NSA_CMP_STRIDE <= ob * NSA_SEL_BLOCK + (NSA_SEL_BLOCK - 1))
                 & (oc * NSA_CMP_STRIDE + (NSA_CMP_LEN - 1) >= ob * NSA_SEL_BLOCK)
                 & (oc < n_cmp) & (ob < n_sel)).astype(F32)
    rowb = lax.broadcasted_iota(jnp.int32, (n_sel, tq), 0)
    cur = jnp.right_shift(t0 + lax.broadcasted_iota(jnp.int32, (n_sel, tq), 1), NSA_SEL_SHIFT)

    win_w = NSA_WINDOW + tq
    k0 = jnp.maximum(t0 - NSA_WINDOW, 0)
    wr = lax.broadcasted_iota(jnp.int32, (tq, win_w), 0)
    wc = lax.broadcasted_iota(jnp.int32, (tq, win_w), 1)
    dist_w = (t0 - k0) + wr - wc
    ok_w = (dist_w >= 0) & (dist_w < NSA_WINDOW)
    dist_wf = dist_w.astype(F32)
    win_rows = pl.ds(pl.multiple_of(k0, tq), win_w)

    heads, o_cmp, o_win = [], [], []
    for hh in range(HEADS_PER_STEP):
        slope = slope_ref[hp * HEADS_PER_STEP + hh]
        qn = _rms(q_ref[0, :, _head_cols(hh)], qg_ref[...])
        qb = qn.astype(MXU_DTYPE)

        s_c = jnp.where(ok_c, _dot_nt(qb, kcmp_s[hh]) * scale - slope * dist_cf, NEG)
        m_c = jnp.max(s_c, axis=1, keepdims=True)
        e_c = jnp.where(ok_c, jnp.exp(s_c - m_c), 0.0)
        p_c = e_c / jnp.maximum(jnp.sum(e_c, axis=1, keepdims=True), 1e-30)
        o_cmp.append(_dot(p_c, vcmp_s[hh]))

        imp = _dot_split_nt(overlap_t, p_c)[0:n_sel, :]
        sel = (rowb == cur) | _top_rows(jnp.where(rowb < cur, imp, NEG), NSA_SEL_TOPN - 1)
        unsel = jnp.where(sel, 0.0, 1.0)
        heads.append(dict(qb=qb, k_s=ksn_s.at[hh], v_s=vsb_s.at[hh], slope=slope, blk_shift=NSA_SEL_SHIFT,
                          unsel=unsel.astype(BF16)))

        s_w = _dot_nt(qb, kwn_s[hh, win_rows, :]) * scale + jnp.where(ok_w, (-slope) * dist_wf, NEG)
        p_w = jnp.exp(s_w - jnp.max(s_w, axis=1, keepdims=True))
        o_win.append(_dot(p_w, vwb_s[hh, win_rows, :]) / jnp.sum(p_w, axis=1, keepdims=True))

    o_slc = _block_masked_attention(heads, qi, strip_s)

    zg = zg_ref[0]
    for hh in range(HEADS_PER_STEP):
        h = hp * HEADS_PER_STEP + hh
        g_cmp = _sigmoid(_lane_col(zg, GL_NG + h))
        g_slc = _sigmoid(_lane_col(zg, GL_NG + HEADS + h))
        g_win = _sigmoid(_lane_col(zg, GL_NG + 2 * HEADS + h))
        y_ref[0, :, _head_cols(hh)] = (g_cmp * o_cmp[hh] + g_slc * o_slc[hh] + g_win * o_win[hh]).astype(y_ref.dtype)


def _nsa(z3, zg3, q_g, k_g, cmp_pos, cmp_w1, cmp_w2, slopes):
    bsz, seq, _ = z3.shape
    assert seq // NSA_CMP_STRIDE == HEAD_DIM, "compressed blocks are laid out on the 128 lanes"
    assert seq % STRIP_W == 0 and seq >= NSA_WINDOW + ATT_TQ
    tq = ATT_TQ
    hps = HEADS_PER_STEP
    wide = hps * HEAD_DIM
    per_group = HEADS // hps
    kv = lambda group: pl.BlockSpec((1, seq, wide), lambda b, h, i: (b, 0, group * per_group + h))
    one = lambda group, hh: pl.BlockSpec((1, seq, HEAD_DIM), lambda b, h, i: (b, 0, group * HEADS + h * hps + hh))
    full = lambda shape: pl.BlockSpec(shape, lambda b, h, i: (0,) * len(shape))
    w1 = cmp_w1.reshape(2, NSA_CMP_LEN, HEAD_DIM, HEAD_DIM).astype(MXU_DTYPE)
    w2 = cmp_w2.astype(MXU_DTYPE)
    seq_buf = lambda: pltpu.VMEM((hps, seq, HEAD_DIM), MXU_DTYPE)
    cmp_buf = lambda: pltpu.VMEM((hps, HEAD_DIM, HEAD_DIM), MXU_DTYPE)
    return pl.pallas_call(
        _nsa_kernel,
        grid=(bsz, per_group, seq // tq),
        in_specs=[pl.BlockSpec(memory_space=pltpu.SMEM),
                  pl.BlockSpec((1, tq, wide), lambda b, h, i: (b, i, G_NQ * per_group + h)),
                  one(G_NKC, 0), one(G_NKC, 1), one(G_NVC, 0), one(G_NVC, 1),
                  kv(G_NKS), kv(G_NVS), kv(G_NKW), kv(G_NVW),
                  pl.BlockSpec((1, tq, GATE_W), lambda b, h, i: (b, i, 0)),
                  full((1, HEAD_DIM)), full((3, HEAD_DIM)), full((2, NSA_CMP_LEN, HEAD_DIM)),
                  full((2, NSA_CMP_LEN, HEAD_DIM, HEAD_DIM)), full((2, HEAD_DIM, HEAD_DIM))],
        out_specs=pl.BlockSpec((1, tq, wide), lambda b, h, i: (b, i, h)),
        out_shape=jax.ShapeDtypeStruct((bsz, seq, GROUP_W), BF16),
        scratch_shapes=[cmp_buf(), cmp_buf(), seq_buf(), seq_buf(), seq_buf(), seq_buf(),
                        pltpu.VMEM((hps, seq // STRIP_W, tq, STRIP_W), F32)],
        compiler_params=_cparams(("arbitrary", "arbitrary", "arbitrary")),
        name="nsa",
    )(slopes, z3, z3, z3, z3, z3, z3, z3, z3, z3, zg3, q_g.reshape(1, HEAD_DIM), k_g, cmp_pos, w1, w2)


def _out_proj_kernel(ym_ref, yb_ref, yr_ref, yn_ref, w_ref, x_ref, g_ref, o_ref):
    acc = jnp.dot(ym_ref[...], w_ref[0, 0:GROUP_W, :], preferred_element_type=F32)
    acc = acc + jnp.dot(yb_ref[...], w_ref[0, GROUP_W:2 * GROUP_W, :], preferred_element_type=F32)
    acc = acc + jnp.dot(yr_ref[...], w_ref[0, 2 * GROUP_W:3 * GROUP_W, :], preferred_element_type=F32)
    acc = acc + jnp.dot(yn_ref[...], w_ref[0, 3 * GROUP_W:4 * GROUP_W, :], preferred_element_type=F32)
    o_ref[...] = x_ref[...] + g_ref[0] * acc


def _out_proj(ys, w_out, layer, x2d, mod3, seq):
    n, d = x2d.shape
    tm, tn = 1024, 1024
    per_b = seq // tm
    y_spec = pl.BlockSpec((tm, GROUP_W), lambda i, j: (i, 0))
    return pl.pallas_call(
        _out_proj_kernel,
        grid=(n // tm, d // tn),
        in_specs=[y_spec, y_spec, y_spec, y_spec,
                  pl.BlockSpec((1, 4 * GROUP_W, tn), lambda i, j: (layer, 0, j)),
                  pl.BlockSpec((tm, tn), lambda i, j: (i, j)),
                  pl.BlockSpec((1, 1, tn), lambda i, j: (i // per_b, 0, 2 * (d // tn) + j))],
        out_specs=pl.BlockSpec((tm, tn), lambda i, j: (i, j)),
        out_shape=jax.ShapeDtypeStruct((n, d), F32),
        compiler_params=_cparams(("arbitrary", "arbitrary")),
        name="out_proj",
    )(*[y.reshape(n, GROUP_W) for y in ys], w_out, x2d, mod3)


SLAB_E0, SLAB_E1, SLAB_R0, SLAB_R1, SLAB_G0, SLAB_G1 = 0, 1, 2, 3, 4, 5


def _route_kernel(x_ref, g_ref, sc_ref, sh_ref, wr_ref, br_ref, h_ref, slab_ref, cnt_ref, carry_s):
    tm = x_ref.shape[0]

    @pl.when(pl.program_id(0) == 0)
    def _():
        carry_s[...] = jnp.zeros(carry_s.shape, F32)

    hmod = _rms(x_ref[...], g_ref[...]) * (1.0 + sc_ref[0]) + sh_ref[0]
    _store_slabs(h_ref, hmod)
    logits = _dot(hmod, wr_ref[...]) + br_ref[...]
    lane = lax.broadcasted_iota(jnp.int32, logits.shape, 1).astype(F32)
    far = 4.0 * GATE_W

    in_g = lane < N_GROUPS
    lg = jnp.where(in_g, logits, NEG)
    g_max = jnp.max(lg, axis=1, keepdims=True)
    grp = jnp.min(jnp.where(in_g & (lg == g_max), lane, far), axis=1, keepdims=True)
    p_grp = 1.0 / jnp.sum(jnp.where(in_g, jnp.exp(lg - g_max), 0.0), axis=1, keepdims=True)

    lo = N_GROUPS + grp * EXPERTS_PER_GROUP
    in_e = (lane >= lo) & (lane < lo + EXPERTS_PER_GROUP)
    le = jnp.where(in_e, logits, NEG)
    e_max = jnp.max(le, axis=1, keepdims=True)
    ee = jnp.where(in_e, jnp.exp(le - e_max), 0.0)
    pe = jnp.where(in_e, ee / jnp.sum(ee, axis=1, keepdims=True), -1.0)
    p1 = jnp.max(pe, axis=1, keepdims=True)
    i1 = jnp.min(jnp.where(pe == p1, lane, far), axis=1, keepdims=True)
    pe2 = jnp.where(lane == i1, -1.0, pe)
    p2 = jnp.max(pe2, axis=1, keepdims=True)
    i2 = jnp.min(jnp.where(pe2 == p2, lane, far), axis=1, keepdims=True)
    e0 = i1 - N_GROUPS
    e1 = i2 - N_GROUPS
    g0 = p_grp * (p1 / (p1 + p2))
    g1 = p_grp * (p2 / (p1 + p2))

    onehot = ((lane == e0) | (lane == e1)).astype(BF16)
    rr = lax.broadcasted_iota(jnp.int32, (tm, tm), 0)
    cc = lax.broadcasted_iota(jnp.int32, (tm, tm), 1)
    before = jnp.dot((cc < rr).astype(BF16), onehot, preferred_element_type=F32) + carry_s[...]
    r0 = jnp.sum(jnp.where(lane == e0, before, 0.0), axis=1, keepdims=True)
    r1 = jnp.sum(jnp.where(lane == e1, before, 0.0), axis=1, keepdims=True)
    carry_s[...] = carry_s[...] + jnp.sum(onehot.astype(F32), axis=0, keepdims=True)
    cnt_ref[...] = carry_s[...]

    slab = jnp.where(lane == SLAB_E0, e0.astype(F32), 0.0)
    slab = jnp.where(lane == SLAB_E1, e1.astype(F32), slab)
    slab = jnp.where(lane == SLAB_R0, r0, slab)
    slab = jnp.where(lane == SLAB_R1, r1, slab)
    slab = jnp.where(lane == SLAB_G0, g0, slab)
    slab = jnp.where(lane == SLAB_G1, g1, slab)
    slab_ref[...] = slab


def _route(x2d, norm_g, mod3, w_router, b_router, seq):
    n, d = x2d.shape
    tm = 512
    per_b = seq // tm
    return pl.pallas_call(
        _route_kernel,
        grid=(n // tm,),
        in_specs=[pl.BlockSpec((tm, d), lambda i: (i, 0)),
                  pl.BlockSpec((1, d), lambda i: (0, 0)),
                  pl.BlockSpec((1, 1, d), lambda i: (i // per_b, 0, 4)),
                  pl.BlockSpec((1, 1, d), lambda i: (i // per_b, 0, 3)),
                  pl.BlockSpec((d, GATE_W), lambda i: (0, 0)),
                  pl.BlockSpec((1, GATE_W), lambda i: (0, 0))],
        out_specs=[pl.BlockSpec((tm * SLAB, HEAD_DIM), lambda i: (i, 0)),
                   pl.BlockSpec((tm, GATE_W), lambda i: (i, 0)),
                   pl.BlockSpec((1, GATE_W), lambda i: (0, 0))],
        out_shape=[jax.ShapeDtypeStruct((n * SLAB, HEAD_DIM), F32),
                   jax.ShapeDtypeStruct((n, GATE_W), F32),
                   jax.ShapeDtypeStruct((1, GATE_W), F32)],
        scratch_shapes=[pltpu.VMEM((1, GATE_W), F32)],
        compiler_params=_cparams(("arbitrary",)),
        name="route",
    )(x2d, norm_g.reshape(1, d), mod3, mod3, w_router, b_router)


def _invert_kernel(dest_ref, pad_lo_ref, pad_hi_ref, inv_ref):
    def clear(i, _):
        inv_ref[i] = 0
        return 0

    def clear_segment(g, _):
        lax.fori_loop(pad_lo_ref[g], pad_hi_ref[g], clear, 0)
        return 0

    def put(a, _):
        inv_ref[dest_ref[a]] = jnp.right_shift(a, 1)
        return 0

    lax.fori_loop(0, pad_lo_ref.shape[0], clear_segment, 0)
    lax.fori_loop(0, dest_ref.shape[0], put, 0, unroll=32)


def _invert(dest_flat, pad_lo, pad_hi, cap):
    smem = pl.BlockSpec(memory_space=pltpu.SMEM)
    return pl.pallas_call(
        _invert_kernel,
        in_specs=[smem, smem, smem],
        out_specs=smem,
        out_shape=jax.ShapeDtypeStruct((cap,), jnp.int32),
        name="invert",
    )(dest_flat, pad_lo, pad_hi)


def _start_row_gather(idx_ref, src_hbm, dst, sem, n_rows, first=0, span=1):
    for r in range(first, n_rows):
        start = idx_ref[r] if span == 1 else pl.multiple_of(idx_ref[r] * span, span)
        pltpu.make_async_copy(src_hbm.at[pl.ds(start, span)], dst.at[pl.ds(r * span, span)], sem).start(
            priority=r % 2)


def _wait_row_gather(src_hbm, dst, sem, n_rows, span=1):
    pltpu.make_async_copy(src_hbm.at[pl.ds(0, n_rows * span)], dst, sem).wait()


SLAB = D_MODEL // HEAD_DIM


def _store_slabs(ref, x):
    for c in range(SLAB):
        ref[pl.ds(c, x.shape[0], stride=SLAB), :] = x[:, c * HEAD_DIM:(c + 1) * HEAD_DIM]


def _load_slabs(ref, n_rows, lead=()):
    return jnp.concatenate([ref[lead + (pl.ds(c, n_rows, stride=SLAB), slice(None))] for c in range(SLAB)], axis=1)


def _expert_kernel(blk_e_ref, n_used_ref, run_slot_ref, next_e_ref, next_ok_ref, inv_ref, h_hbm,
                   w1_hbm, w3_hbm, w2_hbm, y_ref,
                   x_s, w1_f, w3_f, w2_f, w1_s, w3_s, w2_s, sem, wsem, *, layer):
    s = pl.program_id(0)
    rows = EXPERT_ROWS
    n_used = n_used_ref[0]
    ahead = EXPERT_GATHER_AHEAD
    slot = lax.rem(s, ahead + 1)
    cslot = lax.rem(s + 1, ahead + 1)

    blk = s - ahead
    prev = jnp.maximum(blk - 1, 0)
    gather = s < n_used
    compute = (s >= ahead) & (blk < n_used)

    def start_gather(first, stop):
        _start_row_gather(inv_ref, h_hbm, x_s.at[slot], sem.at[slot], stop, first, span=SLAB)

    def weight_copies(e, wslot):
        return (pltpu.make_async_copy(w1_hbm.at[layer, e], w1_f.at[wslot], wsem.at[wslot, 0]),
                pltpu.make_async_copy(w3_hbm.at[layer, e], w3_f.at[wslot], wsem.at[wslot, 1]),
                pltpu.make_async_copy(w2_hbm.at[layer, e], w2_f.at[wslot], wsem.at[wslot, 2]))

    def expert_block(gather_too):
        @pl.when((blk == 0) | (blk_e_ref[blk] != blk_e_ref[prev]))
        def _():
            wslot = run_slot_ref[blk]
            for cp in weight_copies(blk_e_ref[blk], wslot):
                cp.wait()

            @pl.when(next_ok_ref[blk] == 1)
            def _():
                for cp in weight_copies(next_e_ref[blk], 1 - wslot):
                    cp.start()

            w1_s[...] = w1_f[wslot].astype(w1_s.dtype)
            w3_s[...] = w3_f[wslot].astype(w3_s.dtype)
            w2_s[...] = w2_f[wslot].astype(w2_s.dtype)

        cuts = (0, rows // 4, rows // 2, rows) if gather_too else (0, 0, 0, 0)
        _wait_row_gather(h_hbm, x_s.at[cslot], sem.at[cslot], rows, span=SLAB)
        start_gather(cuts[0], cuts[1])
        x = _load_slabs(x_s, rows, (cslot,)).astype(MXU_DTYPE)
        a = jnp.dot(x, w1_s[...], preferred_element_type=F32)
        start_gather(cuts[1], cuts[2])
        b = jnp.dot(x, w3_s[...], preferred_element_type=F32)
        start_gather(cuts[2], cuts[3])
        y_ref[...] = _dot(a * _sigmoid(a) * b, w2_s[...])

    @pl.when(gather & compute)
    def _():
        expert_block(True)

    @pl.when(gather & jnp.logical_not(compute))
    def _():
        @pl.when(s == 0)
        def _():
            for cp in weight_copies(blk_e_ref[0], run_slot_ref[0]):
                cp.start()

        start_gather(0, rows)

    @pl.when(compute & jnp.logical_not(gather))
    def _():
        expert_block(False)

    @pl.when((s >= ahead) & (blk >= n_used))
    def _():
        y_ref[...] = jnp.zeros(y_ref.shape, y_ref.dtype)


def _experts(h2, inv, blk_e, n_used, w1, w3, w2, layer):
    d = D_MODEL
    rows = EXPERT_ROWS
    n_blocks = inv.shape[0] // rows
    idx = jnp.arange(n_blocks, dtype=jnp.int32)
    run_start = jnp.concatenate([jnp.ones((1,), jnp.int32), (blk_e[1:] != blk_e[:-1]).astype(jnp.int32)])
    run_slot = (jnp.cumsum(run_start) - 1) % 2
    later_run = (idx[None, :] > idx[:, None]) & (blk_e[None, :] != blk_e[:, None])
    next_start = jnp.min(jnp.where(later_run, idx[None, :], n_blocks), axis=1)
    next_e = blk_e[jnp.minimum(next_start, n_blocks - 1)]
    next_ok = (next_start < n_used[0]).astype(jnp.int32)

    ahead = EXPERT_GATHER_AHEAD
    done = lambda s: jnp.maximum(s - ahead, 0)
    hbm = pl.BlockSpec(memory_space=pl.ANY)
    return pl.pallas_call(
        functools.partial(_expert_kernel, layer=layer),
        grid_spec=pltpu.PrefetchScalarGridSpec(
            num_scalar_prefetch=5,
            grid=(n_blocks + ahead,),
            in_specs=[pl.BlockSpec((rows,), lambda s, *_: (jnp.minimum(s, n_blocks - 1),), memory_space=pltpu.SMEM),
                      hbm, hbm, hbm, hbm],
            out_specs=pl.BlockSpec((rows, d), lambda s, *_: (done(s), 0)),
            scratch_shapes=[pltpu.VMEM((ahead + 1, rows * SLAB, HEAD_DIM), F32),
                            pltpu.VMEM((2, d, D_EXPERT), F32), pltpu.VMEM((2, d, D_EXPERT), F32),
                            pltpu.VMEM((2, D_EXPERT, d), F32),
                            pltpu.VMEM((d, D_EXPERT), MXU_DTYPE), pltpu.VMEM((d, D_EXPERT), MXU_DTYPE),
                            pltpu.VMEM((D_EXPERT, d), MXU_DTYPE),
                            pltpu.SemaphoreType.DMA((ahead + 1,)), pltpu.SemaphoreType.DMA((2, 3))]),
        out_shape=jax.ShapeDtypeStruct((n_blocks * rows, d), F32),
        compiler_params=_cparams(("arbitrary",)),
        name="experts",
    )(blk_e, n_used, run_slot.astype(jnp.int32), next_e.astype(jnp.int32), next_ok, inv, h2, w1, w3, w2)


def _combine_kernel(d0_ref, d1_ref, x_ref, g_ref, slab_ref, yb_hbm, o_ref, rows_s, sem):
    s = pl.program_id(0)
    n_tiles = pl.num_programs(0) - 1
    tm = x_ref.shape[0]
    slot = lax.rem(s, 2)

    @pl.when(s < n_tiles)
    def _():
        _start_row_gather(d0_ref, yb_hbm, rows_s.at[slot, 0], sem.at[slot], tm)
        _start_row_gather(d1_ref, yb_hbm, rows_s.at[slot, 1], sem.at[slot], tm)

    @pl.when(s >= 1)
    def _():
        _wait_row_gather(yb_hbm, rows_s.at[1 - slot, 0], sem.at[1 - slot], tm)
        _wait_row_gather(yb_hbm, rows_s.at[1 - slot, 1], sem.at[1 - slot], tm)
        route = slab_ref[...]
        g0 = route[:, SLAB_G0:SLAB_G0 + 1]
        g1 = route[:, SLAB_G1:SLAB_G1 + 1]
        o_ref[...] = x_ref[...] + g_ref[0] * (g0 * rows_s[1 - slot, 0] + g1 * rows_s[1 - slot, 1])


def _combine(x2d, mod3, slab, dest0, dest1, yb, seq):
    n, d = x2d.shape
    tm = 256
    per_b = seq // tm
    n_tiles = n // tm
    done = lambda s: jnp.maximum(s - 1, 0)
    idx_spec = pl.BlockSpec((tm,), lambda s: (jnp.minimum(s, n_tiles - 1),), memory_space=pltpu.SMEM)
    return pl.pallas_call(
        _combine_kernel,
        grid=(n_tiles + 1,),
        in_specs=[idx_spec, idx_spec,
                  pl.BlockSpec((tm, d), lambda s: (done(s), 0)),
                  pl.BlockSpec((1, 1, d), lambda s: (done(s) // per_b, 0, 5)),
                  pl.BlockSpec((tm, GATE_W), lambda s: (done(s), 0)),
                  pl.BlockSpec(memory_space=pl.ANY)],
        out_specs=pl.BlockSpec((tm, d), lambda s: (done(s), 0)),
        out_shape=jax.ShapeDtypeStruct((n, d), F32),
        scratch_shapes=[pltpu.VMEM((2, 2, tm, d), F32), pltpu.SemaphoreType.DMA((2,))],
        compiler_params=_cparams(("arbitrary",)),
        name="combine",
    )(dest0, dest1, x2d, mod3, slab, yb)


_OFF_MI = 4 * GROUP_W
_OFF_BQ = _OFF_MI + 2 * HEADS
_OFF_NG = _OFF_BQ + 14 * GROUP_W


def _pack_w_kernel(w_ref, o_ref, og_ref):
    gap = _OFF_BQ - _OFF_MI
    o_ref[0, :, 0:_OFF_MI] = w_ref[0, :, 0:_OFF_MI].astype(o_ref.dtype)
    tail = w_ref[0, :, _OFF_MI:]
    width = tail.shape[1]
    o_ref[0, :, _OFF_MI:] = pltpu.roll(tail, width - gap, axis=1)[:, 0:D_WIDE - _OFF_MI].astype(o_ref.dtype)
    lane = lax.broadcasted_iota(jnp.int32, (w_ref.shape[1], GATE_W), 1)
    first = w_ref[0, :, _OFF_MI:_OFF_MI + GATE_W]
    ragged = w_ref[0, :, _OFF_NG - GL_NG:_OFF_NG - GL_NG + GATE_W]
    gate = jnp.where(lane < GL_NG, first, jnp.where(lane < GL_NG + 3 * HEADS, ragged, 0.0))
    og_ref[0] = gate.astype(og_ref.dtype)


def _pack_w_in(w_in):
    depth, d, d_in = w_in.shape
    tk = 128
    lanes_in = -(-d_in // GATE_W) * GATE_W
    assert (_OFF_NG - GL_NG) % GATE_W == 0 and _OFF_NG - GL_NG + GATE_W == lanes_in
    return pl.pallas_call(
        _pack_w_kernel,
        grid=(depth, d // tk),
        in_specs=[pl.BlockSpec((1, tk, lanes_in), lambda l, i: (l, i, 0))],
        out_specs=[pl.BlockSpec((1, tk, D_WIDE), lambda l, i: (l, i, 0)),
                   pl.BlockSpec((1, tk, GATE_W), lambda l, i: (l, i, 0))],
        out_shape=[jax.ShapeDtypeStruct((depth, d, D_WIDE), MXU_DTYPE),
                   jax.ShapeDtypeStruct((depth, d, GATE_W), MXU_DTYPE)],
        compiler_params=_cparams(("arbitrary", "arbitrary")),
        name="pack_w_in",
    )(w_in)


def _moe(x2d, norm_g, mod3, wg, bg, we, be, w1, w3, w2, layer, seq):
    n, d = x2d.shape
    n_route = N_GROUPS + N_EXPERTS
    w_router = jnp.concatenate([wg, we, jnp.zeros((d, GATE_W - n_route), wg.dtype)], axis=1).astype(MXU_DTYPE)
    b_router = jnp.concatenate([bg, be, jnp.zeros((GATE_W - n_route,), bg.dtype)]).reshape(1, GATE_W)
    h2, slab, cnt = _route(x2d, norm_g, mod3, w_router, b_router, seq)

    rows = EXPERT_ROWS
    counts = cnt[0, :N_EXPERTS].astype(jnp.int32)
    pcounts = (counts + rows - 1) // rows * rows
    pends = jnp.cumsum(pcounts)
    pstarts = pends - pcounts
    eid = slab[:, SLAB_E0:SLAB_E1 + 1].astype(jnp.int32)
    rank = slab[:, SLAB_R0:SLAB_R1 + 1].astype(jnp.int32)
    expert_ids = jnp.arange(N_EXPERTS, dtype=jnp.int32)
    dest = jnp.sum(jnp.where(eid[..., None] == expert_ids, pstarts, 0), axis=-1) + rank
    n_blocks = -(-2 * n // rows) + N_EXPERTS
    blk_row0 = jnp.arange(n_blocks, dtype=jnp.int32) * rows
    blk_e = jnp.minimum(jnp.sum((pends[None, :] <= blk_row0[:, None]).astype(jnp.int32), axis=1), N_EXPERTS - 1)
    n_used = (pends[-1:] // rows).astype(jnp.int32)

    cap = n_blocks * rows
    pad_lo = jnp.concatenate([pstarts + counts, pends[-1:]]).astype(jnp.int32)
    pad_hi = jnp.concatenate([pends, jnp.full((1,), cap, jnp.int32)]).astype(jnp.int32)
    inv = _invert(dest.reshape(-1), pad_lo, pad_hi, cap)
    yb = _experts(h2, inv, blk_e, n_used, w1, w3, w2, layer)
    return _combine(x2d, mod3, slab, dest[:, 0], dest[:, 1], yb, seq)


def _layer(x2d, mod, bsz, seq, layer, norm1_g, norm2_g, w_wide, w_gate, mlstm_gate_b, mlstm_conv_w, mlstm_out_g,
           moba_qk_g, ret_out_g, nsa_q_g, nsa_k_g, nsa_cmp_pos, nsa_cmp_w1, nsa_cmp_w2, w_out, router_g_w,
           router_g_b, router_e_w, router_e_b, exp_w1, exp_w3, exp_w2, slopes, log_gamma):
    n, d = x2d.shape
    mod3 = mod.reshape(bsz, 1, 6 * d)
    z, zg = _norm_in_proj(x2d, norm1_g, mod3, w_wide, layer, w_gate, seq)
    z3 = z.reshape(bsz, seq, D_WIDE)
    zg3 = zg.reshape(bsz, seq, GATE_W)
    y_m = _mlstm(z3, zg3, mlstm_gate_b, mlstm_conv_w, mlstm_out_g)
    y_b = _moba(z3, moba_qk_g, slopes[0::2])
    y_r = _retention(z3, log_gamma, ret_out_g)
    y_n = _nsa(z3, zg3, nsa_q_g, nsa_k_g, nsa_cmp_pos, nsa_cmp_w1, nsa_cmp_w2, slopes[1::2])
    x2d = _out_proj((y_m, y_b, y_r, y_n), w_out, layer, x2d, mod3, seq)
    return _moe(x2d, norm2_g, mod3, router_g_w, router_g_b, router_e_w, router_e_b, exp_w1, exp_w3, exp_w2,
                layer, seq)


def kernel(x, c, norm1_g, norm2_g, ada_w, ada_b, w_in, mlstm_gate_b, mlstm_conv_w, mlstm_out_g, moba_qk_g,
           ret_out_g, nsa_q_g, nsa_k_g, nsa_cmp_pos, nsa_cmp_w1, nsa_cmp_w2, w_out, router_g_w, router_g_b,
           router_e_w, router_e_b, exp_w1, exp_w3, exp_w2):
    bsz, seq, d = x.shape
    depth = ada_w.shape[0]
    n_softmax_heads = 2 * HEADS
    slopes = jnp.exp2(-8.0 * jnp.arange(1, n_softmax_heads + 1, dtype=F32) / n_softmax_heads)
    log_gamma = jnp.log(1.0 - jnp.exp2(-5.0 - jnp.arange(HEADS, dtype=F32)))
    mod = _ada_mod(c, ada_w, ada_b)
    w_wide, w_gate = _pack_w_in(w_in)
    w_out_b = w_out.astype(MXU_DTYPE)
    x2d = x.reshape(bsz * seq, d)
    for l in range(depth):
        x2d = _layer(x2d, mod[l], bsz, seq, l, norm1_g[l], norm2_g[l], w_wide, w_gate, mlstm_gate_b[l],
                     mlstm_conv_w[l], mlstm_out_g[l], moba_qk_g[l], ret_out_g[l], nsa_q_g[l], nsa_k_g[l],
                     nsa_cmp_pos[l], nsa_cmp_w1[l], nsa_cmp_w2[l], w_out_b, router_g_w[l], router_g_b[l],
                     router_e_w[l], router_e_b[l], exp_w1, exp_w3, exp_w2, slopes, log_gamma)
    return x2d.reshape(bsz, seq, d)
```
